```python
import jax, jax.numpy as jnp
from jax import lax
import numpy as np

D_MODEL = 2048
BATCH = 4
SEQ = 2048
DEPTH = 2
DEC_BATCH = 128
DEC_SEQ = 1
PAST_LEN = 8192
PAGE_SIZE = 128

HEAD_DIM = 64
ATTN_DIM = D_MODEL // 2
N_Q_HEADS = ATTN_DIM // HEAD_DIM
N_KV_HEADS = N_Q_HEADS // 4
GROUP = N_Q_HEADS // N_KV_HEADS
KV_DIM = N_KV_HEADS * HEAD_DIM
CONV_DIM = D_MODEL - ATTN_DIM
CONV_GROUPS = CONV_DIM // HEAD_DIM
CONV_WIDTH = 3
MIX_DIM = ATTN_DIM + CONV_DIM
IN_DIM = ATTN_DIM + 2 * KV_DIM + 3 * CONV_DIM
SPLITS = (ATTN_DIM, ATTN_DIM + KV_DIM, ATTN_DIM + 2 * KV_DIM,
          ATTN_DIM + 2 * KV_DIM + CONV_DIM, ATTN_DIM + 2 * KV_DIM + 2 * CONV_DIM)
WINDOW = 128
BLOCK = 128
ROPE_THETA = 500000.0
ROT_DIM = HEAD_DIM // 4
N_META = 16
D_FF = 7 * D_MODEL // 2
N_EXPERTS = 8
TOP_K = 2
N_DENSE = (DEPTH + 1) // 2
N_MOE = DEPTH // 2
EPS = 1e-5

kernel_name = "hymba_swa_sink_shortconv_moe_step"


def rmsnorm(x, g):
    xf = x.astype(jnp.float32)
    xf = xf * lax.rsqrt(jnp.mean(xf * xf, axis=-1, keepdims=True) + EPS)
    return xf.astype(x.dtype) * g


def head_rmsnorm(x, g, n_groups):
    shp = x.shape
    xg = x.reshape(shp[:-1] + (n_groups, shp[-1] // n_groups))
    return rmsnorm(xg, g.reshape(n_groups, shp[-1] // n_groups)).reshape(shp)


def partial_rope(x, pos):
    half = ROT_DIM // 2
    inv_freq = jnp.power(ROPE_THETA, -jnp.arange(half, dtype=jnp.float32) * 2.0 / ROT_DIM)
    ang = pos.astype(jnp.float32)[:, None] * inv_freq[None, :]
    cos = jnp.cos(ang)[:, None, :]
    sin = jnp.sin(ang)[:, None, :]
    xf = x.astype(jnp.float32)
    x1 = xf[..., :half]
    x2 = xf[..., half:ROT_DIM]
    out = jnp.concatenate([x1 * cos - x2 * sin, x2 * cos + x1 * sin, xf[..., ROT_DIM:]], axis=-1)
    return out.astype(x.dtype)


def window_valid(qpos, kpos):
    d = qpos[..., :, None] - kpos[..., None, :]
    return (d >= 0) & (d <= WINDOW) & (kpos[..., None, :] >= 0)


def sink_attention(q, k, v, valid, sinks):
    s = jnp.einsum('...qkgd,...skd->...kgqs', q.astype(jnp.float32), k.astype(jnp.float32)) * (HEAD_DIM ** -0.5)
    s = jnp.where(valid, s, -jnp.inf)
    sink = sinks.astype(jnp.float32).reshape(N_KV_HEADS, GROUP, 1)
    m = jnp.maximum(jnp.max(s, axis=-1), sink)
    p = jnp.exp(s - m[..., None])
    denom = jnp.sum(p, axis=-1) + jnp.exp(sink - m)
    out = jnp.einsum('...kgqs,...skd->...qkgd', p / denom[..., None], v.astype(jnp.float32))
    return out.astype(q.dtype)


def prompt_window_attention(q, k, v, sinks):
    Bn, L = q.shape[0], q.shape[1]
    pos = jnp.arange(L, dtype=jnp.int32)
    q = partial_rope(q, pos)
    k = partial_rope(k, pos)
    pad = (-N_META) % BLOCK
    nb = (L + pad) // BLOCK
    padw = ((0, 0), (pad, 0), (0, 0), (0, 0))
    qb = jnp.pad(q, padw).reshape(Bn, nb, BLOCK, N_KV_HEADS, GROUP, HEAD_DIM)
    kb = jnp.pad(k, padw).reshape(Bn, nb, BLOCK, N_KV_HEADS, HEAD_DIM)
    vb = jnp.pad(v, padw).reshape(Bn, nb, BLOCK, N_KV_HEADS, HEAD_DIM)

    def with_prev(xb):
        prev = jnp.concatenate([jnp.zeros_like(xb[:, :1]), xb[:, :-1]], axis=1)
        return jnp.concatenate([prev, xb], axis=2)

    qpos = (jnp.arange(nb * BLOCK, dtype=jnp.int32) - pad).reshape(nb, BLOCK)
    kpos = jnp.concatenate([qpos - BLOCK, qpos], axis=1)
    valid = window_valid(qpos, kpos)[None, :, None, None]
    out = sink_attention(qb, with_prev(kb), with_prev(vb), valid, sinks)
    out = out.reshape(Bn, nb * BLOCK, ATTN_DIM)[:, pad:]
    return out, k[:, L - WINDOW:], v[:, L - WINDOW:]


def sample_window_attention(q, k, v, sinks, cache_k_l, cache_v_l):
    Dn, T = q.shape[0], q.shape[1]
    W = cache_k_l.shape[1]
    qpos = PAST_LEN + jnp.arange(T, dtype=jnp.int32)
    q = partial_rope(q, qpos)
    k = partial_rope(k, qpos)
    k_all = jnp.concatenate([cache_k_l, k.astype(cache_k_l.dtype)], axis=1)
    v_all = jnp.concatenate([cache_v_l, v.astype(cache_v_l.dtype)], axis=1)
    kpos = PAST_LEN - W + jnp.arange(W + T, dtype=jnp.int32)
    valid = window_valid(qpos, kpos)[None, None, None]
    qg = q.reshape(Dn, T, N_KV_HEADS, GROUP, HEAD_DIM)
    out = sink_attention(qg, k_all, v_all, valid, sinks).reshape(Dn, T, ATTN_DIM)
    return out, k_all[:, T:], v_all[:, T:]


def causal_short_conv(u, w, prev):
    T = u.shape[1]
    full = jnp.concatenate([prev.astype(u.dtype), u], axis=1)
    y = full[:, 0:T] * w[0]
    for i in range(1, CONV_WIDTH):
        y = y + full[:, i:i + T] * w[i]
    return y, full[:, T:]


def project_in(n, w_in_l):
    z = n @ w_in_l
    q, k, v, gb, gc, xv = jnp.split(z, SPLITS, axis=-1)
    Bn, T = n.shape[0], n.shape[1]
    return (q.reshape(Bn, T, N_Q_HEADS, HEAD_DIM),
            k.reshape(Bn, T, N_KV_HEADS, HEAD_DIM),
            v.reshape(Bn, T, N_KV_HEADS, HEAD_DIM),
            gb, gc * xv)


def project_out(attn, conv_y, gb, g_attn_l, g_conv_l, w_out_l):
    a = head_rmsnorm(attn, g_attn_l, N_Q_HEADS)
    c = head_rmsnorm(gb * conv_y, g_conv_l, CONV_GROUPS)
    return jnp.concatenate([a, c], axis=-1) @ w_out_l


def swiglu(x, w1, w3, w2):
    return (jax.nn.silu(x @ w1) * (x @ w3)) @ w2


def moe_swiglu(x, router, w1, w3, w2):
    logits = (x @ router).astype(jnp.float32)
    topv, topi = lax.top_k(logits, TOP_K)
    gates = jax.nn.softmax(topv, axis=-1)
    combine = jnp.sum(jax.nn.one_hot(topi, N_EXPERTS, dtype=jnp.float32) * gates[..., None], axis=-2)
    combine = combine.astype(x.dtype)
    out = jnp.zeros_like(x)
    for e in range(N_EXPERTS):
        out = out + combine[..., e:e + 1] * swiglu(x, w1[e], w3[e], w2[e])
    return out


def setup_inputs(seed: int = 0) -> dict:
    key = jax.random.key(seed)
    ks = jax.random.split(key, 24)
    f32 = jnp.float32
    W = min(WINDOW, PAST_LEN)

    def nrm(k, shape, scale):
        return jax.random.normal(k, shape, f32) * scale

    def gain(k, shape):
        return 1.0 + 0.02 * jax.random.normal(k, shape, f32)

    return {
        "x_prompt": nrm(ks[0], (BATCH, SEQ, D_MODEL), 1.0),
        "x_sample": nrm(ks[1], (DEC_BATCH, DEC_SEQ, D_MODEL), 1.0),
        "cache_k": nrm(ks[2], (DEPTH, DEC_BATCH, W, N_KV_HEADS, HEAD_DIM), 1.0),
        "cache_v": nrm(ks[3], (DEPTH, DEC_BATCH, W, N_KV_HEADS, HEAD_DIM), 1.0),
        "state_conv": nrm(ks[4], (DEPTH, DEC_BATCH, CONV_WIDTH - 1, CONV_DIM), 1.0),
        "meta_tokens": nrm(ks[5], (N_META, D_MODEL), 1.0),
        "g_mix": gain(ks[6], (DEPTH, D_MODEL)),
        "w_in": nrm(ks[7], (DEPTH, D_MODEL, IN_DIM), D_MODEL ** -0.5),
        "conv_w": nrm(ks[8], (DEPTH, CONV_WIDTH, CONV_DIM), CONV_WIDTH ** -0.5),
        "attn_sinks": nrm(ks[9], (DEPTH, N_Q_HEADS), 1.0),
        "g_attn_out": gain(ks[10], (DEPTH, ATTN_DIM)),
        "g_conv_out": gain(ks[11], (DEPTH, CONV_DIM)),
        "w_out": nrm(ks[12], (DEPTH, MIX_DIM, D_MODEL), MIX_DIM ** -0.5),
        "g_ffn": gain(ks[13], (DEPTH, D_MODEL)),
        "dense_w1": nrm(ks[14], (N_DENSE, D_MODEL, D_FF), D_MODEL ** -0.5),
        "dense_w3": nrm(ks[15], (N_DENSE, D_MODEL, D_FF), D_MODEL ** -0.5),
        "dense_w2": nrm(ks[16], (N_DENSE, D_FF, D_MODEL), D_FF ** -0.5),
        "moe_router": nrm(ks[17], (N_MOE, D_MODEL, N_EXPERTS), D_MODEL ** -0.5),
        "moe_w1": nrm(ks[18], (N_MOE, N_EXPERTS, D_MODEL, D_FF), D_MODEL ** -0.5),
        "moe_w3": nrm(ks[19], (N_MOE, N_EXPERTS, D_MODEL, D_FF), D_MODEL ** -0.5),
        "moe_w2": nrm(ks[20], (N_MOE, N_EXPERTS, D_FF, D_MODEL), D_FF ** -0.5),
        "g_final": gain(ks[21], (D_MODEL,)),
    }


def reference(x_prompt, x_sample, cache_k, cache_v, state_conv, meta_tokens, g_mix, w_in, conv_w,
              attn_sinks, g_attn_out, g_conv_out, w_out, g_ffn, dense_w1, dense_w3, dense_w2,
              moe_router, moe_w1, moe_w3, moe_w2, g_final):
    Bp = x_prompt.shape[0]
    meta = jnp.broadcast_to(meta_tokens.astype(x_prompt.dtype)[None], (Bp, N_META, D_MODEL))
    h_p = jnp.concatenate([meta, x_prompt], axis=1)
    h_s = x_sample
    nk_p, nv_p, nc_p, nk_s, nv_s, nc_s = [], [], [], [], [], []
    for l in range(DEPTH):
        n_p = rmsnorm(h_p, g_mix[l])
        q, k, v, gb, u = project_in(n_p, w_in[l])
        a_p, kr, vr = prompt_window_attention(q, k, v, attn_sinks[l])
        c_p, cs = causal_short_conv(u, conv_w[l], jnp.zeros((Bp, CONV_WIDTH - 1, CONV_DIM), u.dtype))
        h_p = h_p + project_out(a_p, c_p, gb, g_attn_out[l], g_conv_out[l], w_out[l])
        nk_p.append(kr); nv_p.append(vr); nc_p.append(cs)

        n_s = rmsnorm(h_s, g_mix[l])
        q, k, v, gb, u = project_in(n_s, w_in[l])
        a_s, kr, vr = sample_window_attention(q, k, v, attn_sinks[l], cache_k[l], cache_v[l])
        c_s, cs = causal_short_conv(u, conv_w[l], state_conv[l])
        h_s = h_s + project_out(a_s, c_s, gb, g_attn_out[l], g_conv_out[l], w_out[l])
        nk_s.append(kr); nv_s.append(vr); nc_s.append(cs)

        n_p = rmsnorm(h_p, g_ffn[l])
        n_s = rmsnorm(h_s, g_ffn[l])
        j = l // 2
        if l % 2 == 0:
            h_p = h_p + swiglu(n_p, dense_w1[j], dense_w3[j], dense_w2[j])
            h_s = h_s + swiglu(n_s, dense_w1[j], dense_w3[j], dense_w2[j])
        else:
            h_p = h_p + moe_swiglu(n_p, moe_router[j], moe_w1[j], moe_w3[j], moe_w2[j])
            h_s = h_s + moe_swiglu(n_s, moe_router[j], moe_w1[j], moe_w3[j], moe_w2[j])

    y_prompt = rmsnorm(h_p, g_final)[:, N_META:]
    y_sample = rmsnorm(h_s, g_final)
    return (y_prompt, y_sample, jnp.stack(nk_p), jnp.stack(nv_p), jnp.stack(nc_p),
            jnp.stack(nk_s), jnp.stack(nv_s), jnp.stack(nc_s))
```

```python
import functools

import jax
import jax.numpy as jnp
import numpy as np
from jax import lax
from jax.experimental import pallas as pl
from jax.experimental.pallas import tpu as pltpu

D_MODEL = 2048
BATCH = 4
SEQ = 2048
DEPTH = 2
DEC_BATCH = 128
PAST_LEN = 8192
HEAD_DIM = 64
ATTN_DIM = D_MODEL // 2
N_Q_HEADS = ATTN_DIM // HEAD_DIM
N_KV_HEADS = N_Q_HEADS // 4
GROUP = N_Q_HEADS // N_KV_HEADS
KV_DIM = N_KV_HEADS * HEAD_DIM
CONV_DIM = D_MODEL - ATTN_DIM
CONV_WIDTH = 3
IN_DIM = ATTN_DIM + 2 * KV_DIM + 3 * CONV_DIM
WINDOW = 128
ROPE_THETA = 500000.0
ROT_DIM = HEAD_DIM // 4
N_META = 16
D_FF = 7 * D_MODEL // 2
N_EXPERTS = 8
EPS = 1e-5

ROW_BLOCK = 128
N_PROMPT_ROWS = BATCH * SEQ
SAMPLE_ROW0 = N_PROMPT_ROWS
TAIL_ROW0 = SAMPLE_ROW0 + DEC_BATCH
N_ROWS = TAIL_ROW0 + ROW_BLOCK
N_ROW_BLOCKS = N_ROWS // ROW_BLOCK
BLOCKS_PER_SEQ = SEQ // ROW_BLOCK
SAMPLE_BLOCK = SAMPLE_ROW0 // ROW_BLOCK
META_BLOCK = TAIL_ROW0 // ROW_BLOCK
META_ROW_IN_BLOCK = ROW_BLOCK - N_META
SEG = 256

VMEM_LIMIT_V7X = 56 * 1024 * 1024

bf16 = jnp.bfloat16
f32 = jnp.float32


def _cparams(*sem):
    return pltpu.CompilerParams(dimension_semantics=sem, vmem_limit_bytes=VMEM_LIMIT_V7X)


def _rms(x, g):
    ms = jnp.mean(x * x, axis=-1, keepdims=True)
    return (x * lax.rsqrt(ms + EPS)) * g


def _segsum(x, bd):
    x1 = x.astype(bf16)
    r1 = x - x1.astype(f32)
    x2 = r1.astype(bf16)
    x3 = (r1 - x2.astype(f32)).astype(bf16)
    d = functools.partial(jnp.dot, preferred_element_type=f32)
    return d(x1, bd) + d(x2, bd) + d(x3, bd)


def _rope(x, cos_t, sin_t):
    n = x.shape[-1]
    lane = lax.broadcasted_iota(jnp.int32, x.shape, x.ndim - 1) % HEAD_DIM
    nxt = pltpu.roll(x, n - ROT_DIM // 2, axis=x.ndim - 1)
    prv = pltpu.roll(x, ROT_DIM // 2, axis=x.ndim - 1)
    return x * cos_t + jnp.where(lane < ROT_DIM // 2, nxt, prv) * sin_t


def _norm_kernel(x_ref, g_ref, o_ref):
    o_ref[...] = _rms(x_ref[...], g_ref[...]).astype(o_ref.dtype)


def _norm_rows(x, g, tm=768):
    return pl.pallas_call(
        _norm_kernel,
        grid=(N_ROWS // tm,),
        in_specs=[pl.BlockSpec((tm, D_MODEL), lambda i: (i, 0)),
                  pl.BlockSpec((1, D_MODEL), lambda i: (0, 0))],
        out_specs=pl.BlockSpec((tm, D_MODEL), lambda i: (i, 0)),
        out_shape=jax.ShapeDtypeStruct((N_ROWS, D_MODEL), bf16),
        compiler_params=_cparams("parallel"),
    )(x, g)


def _inproj_kernel(x_ref, w_ref, o_ref, wbf_ref):
    @pl.when(pl.program_id(1) == 0)
    def _():
        wbf_ref[...] = w_ref[...].astype(bf16)

    o_ref[...] = jnp.dot(x_ref[...], wbf_ref[...], preferred_element_type=f32)


def _in_proj(xn, w_in, layer, tm=768, tn=768):
    return pl.pallas_call(
        _inproj_kernel,
        grid=(IN_DIM // tn, N_ROWS // tm),
        in_specs=[pl.BlockSpec((tm, D_MODEL), lambda j, i: (i, 0)),
                  pl.BlockSpec((None, D_MODEL, tn), lambda j, i: (layer, 0, j))],
        out_specs=pl.BlockSpec((tm, tn), lambda j, i: (i, j)),
        out_shape=jax.ShapeDtypeStruct((N_ROWS, IN_DIM), f32),
        scratch_shapes=[pltpu.VMEM((D_MODEL, tn), bf16)],
        compiler_params=_cparams("parallel", "arbitrary"),
    )(xn, w_in)


def _attn_prompt_kernel(sink_ref, q_ref, kc_ref, vc_ref, kp_ref, vp_ref, tc_ref, tp_ref, g_ref,
                        an_ref, krot_ref):
    s = pl.program_id(0)
    is_meta = s == N_ROW_BLOCKS - 2
    j = s % BLOCKS_PER_SEQ
    tc = tc_ref[...]
    tp = tp_ref[...]
    q = _rope(q_ref[...], jnp.tile(tc[:, :128], (1, 8)), jnp.tile(tc[:, 128:], (1, 8)))
    kc = _rope(kc_ref[...], jnp.tile(tc[:, :128], (1, 2)), jnp.tile(tc[:, 128:], (1, 2)))
    kp = _rope(kp_ref[...], jnp.tile(tp[:, :128], (1, 2)), jnp.tile(tp[:, 128:], (1, 2)))
    krot_ref[...] = kc
    kall = jnp.concatenate([kp, kc], axis=0).astype(bf16)
    vall = jnp.concatenate([vp_ref[...], vc_ref[...]], axis=0).astype(bf16)

    row = lax.broadcasted_iota(jnp.int32, (GROUP * ROW_BLOCK, 2 * ROW_BLOCK), 0) % ROW_BLOCK
    col = lax.broadcasted_iota(jnp.int32, (GROUP * ROW_BLOCK, 2 * ROW_BLOCK), 1)
    lo_prev = jnp.where(is_meta, ROW_BLOCK, jnp.where(j == 0, META_ROW_IN_BLOCK, 0))
    lo_cur = jnp.where(is_meta, META_ROW_IN_BLOCK, 0)
    ccur = col - ROW_BLOCK
    valid4 = (((col < ROW_BLOCK) & (col >= row) & (col >= lo_prev))
              | ((ccur >= 0) & (ccur <= row) & (ccur >= lo_cur)))
    half = lax.broadcasted_iota(jnp.int32, (ROW_BLOCK, 128), 1) // HEAD_DIM
    half4 = lax.broadcasted_iota(jnp.int32, (GROUP * ROW_BLOCK, 128), 1) // HEAD_DIM
    g_all = g_ref[...]

    for kvh in range(N_KV_HEADS):
        kside = kvh % 2
        kpair = kall[:, 128 * (kvh // 2):128 * (kvh // 2) + 128]
        vpair = vall[:, 128 * (kvh // 2):128 * (kvh // 2) + 128]
        qs, sinks = [], []
        for g in range(GROUP):
            h = GROUP * kvh + g
            x = q[:, 128 * (h // 2):128 * (h // 2) + 128]
            if h % 2 != kside:
                x = pltpu.roll(x, HEAD_DIM, axis=1)
            qs.append(jnp.where(half == kside, x, 0.0))
            sinks.append(jnp.full((ROW_BLOCK, 1), sink_ref[h], f32))
        q4 = jnp.concatenate(qs, axis=0).astype(bf16)
        sink = jnp.concatenate(sinks, axis=0)
        sc = lax.dot_general(q4, kpair, (((1,), (1,)), ((), ())), preferred_element_type=f32)
        sc = jnp.where(valid4, sc * (HEAD_DIM ** -0.5), -jnp.inf)
        m = jnp.maximum(jnp.max(sc, axis=-1, keepdims=True), sink)
        p = jnp.exp(sc - m)
        denom = jnp.sum(p, axis=-1, keepdims=True) + jnp.exp(sink - m)
        o = jnp.dot(p.astype(bf16), vpair, preferred_element_type=f32) / denom
        ms = jnp.sum(jnp.where(half4 == kside, o * o, 0.0), axis=-1, keepdims=True) * (1.0 / HEAD_DIM)
        on = o * lax.rsqrt(ms + EPS)
        for pair in range(2):
            parts = []
            for side in range(2):
                g = 2 * pair + side
                x = on[ROW_BLOCK * g:ROW_BLOCK * (g + 1)]
                if side != kside:
                    x = pltpu.roll(x, HEAD_DIM, axis=1)
                parts.append(x)
            blk = 2 * kvh + pair
            out = jnp.where(half == 0, parts[0], parts[1]) * g_all[:, 128 * blk:128 * blk + 128]
            an_ref[:, 128 * blk:128 * blk + 128] = out.astype(an_ref.dtype)


def _cur_block(s):
    return jnp.where(s == N_ROW_BLOCKS - 2, META_BLOCK, s)


def _prev_block(s):
    return jnp.where((s % BLOCKS_PER_SEQ == 0) | (s == N_ROW_BLOCKS - 2), META_BLOCK, s - 1)


def _cur_tab(s):
    return jnp.where(s == N_ROW_BLOCKS - 2, BLOCKS_PER_SEQ, s % BLOCKS_PER_SEQ)


def _prev_tab(s):
    return jnp.where((s % BLOCKS_PER_SEQ == 0) | (s == N_ROW_BLOCKS - 2), BLOCKS_PER_SEQ, s % BLOCKS_PER_SEQ - 1)


def _attn_prompt(z, sinks, rope_tab, g_attn):
    kcol, vcol = ATTN_DIM // KV_DIM, ATTN_DIM // KV_DIM + 1
    return pl.pallas_call(
        _attn_prompt_kernel,
        grid=(N_ROW_BLOCKS - 1,),
        in_specs=[pl.BlockSpec(memory_space=pltpu.SMEM),
                  pl.BlockSpec((ROW_BLOCK, ATTN_DIM), lambda s: (_cur_block(s), 0)),
                  pl.BlockSpec((ROW_BLOCK, KV_DIM), lambda s: (_cur_block(s), kcol)),
                  pl.BlockSpec((ROW_BLOCK, KV_DIM), lambda s: (_cur_block(s), vcol)),
                  pl.BlockSpec((ROW_BLOCK, KV_DIM), lambda s: (_prev_block(s), kcol)),
                  pl.BlockSpec((ROW_BLOCK, KV_DIM), lambda s: (_prev_block(s), vcol)),
                  pl.BlockSpec((ROW_BLOCK, 256), lambda s: (_cur_tab(s), 0)),
                  pl.BlockSpec((ROW_BLOCK, 256), lambda s: (_prev_tab(s), 0)),
                  pl.BlockSpec((1, ATTN_DIM), lambda s: (0, 0))],
        out_specs=[pl.BlockSpec((ROW_BLOCK, ATTN_DIM), lambda s: (_cur_block(s), 0)),
                   pl.BlockSpec((ROW_BLOCK, KV_DIM), lambda s: (_cur_block(s), 0))],
        out_shape=[jax.ShapeDtypeStruct((N_ROWS, ATTN_DIM), bf16),
                   jax.ShapeDtypeStruct((N_ROWS, KV_DIM), f32)],
        compiler_params=_cparams("arbitrary"),
    )(sinks, z, z, z, z, z, rope_tab, rope_tab, g_attn)


def _attn_sample_kernel(an_in_ref, q_ref, k_ref, v_ref, ck_ref, cv_ref, tab_ref, sink_ref, g_ref, bd_ref,
                        an_ref, nk_ref, nv_ref):
    del an_in_ref
    bt = q_ref.shape[0]
    w = ck_ref.shape[1]
    tab = tab_ref[...]
    q = _rope(q_ref[...], jnp.tile(tab[0:1, :128], (1, 8)), jnp.tile(tab[0:1, 128:], (1, 8)))
    k = _rope(k_ref[...], jnp.tile(tab[0:1, :128], (1, 2)), jnp.tile(tab[0:1, 128:], (1, 2)))
    v = v_ref[...]
    ck = ck_ref[...]
    cv = cv_ref[...]
    bd = bd_ref[...]

    last = lax.broadcasted_iota(jnp.int32, (w, KV_DIM), 0) == w - 1
    for b in range(bt):
        nk_ref[b] = jnp.where(last, k[b:b + 1], pltpu.roll(ck[b], w - 1, axis=0))
        nv_ref[b] = jnp.where(last, v[b:b + 1], pltpu.roll(cv[b], w - 1, axis=0))

    seg = lax.broadcasted_iota(jnp.int32, (bt, SEG), 1) // HEAD_DIM
    scale = HEAD_DIM ** -0.5
    chunks = [jnp.zeros((bt, SEG), f32) for _ in range(N_KV_HEADS)]
    for g in range(GROUP):
        qg = jnp.zeros((bt, SEG), f32)
        for kvh in range(N_KV_HEADS):
            x = q[:, SEG * kvh:SEG * (kvh + 1)]
            sh = (HEAD_DIM * (kvh - g)) % SEG
            if sh:
                x = pltpu.roll(x, sh, axis=1)
            qg = jnp.where(seg == kvh, x, qg)
        prod = (ck * qg[:, None, :]).reshape(bt * w, SEG)
        sc = (_segsum(prod, bd) * scale).reshape(bt, w, SEG)
        sn = _segsum(qg * k, bd) * scale
        sink = sink_ref[g:g + 1, :]
        m = jnp.maximum(jnp.maximum(jnp.max(sc, axis=1), sn), sink)
        p = jnp.exp(sc - m[:, None, :])
        pn = jnp.exp(sn - m)
        denom = jnp.sum(p, axis=1) + pn + jnp.exp(sink - m)
        o = (jnp.sum(p * cv, axis=1) + pn * v) / denom
        ms = _segsum(o * o, bd) * (1.0 / HEAD_DIM)
        on = o * lax.rsqrt(ms + EPS)
        for kvh in range(N_KV_HEADS):
            sh = (HEAD_DIM * (g - kvh)) % SEG
            x = pltpu.roll(on, sh, axis=1) if sh else on
            chunks[kvh] = jnp.where(seg == g, x, chunks[kvh])
    out = jnp.concatenate(chunks, axis=1) * g_ref[...]
    an_ref[...] = out.astype(an_ref.dtype)


def _attn_sample(an, z, cache_k_l, cache_v_l, tab, sinks_exp, g_attn, bd, bt=16):
    w = cache_k_l.shape[1]
    row0 = SAMPLE_ROW0 // bt
    kcol, vcol = ATTN_DIM // KV_DIM, ATTN_DIM // KV_DIM + 1
    cache_spec = pl.BlockSpec((bt, w, KV_DIM), lambda i: (i, 0, 0))
    return pl.pallas_call(
        _attn_sample_kernel,
        grid=(DEC_BATCH // bt,),
        in_specs=[pl.BlockSpec(memory_space=pl.ANY),
                  pl.BlockSpec((bt, ATTN_DIM), lambda i: (row0 + i, 0)),
                  pl.BlockSpec((bt, KV_DIM), lambda i: (row0 + i, kcol)),
                  pl.BlockSpec((bt, KV_DIM), lambda i: (row0 + i, vcol)),
                  cache_spec, cache_spec,
                  pl.BlockSpec((8, 256), lambda i: (0, 0)),
                  pl.BlockSpec((GROUP, SEG), lambda i: (0, 0)),
                  pl.BlockSpec((1, ATTN_DIM), lambda i: (0, 0)),
                  pl.BlockSpec((SEG, SEG), lambda i: (0, 0))],
        out_specs=[pl.BlockSpec((bt, ATTN_DIM), lambda i: (row0 + i, 0)), cache_spec, cache_spec],
        out_shape=[jax.ShapeDtypeStruct((N_ROWS, ATTN_DIM), bf16),
                   jax.ShapeDtypeStruct(cache_k_l.shape, f32),
                   jax.ShapeDtypeStruct(cache_v_l.shape, f32)],
        input_output_aliases={0: 0},
        compiler_params=_cparams("parallel"),
    )(an, z, z, z, cache_k_l, cache_v_l, tab, sinks_exp, g_attn, bd)


def _conv_kernel(z1_ref, z2_ref, p1_ref, p2_ref, s0_ref, s1_ref, w_ref, g_ref, bd_ref, c_ref, u_ref):
    i = pl.program_id(0)

    def split(b1, b2):
        gb = b1[:, :CONV_DIM]
        gc = jnp.concatenate([b1[:, CONV_DIM:], b2[:, :CONV_DIM // 2]], axis=1)
        return gb, gc * b2[:, CONV_DIM // 2:]

    gb, u = split(z1_ref[...], z2_ref[...])
    _, up = split(p1_ref[...], p2_ref[...])
    u_ref[...] = u
    row = lax.broadcasted_iota(jnp.int32, u.shape, 0)
    u1 = jnp.where(row == 0, up[7:8], pltpu.roll(u, 1, axis=0))
    u2 = jnp.where(row == 0, up[6:7], jnp.where(row == 1, up[7:8], pltpu.roll(u, 2, axis=0)))
    is_sample = i == SAMPLE_BLOCK
    u1 = jnp.where(is_sample, s1_ref[...], u1)
    u2 = jnp.where(is_sample, s0_ref[...], u2)
    w = w_ref[...]
    y = u2 * w[0:1] + u1 * w[1:2] + u * w[2:3]
    t = gb * y
    bd = bd_ref[...]
    g = g_ref[...]
    for c in range(CONV_DIM // SEG):
        tc = t[:, SEG * c:SEG * (c + 1)]
        ms = _segsum(tc * tc, bd) * (1.0 / HEAD_DIM)
        c_ref[:, SEG * c:SEG * (c + 1)] = (tc * lax.rsqrt(ms + EPS) * g[:, SEG * c:SEG * (c + 1)]).astype(c_ref.dtype)


def _conv_prev_block(i):
    last8_of_meta = N_ROWS // 8 - 1
    return jnp.where(i % BLOCKS_PER_SEQ == 0, last8_of_meta, (ROW_BLOCK // 8) * i - 1)


def _conv_mix(z, state_l, conv_w_l, g_conv, bd):
    wide = 3 * CONV_DIM // 2
    return pl.pallas_call(
        _conv_kernel,
        grid=(N_ROW_BLOCKS,),
        in_specs=[pl.BlockSpec((ROW_BLOCK, wide), lambda i: (i, 1)),
                  pl.BlockSpec((ROW_BLOCK, wide), lambda i: (i, 2)),
                  pl.BlockSpec((8, wide), lambda i: (_conv_prev_block(i), 1)),
                  pl.BlockSpec((8, wide), lambda i: (_conv_prev_block(i), 2)),
                  pl.BlockSpec((DEC_BATCH, CONV_DIM), lambda i: (0, 0)),
                  pl.BlockSpec((DEC_BATCH, CONV_DIM), lambda i: (0, 1)),
                  pl.BlockSpec((CONV_WIDTH, CONV_DIM), lambda i: (0, 0)),
                  pl.BlockSpec((1, CONV_DIM), lambda i: (0, 0)),
                  pl.BlockSpec((SEG, SEG), lambda i: (0, 0))],
        out_specs=[pl.BlockSpec((ROW_BLOCK, CONV_DIM), lambda i: (i, 0)),
                   pl.BlockSpec((ROW_BLOCK, CONV_DIM), lambda i: (i, 0))],
        out_shape=[jax.ShapeDtypeStruct((N_ROWS, CONV_DIM), bf16),
                   jax.ShapeDtypeStruct((N_ROWS, CONV_DIM), f32)],
        compiler_params=_cparams("parallel"),
    )(z, z, z, z, state_l, state_l, conv_w_l, g_conv, bd)


def _outproj_kernel(a_ref, c_ref, wa_ref, wc_ref, h_ref, g_ref, ho_ref, no_ref):
    acc = jnp.dot(a_ref[...], wa_ref[...], preferred_element_type=f32)
    acc += jnp.dot(c_ref[...], wc_ref[...], preferred_element_type=f32)
    hn = h_ref[...] + acc
    ho_ref[...] = hn
    no_ref[...] = _rms(hn, g_ref[...]).astype(no_ref.dtype)


def _out_proj(an, c, w_out_bf, layer, h, g_next, tm=384):
    row = lambda i: (i, 0)
    return pl.pallas_call(
        _outproj_kernel,
        grid=(N_ROWS // tm,),
        in_specs=[pl.BlockSpec((tm, ATTN_DIM), row),
                  pl.BlockSpec((tm, CONV_DIM), row),
                  pl.BlockSpec((None, ATTN_DIM, D_MODEL), lambda i: (layer, 0, 0)),
                  pl.BlockSpec((None, CONV_DIM, D_MODEL), lambda i: (layer, 1, 0)),
                  pl.BlockSpec((tm, D_MODEL), row),
                  pl.BlockSpec((1, D_MODEL), lambda i: (0, 0))],
        out_specs=[pl.BlockSpec((tm, D_MODEL), row), pl.BlockSpec((tm, D_MODEL), row)],
        out_shape=[jax.ShapeDtypeStruct((N_ROWS, D_MODEL), f32),
                   jax.ShapeDtypeStruct((N_ROWS, D_MODEL), bf16)],
        compiler_params=_cparams("parallel"),
    )(an, c, w_out_bf, w_out_bf, h, g_next)


def _router_kernel(h_ref, g_ref, r_ref, cmb_ref):
    n = _rms(h_ref[...], g_ref[...])
    logits = jnp.dot(n, r_ref[...], preferred_element_type=f32, precision=lax.Precision.HIGHEST)
    idx = lax.broadcasted_iota(jnp.int32, logits.shape, 1)
    m1 = jnp.max(logits, axis=-1, keepdims=True)
    i1 = jnp.min(jnp.where(logits == m1, idx, N_EXPERTS), axis=-1, keepdims=True)
    rest = jnp.where(idx == i1, -jnp.inf, logits)
    m2 = jnp.max(rest, axis=-1, keepdims=True)
    i2 = jnp.min(jnp.where(rest == m2, idx, N_EXPERTS), axis=-1, keepdims=True)
    e2 = jnp.exp(m2 - m1)
    den = 1.0 + e2
    cmb_ref[...] = jnp.where(idx == i1, 1.0 / den, jnp.where(idx == i2, e2 / den, 0.0))


def _router(h, g, router, tm=768):
    return pl.pallas_call(
        _router_kernel,
        grid=(N_ROWS // tm,),
        in_specs=[pl.BlockSpec((tm, D_MODEL), lambda i: (i, 0)),
                  pl.BlockSpec((1, D_MODEL), lambda i: (0, 0)),
                  pl.BlockSpec((D_MODEL, N_EXPERTS), lambda i: (0, 0))],
        out_specs=pl.BlockSpec((tm, N_EXPERTS), lambda i: (i, 0)),
        out_shape=jax.ShapeDtypeStruct((N_ROWS, N_EXPERTS), f32),
        compiler_params=_cparams("parallel"),
    )(h, g, router)


def _ffn_kernel(x_ref, w1_ref, w3_ref, w2_ref, cmb_ref, h_ref, g_ref, ho_ref, no_ref, acc_ref, *, gated):
    e, f = pl.program_id(1), pl.program_id(2)
    ne, nf = pl.num_programs(1), pl.num_programs(2)

    @pl.when((e == 0) & (f == 0))
    def _():
        acc_ref[...] = h_ref[...]

    x = x_ref[...]
    a = jnp.dot(x, w1_ref[...].astype(bf16), preferred_element_type=f32)
    b = jnp.dot(x, w3_ref[...].astype(bf16), preferred_element_type=f32)
    act = a * jax.nn.sigmoid(a) * b
    if gated:
        act = act * cmb_ref[...]
    acc_ref[...] += jnp.dot(act.astype(bf16), w2_ref[...].astype(bf16), preferred_element_type=f32)

    @pl.when((e == ne - 1) & (f == nf - 1))
    def _():
        hn = acc_ref[...]
        ho_ref[...] = hn
        no_ref[...] = _rms(hn, g_ref[...]).astype(no_ref.dtype)


def _ffn(xn, w1, w3, w2, layer_idx, combine_t, h, g_next, n_dtype, tm=768, tf=256):
    ne = w1.shape[1]
    gated = combine_t is not None
    if not gated:
        combine_t = jnp.ones((1, 8, 1), f32)
    row = lambda i, e, f: (i, 0)
    once = pl.Buffered(1)
    cmb_spec = (pl.BlockSpec((None, tm, 1), lambda i, e, f: (e, i, 0)) if gated
                else pl.BlockSpec((None, 8, 1), lambda i, e, f: (0, 0, 0)))
    return pl.pallas_call(
        functools.partial(_ffn_kernel, gated=gated),
        grid=(N_ROWS // tm, ne, D_FF // tf),
        in_specs=[pl.BlockSpec((tm, D_MODEL), row),
                  pl.BlockSpec((None, None, D_MODEL, tf), lambda i, e, f: (layer_idx, e, 0, f)),
                  pl.BlockSpec((None, None, D_MODEL, tf), lambda i, e, f: (layer_idx, e, 0, f)),
                  pl.BlockSpec((None, None, tf, D_MODEL), lambda i, e, f: (layer_idx, e, f, 0)),
                  cmb_spec,
                  pl.BlockSpec((tm, D_MODEL), row, pipeline_mode=once),
                  pl.BlockSpec((1, D_MODEL), lambda i, e, f: (0, 0))],
        out_specs=[pl.BlockSpec((tm, D_MODEL), row, pipeline_mode=once),
                   pl.BlockSpec((tm, D_MODEL), row, pipeline_mode=once)],
        out_shape=[jax.ShapeDtypeStruct((N_ROWS, D_MODEL), f32),
                   jax.ShapeDtypeStruct((N_ROWS, D_MODEL), n_dtype)],
        scratch_shapes=[pltpu.VMEM((tm, D_MODEL), f32)],
        compiler_params=_cparams("parallel", "arbitrary", "arbitrary"),
    )(xn, w1, w3, w2, combine_t, h, g_next)


def _rope_patterns(pos):
    half = ROT_DIM // 2
    inv_freq = jnp.power(ROPE_THETA, -jnp.arange(half, dtype=f32) * 2.0 / ROT_DIM)
    ang = pos.astype(f32)[:, None] * inv_freq[None, :]
    cos, sin = jnp.cos(ang), jnp.sin(ang)
    n = pos.shape[0]
    cos_h = jnp.concatenate([cos, cos, jnp.ones((n, HEAD_DIM - ROT_DIM), f32)], axis=1)
    sin_h = jnp.concatenate([-sin, sin, jnp.zeros((n, HEAD_DIM - ROT_DIM), f32)], axis=1)
    return jnp.concatenate([cos_h, cos_h, sin_h, sin_h], axis=1)


def _block_diag_ones():
    i = np.arange(SEG) // HEAD_DIM
    return jnp.asarray((i[:, None] == i[None, :]).astype(np.float32), dtype=bf16)


def kernel(x_prompt, x_sample, cache_k, cache_v, state_conv, meta_tokens, g_mix, w_in, conv_w, attn_sinks,
           g_attn_out, g_conv_out, w_out, g_ffn, dense_w1, dense_w3, dense_w2, moe_router, moe_w1, moe_w3,
           moe_w2, g_final):
    w_cache = cache_k.shape[2]
    h = jnp.concatenate([x_prompt.reshape(N_PROMPT_ROWS, D_MODEL), x_sample.reshape(DEC_BATCH, D_MODEL),
                         jnp.zeros((META_ROW_IN_BLOCK, D_MODEL), f32), meta_tokens.astype(f32)], axis=0)

    prompt_pos = N_META + jnp.arange(SEQ, dtype=jnp.int32)
    tail_pos = jnp.maximum(jnp.arange(ROW_BLOCK, dtype=jnp.int32) - META_ROW_IN_BLOCK, 0)
    rope_tab = _rope_patterns(jnp.concatenate([prompt_pos, tail_pos]))
    sample_tab = _rope_patterns(jnp.full((8,), PAST_LEN, jnp.int32))
    bd = _block_diag_ones()
    w_out_bf = w_out.astype(bf16)
    cache_k = cache_k.reshape(DEPTH, DEC_BATCH, w_cache, KV_DIM)
    cache_v = cache_v.reshape(DEPTH, DEC_BATCH, w_cache, KV_DIM)

    xn = _norm_rows(h, g_mix[0:1])
    nk_p, nv_p, nc_p, nk_s, nv_s, nc_s = [], [], [], [], [], []
    y = None
    for l in range(DEPTH):
        z = _in_proj(xn, w_in, l)
        g_attn = g_attn_out[l:l + 1]
        an, krot = _attn_prompt(z, attn_sinks[l], rope_tab, g_attn)
        sinks_exp = jnp.repeat(attn_sinks[l].reshape(N_KV_HEADS, GROUP).T, HEAD_DIM, axis=1)
        an, nk, nv = _attn_sample(an, z, cache_k[l], cache_v[l], sample_tab, sinks_exp, g_attn, bd)
        state_l = state_conv[l].reshape(DEC_BATCH, (CONV_WIDTH - 1) * CONV_DIM)
        c, u = _conv_mix(z, state_l, conv_w[l], g_conv_out[l:l + 1], bd)
        h, xn = _out_proj(an, c, w_out_bf, l, h, g_ffn[l:l + 1])

        last = l == DEPTH - 1
        g_next = g_final[None] if last else g_mix[l + 1:l + 2]
        n_dtype = f32 if last else bf16
        j = l // 2
        if l % 2 == 0:
            h, nxt = _ffn(xn, dense_w1[:, None], dense_w3[:, None], dense_w2[:, None], j, None, h, g_next, n_dtype)
        else:
            combine = _router(h, g_ffn[l:l + 1], moe_router[j])
            combine_t = combine.T[:, :, None]
            h, nxt = _ffn(xn, moe_w1, moe_w3, moe_w2, j, combine_t, h, g_next, n_dtype)
        if last:
            y = nxt
        else:
            xn = nxt

        zp = z[:N_PROMPT_ROWS].reshape(BATCH, SEQ, IN_DIM)
        nk_p.append(krot[:N_PROMPT_ROWS].reshape(BATCH, SEQ, N_KV_HEADS, HEAD_DIM)[:, SEQ - WINDOW:])
        nv_p.append(zp[:, SEQ - WINDOW:, ATTN_DIM + KV_DIM:ATTN_DIM + 2 * KV_DIM]
                    .reshape(BATCH, WINDOW, N_KV_HEADS, HEAD_DIM))
        nc_p.append(u[:N_PROMPT_ROWS].reshape(BATCH, SEQ, CONV_DIM)[:, SEQ - (CONV_WIDTH - 1):])
        nk_s.append(nk.reshape(DEC_BATCH, w_cache, N_KV_HEADS, HEAD_DIM))
        nv_s.append(nv.reshape(DEC_BATCH, w_cache, N_KV_HEADS, HEAD_DIM))
        nc_s.append(jnp.stack([state_conv[l][:, 1], u[SAMPLE_ROW0:SAMPLE_ROW0 + DEC_BATCH]], axis=1))

    y_prompt = y[:N_PROMPT_ROWS].reshape(BATCH, SEQ, D_MODEL)
    y_sample = y[SAMPLE_ROW0:SAMPLE_ROW0 + DEC_BATCH].reshape(DEC_BATCH, 1, D_MODEL)
    return (y_prompt, y_sample, jnp.stack(nk_p), jnp.stack(nv_p), jnp.stack(nc_p),
            jnp.stack(nk_s), jnp.stack(nv_s), jnp.stack(nc_s))
```

```python
import functools

import jax
import jax.numpy as jnp
import numpy as np
from jax import lax
from jax.experimental import pallas as pl
from jax.experimental.pallas import tpu as pltpu

D_MODEL = 2048
BATCH = 4
SEQ = 2048
DEPTH = 2
DEC_BATCH = 128
PAST_LEN = 8192
HEAD_DIM = 64
ATTN_DIM = D_MODEL // 2
N_Q_HEADS = ATTN_DIM // HEAD_DIM
N_KV_HEADS = N_Q_HEADS // 4
GROUP = N_Q_HEADS // N_KV_HEADS
KV_DIM = N_KV_HEADS * HEAD_DIM
CONV_DIM = D_MODEL - ATTN_DIM
CONV_WIDTH = 3
IN_DIM = ATTN_DIM + 2 * KV_DIM + 3 * CONV_DIM
WINDOW = 128
ROPE_THETA = 500000.0
ROT_DIM = HEAD_DIM // 4
N_META = 16
D_FF = 7 * D_MODEL // 2
N_EXPERTS = 8
EPS = 1e-5

ROW_BLOCK = 128
N_PROMPT_ROWS = BATCH * SEQ
SAMPLE_ROW0 = N_PROMPT_ROWS
TAIL_ROW0 = SAMPLE_ROW0 + DEC_BATCH
N_ROWS = TAIL_ROW0 + ROW_BLOCK
N_ROW_BLOCKS = N_ROWS // ROW_BLOCK
BLOCKS_PER_SEQ = SEQ // ROW_BLOCK
SAMPLE_BLOCK = SAMPLE_ROW0 // ROW_BLOCK
META_BLOCK = TAIL_ROW0 // ROW_BLOCK
META_ROW_IN_BLOCK = ROW_BLOCK - N_META
SEG = 256

VMEM_LIMIT_V7X = 56 * 1024 * 1024

bf16 = jnp.bfloat16
f32 = jnp.float32


def _cparams(*sem):
    return pltpu.CompilerParams(dimension_semantics=sem, vmem_limit_bytes=VMEM_LIMIT_V7X)


def _rms(x, g):
    ms = jnp.mean(x * x, axis=-1, keepdims=True)
    return (x * lax.rsqrt(ms + EPS)) * g


def _segsum(x, bd):
    x1 = x.astype(bf16)
    r1 = x - x1.astype(f32)
    x2 = r1.astype(bf16)
    x3 = (r1 - x2.astype(f32)).astype(bf16)
    d = functools.partial(jnp.dot, preferred_element_type=f32)
    return d(x1, bd) + d(x2, bd) + d(x3, bd)


def _rope(x, cos_t, sin_t):
    n = x.shape[-1]
    lane = lax.broadcasted_iota(jnp.int32, x.shape, x.ndim - 1) % HEAD_DIM
    nxt = pltpu.roll(x, n - ROT_DIM // 2, axis=x.ndim - 1)
    prv = pltpu.roll(x, ROT_DIM // 2, axis=x.ndim - 1)
    return x * cos_t + jnp.where(lane < ROT_DIM // 2, nxt, prv) * sin_t


def _norm_kernel(x_ref, g_ref, o_ref):
    o_ref[...] = _rms(x_ref[...], g_ref[...]).astype(o_ref.dtype)


def _norm_rows(x, g, tm=768):
    return pl.pallas_call(
        _norm_kernel,
        grid=(N_ROWS // tm,),
        in_specs=[pl.BlockSpec((tm, D_MODEL), lambda i: (i, 0)),
                  pl.BlockSpec((1, D_MODEL), lambda i: (0, 0))],
        out_specs=pl.BlockSpec((tm, D_MODEL), lambda i: (i, 0)),
        out_shape=jax.ShapeDtypeStruct((N_ROWS, D_MODEL), bf16),
        compiler_params=_cparams("parallel"),
    )(x, g)


def _inproj_kernel(x_ref, w_ref, o_ref, wbf_ref):
    @pl.when(pl.program_id(1) == 0)
    def _():
        wbf_ref[...] = w_ref[...].astype(bf16)

    o_ref[...] = jnp.dot(x_ref[...], wbf_ref[...], preferred_element_type=f32)


def _in_proj(xn, w_in, layer, tm=768, tn=768):
    return pl.pallas_call(
        _inproj_kernel,
        grid=(IN_DIM // tn, N_ROWS // tm),
        in_specs=[pl.BlockSpec((tm, D_MODEL), lambda j, i: (i, 0)),
                  pl.BlockSpec((None, D_MODEL, tn), lambda j, i: (layer, 0, j))],
        out_specs=pl.BlockSpec((tm, tn), lambda j, i: (i, j)),
        out_shape=jax.ShapeDtypeStruct((N_ROWS, IN_DIM), f32),
        scratch_shapes=[pltpu.VMEM((D_MODEL, tn), bf16)],
        compiler_params=_cparams("parallel", "arbitrary"),
    )(xn, w_in)


def _attn_prompt_kernel(sink_ref, q_ref, kc_ref, vc_ref, kp_ref, vp_ref, tc_ref, tp_ref, g_ref,
                        an_ref, krot_ref):
    s = pl.program_id(0)
    is_meta = s == N_ROW_BLOCKS - 2
    j = s % BLOCKS_PER_SEQ
    tc = tc_ref[...]
    tp = tp_ref[...]
    q = _rope(q_ref[...], jnp.tile(tc[:, :128], (1, 8)), jnp.tile(tc[:, 128:], (1, 8)))
    kc = _rope(kc_ref[...], jnp.tile(tc[:, :128], (1, 2)), jnp.tile(tc[:, 128:], (1, 2)))
    kp = _rope(kp_ref[...], jnp.tile(tp[:, :128], (1, 2)), jnp.tile(tp[:, 128:], (1, 2)))
    krot_ref[...] = kc
    kall = jnp.concatenate([kp, kc], axis=0).astype(bf16)
    vall = jnp.concatenate([vp_ref[...], vc_ref[...]], axis=0).astype(bf16)

    row = lax.broadcasted_iota(jnp.int32, (GROUP * ROW_BLOCK, 2 * ROW_BLOCK), 0) % ROW_BLOCK
    col = lax.broadcasted_iota(jnp.int32, (GROUP * ROW_BLOCK, 2 * ROW_BLOCK), 1)
    lo_prev = jnp.where(is_meta, ROW_BLOCK, jnp.where(j == 0, META_ROW_IN_BLOCK, 0))
    lo_cur = jnp.where(is_meta, META_ROW_IN_BLOCK, 0)
    ccur = col - ROW_BLOCK
    valid4 = (((col < ROW_BLOCK) & (col >= row) & (col >= lo_prev))
              | ((ccur >= 0) & (ccur <= row) & (ccur >= lo_cur)))
    half = lax.broadcasted_iota(jnp.int32, (ROW_BLOCK, 128), 1) // HEAD_DIM
    half4 = lax.broadcasted_iota(jnp.int32, (GROUP * ROW_BLOCK, 128), 1) // HEAD_DIM
    g_all = g_ref[...]

    for kvh in range(N_KV_HEADS):
        kside = kvh % 2
        kpair = kall[:, 128 * (kvh // 2):128 * (kvh // 2) + 128]
        vpair = vall[:, 128 * (kvh // 2):128 * (kvh // 2) + 128]
        qs, sinks = [], []
        for g in range(GROUP):
            h = GROUP * kvh + g
            x = q[:, 128 * (h // 2):128 * (h // 2) + 128]
            if h % 2 != kside:
                x = pltpu.roll(x, HEAD_DIM, axis=1)
            qs.append(jnp.where(half == kside, x, 0.0))
            sinks.append(jnp.full((ROW_BLOCK, 1), sink_ref[h], f32))
        q4 = jnp.concatenate(qs, axis=0).astype(bf16)
        sink = jnp.concatenate(sinks, axis=0)
        sc = lax.dot_general(q4, kpair, (((1,), (1,)), ((), ())), preferred_element_type=f32)
        sc = jnp.where(valid4, sc * (HEAD_DIM ** -0.5), -jnp.inf)
        m = jnp.maximum(jnp.max(sc, axis=-1, keepdims=True), sink)
        p = jnp.exp(sc - m)
        denom = jnp.sum(p, axis=-1, keepdims=True) + jnp.exp(sink - m)
        o = jnp.dot(p.astype(bf16), vpair, preferred_element_type=f32) / denom
        ms = jnp.sum(jnp.where(half4 == kside, o * o, 0.0), axis=-1, keepdims=True) * (1.0 / HEAD_DIM)
        on = o * lax.rsqrt(ms + EPS)
        for pair in range(2):
            parts = []
            for side in range(2):
                g = 2 * pair + side
                x = on[ROW_BLOCK * g:ROW_BLOCK * (g + 1)]
                if side != kside:
                    x = pltpu.roll(x, HEAD_DIM, axis=1)
                parts.append(x)
            blk = 2 * kvh + pair
            out = jnp.where(half == 0, parts[0], parts[1]) * g_all[:, 128 * blk:128 * blk + 128]
            an_ref[:, 128 * blk:128 * blk + 128] = out.astype(an_ref.dtype)


def _cur_block(s):
    return jnp.where(s == N_ROW_BLOCKS - 2, META_BLOCK, s)


def _prev_block(s):
    return jnp.where((s % BLOCKS_PER_SEQ == 0) | (s == N_ROW_BLOCKS - 2), META_BLOCK, s - 1)


def _cur_tab(s):
    return jnp.where(s == N_ROW_BLOCKS - 2, BLOCKS_PER_SEQ, s % BLOCKS_PER_SEQ)


def _prev_tab(s):
    return jnp.where((s % BLOCKS_PER_SEQ == 0) | (s == N_ROW_BLOCKS - 2), BLOCKS_PER_SEQ, s % BLOCKS_PER_SEQ - 1)


def _attn_prompt(z, sinks, rope_tab, g_attn):
    kcol, vcol = ATTN_DIM // KV_DIM, ATTN_DIM // KV_DIM + 1
    return pl.pallas_call(
        _attn_prompt_kernel,
        grid=(N_ROW_BLOCKS - 1,),
        in_specs=[pl.BlockSpec(memory_space=pltpu.SMEM),
                  pl.BlockSpec((ROW_BLOCK, ATTN_DIM), lambda s: (_cur_block(s), 0)),
                  pl.BlockSpec((ROW_BLOCK, KV_DIM), lambda s: (_cur_block(s), kcol)),
                  pl.BlockSpec((ROW_BLOCK, KV_DIM), lambda s: (_cur_block(s), vcol)),
                  pl.BlockSpec((ROW_BLOCK, KV_DIM), lambda s: (_prev_block(s), kcol)),
                  pl.BlockSpec((ROW_BLOCK, KV_DIM), lambda s: (_prev_block(s), vcol)),
                  pl.BlockSpec((ROW_BLOCK, 256), lambda s: (_cur_tab(s), 0)),
                  pl.BlockSpec((ROW_BLOCK, 256), lambda s: (_prev_tab(s), 0)),
                  pl.BlockSpec((1, ATTN_DIM), lambda s: (0, 0))],
        out_specs=[pl.BlockSpec((ROW_BLOCK, ATTN_DIM), lambda s: (_cur_block(s), 0)),
                   pl.BlockSpec((ROW_BLOCK, KV_DIM), lambda s: (_cur_block(s), 0))],
        out_shape=[jax.ShapeDtypeStruct((N_ROWS, ATTN_DIM), bf16),
                   jax.ShapeDtypeStruct((N_ROWS, KV_DIM), f32)],
        compiler_params=_cparams("arbitrary"),
    )(sinks, z, z, z, z, z, rope_tab, rope_tab, g_attn)


def _attn_sample_kernel(an_in_ref, q_ref, k_ref, v_ref, ck_ref, cv_ref, tab_ref, sink_ref, g_ref, bd_ref,
                        an_ref, nk_ref, nv_ref):
    del an_in_ref
    bt = q_ref.shape[0]
    w = ck_ref.shape[1]
    tab = tab_ref[...]
    q = _rope(q_ref[...], jnp.tile(tab[0:1, :128], (1, 8)), jnp.tile(tab[0:1, 128:], (1, 8)))
    k = _rope(k_ref[...], jnp.tile(tab[0:1, :128], (1, 2)), jnp.tile(tab[0:1, 128:], (1, 2)))
    v = v_ref[...]
    ck = ck_ref[...]
    cv = cv_ref[...]
    bd = bd_ref[...]

    last = lax.broadcasted_iota(jnp.int32, (w, KV_DIM), 0) == w - 1
    for b in range(bt):
        nk_ref[b] = jnp.where(last, k[b:b + 1], pltpu.roll(ck[b], w - 1, axis=0))
        nv_ref[b] = jnp.where(last, v[b:b + 1], pltpu.roll(cv[b], w - 1, axis=0))

    seg = lax.broadcasted_iota(jnp.int32, (bt, SEG), 1) // HEAD_DIM
    scale = HEAD_DIM ** -0.5
    chunks = [jnp.zeros((bt, SEG), f32) for _ in range(N_KV_HEADS)]
    for g in range(GROUP):
        qg = jnp.zeros((bt, SEG), f32)
        for kvh in range(N_KV_HEADS):
            x = q[:, SEG * kvh:SEG * (kvh + 1)]
            sh = (HEAD_DIM * (kvh - g)) % SEG
            if sh:
                x = pltpu.roll(x, sh, axis=1)
            qg = jnp.where(seg == kvh, x, qg)
        prod = (ck * qg[:, None, :]).reshape(bt * w, SEG)
        sc = (_segsum(prod, bd) * scale).reshape(bt, w, SEG)
        sn = _segsum(qg * k, bd) * scale
        sink = sink_ref[g:g + 1, :]
        m = jnp.maximum(jnp.maximum(jnp.max(sc, axis=1), sn), sink)
        p = jnp.exp(sc - m[:, None, :])
        pn = jnp.exp(sn - m)
        denom = jnp.sum(p, axis=1) + pn + jnp.exp(sink - m)
        o = (jnp.sum(p * cv, axis=1) + pn * v) / denom
        ms = _segsum(o * o, bd) * (1.0 / HEAD_DIM)
        on = o * lax.rsqrt(ms + EPS)
        for kvh in range(N_KV_HEADS):
            sh = (HEAD_DIM * (g - kvh)) % SEG
            x = pltpu.roll(on, sh, axis=1) if sh else on
            chunks[kvh] = jnp.where(seg == g, x, chunks[kvh])
    out = jnp.concatenate(chunks, axis=1) * g_ref[...]
    an_ref[...] = out.astype(an_ref.dtype)


def _attn_sample(an, z, cache_k_l, cache_v_l, tab, sinks_exp, g_attn, bd, bt=16):
    w = cache_k_l.shape[1]
    row0 = SAMPLE_ROW0 // bt
    kcol, vcol = ATTN_DIM // KV_DIM, ATTN_DIM // KV_DIM + 1
    cache_spec = pl.BlockSpec((bt, w, KV_DIM), lambda i: (i, 0, 0))
    return pl.pallas_call(
        _attn_sample_kernel,
        grid=(DEC_BATCH // bt,),
        in_specs=[pl.BlockSpec(memory_space=pl.ANY),
                  pl.BlockSpec((bt, ATTN_DIM), lambda i: (row0 + i, 0)),
                  pl.BlockSpec((bt, KV_DIM), lambda i: (row0 + i, kcol)),
                  pl.BlockSpec((bt, KV_DIM), lambda i: (row0 + i, vcol)),
                  cache_spec, cache_spec,
                  pl.BlockSpec((8, 256), lambda i: (0, 0)),
                  pl.BlockSpec((GROUP, SEG), lambda i: (0, 0)),
                  pl.BlockSpec((1, ATTN_DIM), lambda i: (0, 0)),
                  pl.BlockSpec((SEG, SEG), lambda i: (0, 0))],
        out_specs=[pl.BlockSpec((bt, ATTN_DIM), lambda i: (row0 + i, 0)), cache_spec, cache_spec],
        out_shape=[jax.ShapeDtypeStruct((N_ROWS, ATTN_DIM), bf16),
                   jax.ShapeDtypeStruct(cache_k_l.shape, f32),
                   jax.ShapeDtypeStruct(cache_v_l.shape, f32)],
        input_output_aliases={0: 0},
        compiler_params=_cparams("parallel"),
    )(an, z, z, z, cache_k_l, cache_v_l, tab, sinks_exp, g_attn, bd)


def _conv_kernel(z1_ref, z2_ref, p1_ref, p2_ref, s0_ref, s1_ref, w_ref, g_ref, bd_ref, c_ref, u_ref):
    i = pl.program_id(0)

    def split(b1, b2):
        gb = b1[:, :CONV_DIM]
        gc = jnp.concatenate([b1[:, CONV_DIM:], b2[:, :CONV_DIM // 2]], axis=1)
        return gb, gc * b2[:, CONV_DIM // 2:]

    gb, u = split(z1_ref[...], z2_ref[...])
    _, up = split(p1_ref[...], p2_ref[...])
    u_ref[...] = u
    row = lax.broadcasted_iota(jnp.int32, u.shape, 0)
    u1 = jnp.where(row == 0, up[7:8], pltpu.roll(u, 1, axis=0))
    u2 = jnp.where(row == 0, up[6:7], jnp.where(row == 1, up[7:8], pltpu.roll(u, 2, axis=0)))
    is_sample = i == SAMPLE_BLOCK
    u1 = jnp.where(is_sample, s1_ref[...], u1)
    u2 = jnp.where(is_sample, s0_ref[...], u2)
    w = w_ref[...]
    y = u2 * w[0:1] + u1 * w[1:2] + u * w[2:3]
    t = gb * y
    bd = bd_ref[...]
    g = g_ref[...]
    for c in range(CONV_DIM // SEG):
        tc = t[:, SEG * c:SEG * (c + 1)]
        ms = _segsum(tc * tc, bd) * (1.0 / HEAD_DIM)
        c_ref[:, SEG * c:SEG * (c + 1)] = (tc * lax.rsqrt(ms + EPS) * g[:, SEG * c:SEG * (c + 1)]).astype(c_ref.dtype)


def _conv_prev_block(i):
    last8_of_meta = N_ROWS // 8 - 1
    return jnp.where(i % BLOCKS_PER_SEQ == 0, last8_of_meta, (ROW_BLOCK // 8) * i - 1)


def _conv_mix(z, state_l, conv_w_l, g_conv, bd):
    wide = 3 * CONV_DIM // 2
    return pl.pallas_call(
        _conv_kernel,
        grid=(N_ROW_BLOCKS,),
        in_specs=[pl.BlockSpec((ROW_BLOCK, wide), lambda i: (i, 1)),
                  pl.BlockSpec((ROW_BLOCK, wide), lambda i: (i, 2)),
                  pl.BlockSpec((8, wide), lambda i: (_conv_prev_block(i), 1)),
                  pl.BlockSpec((8, wide), lambda i: (_conv_prev_block(i), 2)),
                  pl.BlockSpec((DEC_BATCH, CONV_DIM), lambda i: (0, 0)),
                  pl.BlockSpec((DEC_BATCH, CONV_DIM), lambda i: (0, 1)),
                  pl.BlockSpec((CONV_WIDTH, CONV_DIM), lambda i: (0, 0)),
                  pl.BlockSpec((1, CONV_DIM), lambda i: (0, 0)),
                  pl.BlockSpec((SEG, SEG), lambda i: (0, 0))],
        out_specs=[pl.BlockSpec((ROW_BLOCK, CONV_DIM), lambda i: (i, 0)),
                   pl.BlockSpec((ROW_BLOCK, CONV_DIM), lambda i: (i, 0))],
        out_shape=[jax.ShapeDtypeStruct((N_ROWS, CONV_DIM), bf16),
                   jax.ShapeDtypeStruct((N_ROWS, CONV_DIM), f32)],
        compiler_params=_cparams("parallel"),
    )(z, z, z, z, state_l, state_l, conv_w_l, g_conv, bd)


def _outproj_kernel(a_ref, c_ref, wa_ref, wc_ref, h_ref, g_ref, ho_ref, no_ref):
    acc = jnp.dot(a_ref[...], wa_ref[...], preferred_element_type=f32)
    acc += jnp.dot(c_ref[...], wc_ref[...], preferred_element_type=f32)
    hn = h_ref[...] + acc
    ho_ref[...] = hn
    no_ref[...] = _rms(hn, g_ref[...]).astype(no_ref.dtype)


def _out_proj(an, c, w_out_bf, layer, h, g_next, tm=384):
    row = lambda i: (i, 0)
    return pl.pallas_call(
        _outproj_kernel,
        grid=(N_ROWS // tm,),
        in_specs=[pl.BlockSpec((tm, ATTN_DIM), row),
                  pl.BlockSpec((tm, CONV_DIM), row),
                  pl.BlockSpec((None, ATTN_DIM, D_MODEL), lambda i: (layer, 0, 0)),
                  pl.BlockSpec((None, CONV_DIM, D_MODEL), lambda i: (layer, 1, 0)),
                  pl.BlockSpec((tm, D_MODEL), row),
                  pl.BlockSpec((1, D_MODEL), lambda i: (0, 0))],
        out_specs=[pl.BlockSpec((tm, D_MODEL), row), pl.BlockSpec((tm, D_MODEL), row)],
        out_shape=[jax.ShapeDtypeStruct((N_ROWS, D_MODEL), f32),
                   jax.ShapeDtypeStruct((N_ROWS, D_MODEL), bf16)],
        compiler_params=_cparams("parallel"),
    )(an, c, w_out_bf, w_out_bf, h, g_next)


def _router_kernel(h_ref, g_ref, r_ref, tri_ref, cmb_ref, dest_ref, carry_ref):
    i = pl.program_id(0)
    tm = h_ref.shape[0]

    @pl.when(i == 0)
    def _():
        carry_ref[...] = jnp.zeros_like(carry_ref)

    n = _rms(h_ref[...], g_ref[...])
    logits = jnp.dot(n, r_ref[...], preferred_element_type=f32, precision=lax.Precision.HIGHEST)
    idx = lax.broadcasted_iota(jnp.int32, logits.shape, 1)
    m1 = jnp.max(logits, axis=-1, keepdims=True)
    i1 = jnp.min(jnp.where(logits == m1, idx, N_EXPERTS), axis=-1, keepdims=True)
    rest = jnp.where(idx == i1, -jnp.inf, logits)
    m2 = jnp.max(rest, axis=-1, keepdims=True)
    i2 = jnp.min(jnp.where(rest == m2, idx, N_EXPERTS), axis=-1, keepdims=True)
    e2 = jnp.exp(m2 - m1)
    den = 1.0 + e2
    rowid = i * tm + lax.broadcasted_iota(jnp.int32, logits.shape, 0)
    real = (rowid < TAIL_ROW0) | (rowid >= N_ROWS - N_META)
    sel = ((idx == i1) | (idx == i2)) & real
    cmb_ref[...] = jnp.where(sel, jnp.where(idx == i1, 1.0 / den, e2 / den), 0.0)
    self = jnp.where(sel, 1.0, 0.0)
    incl = jnp.dot(tri_ref[...], self.astype(bf16), preferred_element_type=f32)
    carry = carry_ref[...]
    dest_ref[...] = jnp.where(sel, incl - self + carry, -1.0).astype(jnp.int32)
    carry_ref[...] = carry + incl[tm - 1:tm, :]


def _router(h, g, router, tm=768):
    tri = jnp.asarray(np.tril(np.ones((tm, tm), np.float32)), dtype=bf16)
    return pl.pallas_call(
        _router_kernel,
        grid=(N_ROWS // tm,),
        in_specs=[pl.BlockSpec((tm, D_MODEL), lambda i: (i, 0)),
                  pl.BlockSpec((1, D_MODEL), lambda i: (0, 0)),
                  pl.BlockSpec((D_MODEL, N_EXPERTS), lambda i: (0, 0)),
                  pl.BlockSpec((tm, tm), lambda i: (0, 0))],
        out_specs=[pl.BlockSpec((tm, N_EXPERTS), lambda i: (i, 0)),
                   pl.BlockSpec((tm, N_EXPERTS), lambda i: (i, 0))],
        out_shape=[jax.ShapeDtypeStruct((N_ROWS, N_EXPERTS), f32),
                   jax.ShapeDtypeStruct((N_ROWS, N_EXPERTS), jnp.int32)],
        scratch_shapes=[pltpu.VMEM((1, N_EXPERTS), f32)],
        compiler_params=_cparams("arbitrary"),
    )(h, g, router, tri)


MOE_CHUNK = 256
MOE_SUB = 768
MOE_VISIT_ROWS = 3 * MOE_SUB
MOE_CAP = 4 * MOE_VISIT_ROWS
MOE_TF = 256
N_CHUNKS = N_ROWS // MOE_CHUNK
N_FSTEPS = D_FF // MOE_TF
MOE_MAX_VISITS = (2 * N_ROWS) // MOE_VISIT_ROWS + N_EXPERTS


def _moe_kernel(vis_e, vis_k, vis_nsub, vis_ok, rlo, xn_ref, dest_ref, w1_ref, w3_ref, w2_ref, y_ref, acc_ref):
    v, p = pl.program_id(0), pl.program_id(1)
    ok = vis_ok[v] == 1
    e = vis_e[v]
    base = vis_k[v] * MOE_VISIT_ROWS

    @pl.when(ok & (p == 0))
    def _():
        y_ref[...] = jnp.zeros_like(y_ref)

    @pl.when(ok & (p < N_CHUNKS))
    def _():
        lo = rlo[e * (N_CHUNKS + 1) + p]
        hi = rlo[e * (N_CHUNKS + 1) + p + 1]

        @pl.when((hi > lo) & (hi > base) & (lo < base + MOE_VISIT_ROWS))
        def _():
            tile = jnp.minimum(jnp.maximum(lo - base, 0) // MOE_CHUNK, MOE_VISIT_ROWS // MOE_CHUNK - 2)
            off = pl.multiple_of(tile * MOE_CHUNK, MOE_CHUNK)
            rel = dest_ref[...] - (base + off)
            win = lax.broadcasted_iota(jnp.int32, (2 * MOE_CHUNK, MOE_CHUNK), 0)
            onehot = jnp.where(win == rel, 1.0, 0.0).astype(bf16)
            rows = jnp.dot(onehot, xn_ref[...], preferred_element_type=f32)
            y_ref[pl.ds(off, 2 * MOE_CHUNK), :] += rows.astype(bf16)

    @pl.when(ok & (p >= N_CHUNKS))
    def _():
        f = p - N_CHUNKS
        w1 = w1_ref[...].astype(bf16)
        w3 = w3_ref[...].astype(bf16)
        w2 = w2_ref[...].astype(bf16)
        for r in range(MOE_VISIT_ROWS // MOE_SUB):
            rows = slice(r * MOE_SUB, (r + 1) * MOE_SUB)

            @pl.when(r < vis_nsub[v])
            def _():
                x = y_ref[rows, :]
                a = jnp.dot(x, w1, preferred_element_type=f32)
                b = jnp.dot(x, w3, preferred_element_type=f32)
                act = (a * jax.nn.sigmoid(a) * b).astype(bf16)
                contrib = jnp.dot(act, w2, preferred_element_type=f32)

                @pl.when(f == 0)
                def _():
                    acc_ref[rows, :] = contrib

                @pl.when((f > 0) & (f < N_FSTEPS - 1))
                def _():
                    acc_ref[rows, :] += contrib

                @pl.when(f == N_FSTEPS - 1)
                def _():
                    y_ref[rows, :] = (acc_ref[rows, :] + contrib).astype(y_ref.dtype)


def _moe_experts(xn, dest_exp, tables, w1, w3, w2, layer_idx):
    def chunk(v, p, ve, vk, vn, vok, rlo):
        return jnp.where(vok[v] == 1, jnp.minimum(p, N_CHUNKS - 1), N_CHUNKS - 1)

    def fstep(v, p, ve, vk, vn, vok, rlo):
        return jnp.where(vok[v] == 1, jnp.maximum(p - N_CHUNKS, 0), N_FSTEPS - 1)

    grid_spec = pltpu.PrefetchScalarGridSpec(
        num_scalar_prefetch=5,
        grid=(MOE_MAX_VISITS, N_CHUNKS + N_FSTEPS),
        in_specs=[pl.BlockSpec((MOE_CHUNK, D_MODEL), lambda v, p, *t: (chunk(v, p, *t), 0)),
                  pl.BlockSpec((None, None, 1, MOE_CHUNK), lambda v, p, *t: (t[0][v], chunk(v, p, *t), 0, 0)),
                  pl.BlockSpec((None, None, D_MODEL, MOE_TF), lambda v, p, *t: (layer_idx, t[0][v], 0, fstep(v, p, *t))),
                  pl.BlockSpec((None, None, D_MODEL, MOE_TF), lambda v, p, *t: (layer_idx, t[0][v], 0, fstep(v, p, *t))),
                  pl.BlockSpec((None, None, MOE_TF, D_MODEL), lambda v, p, *t: (layer_idx, t[0][v], fstep(v, p, *t), 0))],
        out_specs=pl.BlockSpec((None, MOE_VISIT_ROWS, D_MODEL), lambda v, p, *t: (t[0][v], t[1][v], 0),
                               pipeline_mode=pl.Buffered(1)),
        scratch_shapes=[pltpu.VMEM((MOE_VISIT_ROWS, D_MODEL), f32)],
    )
    return pl.pallas_call(
        _moe_kernel,
        grid_spec=grid_spec,
        out_shape=jax.ShapeDtypeStruct((N_EXPERTS, MOE_CAP, D_MODEL), bf16),
        compiler_params=_cparams("arbitrary", "arbitrary"),
    )(*tables, xn, dest_exp, w1, w3, w2)


def _combine_kernel(t0, nec, h_ref, dest_ref, cmb_ref, g_ref, *rest):
    ywin = rest[:2 * N_EXPERTS]
    yp_ref, ys_ref, acc_ref = rest[2 * N_EXPERTS:]
    c = pl.program_id(0)
    acc_ref[...] = h_ref[...]
    dest = dest_ref[...]
    cmb = cmb_ref[...]
    lane = lax.broadcasted_iota(jnp.int32, (MOE_CHUNK, MOE_CHUNK), 1)
    for e in range(N_EXPERTS):
        @pl.when(nec[e * N_CHUNKS + c] > 0)
        def _():
            rel = dest[:, e:e + 1] - MOE_CHUNK * t0[e * N_CHUNKS + c]
            got = jnp.zeros((MOE_CHUNK, D_MODEL), f32)
            for w in range(2):
                onehot = jnp.where(rel - MOE_CHUNK * w == lane, 1.0, 0.0).astype(bf16)
                got += jnp.dot(onehot, ywin[2 * e + w][...], preferred_element_type=f32)
            acc_ref[...] += cmb[:, e:e + 1] * got

    n = _rms(acc_ref[...], g_ref[...])
    n_prompt_chunks = N_PROMPT_ROWS // MOE_CHUNK

    @pl.when(c < n_prompt_chunks)
    def _():
        yp_ref[...] = n

    @pl.when(c == n_prompt_chunks)
    def _():
        ys_ref[...] = n[:DEC_BATCH]


def _moe_combine(y_sorted, win_tables, h, dest_tok, cmb, g_final):
    t0, t0c, t1c, nec = win_tables
    row = lambda c, *t: (c, 0)
    n_prompt_chunks = N_PROMPT_ROWS // MOE_CHUNK
    wins = []
    for e in range(N_EXPERTS):
        wins.append(pl.BlockSpec((None, MOE_CHUNK, D_MODEL), lambda c, a, b, *t, e=e: (e, a[e * N_CHUNKS + c], 0)))
        wins.append(pl.BlockSpec((None, MOE_CHUNK, D_MODEL), lambda c, a, b, *t, e=e: (e, b[e * N_CHUNKS + c], 0)))
    grid_spec = pltpu.PrefetchScalarGridSpec(
        num_scalar_prefetch=4,
        grid=(N_CHUNKS,),
        in_specs=[pl.BlockSpec((MOE_CHUNK, D_MODEL), row),
                  pl.BlockSpec((MOE_CHUNK, N_EXPERTS), row),
                  pl.BlockSpec((MOE_CHUNK, N_EXPERTS), row),
                  pl.BlockSpec((1, D_MODEL), lambda c, *t: (0, 0))] + wins,
        out_specs=[pl.BlockSpec((MOE_CHUNK, D_MODEL), lambda c, *t: (jnp.minimum(c, n_prompt_chunks - 1), 0)),
                   pl.BlockSpec((DEC_BATCH, D_MODEL), lambda c, *t: (0, 0))],
        scratch_shapes=[pltpu.VMEM((MOE_CHUNK, D_MODEL), f32)],
    )

    def body(t0c_ref, t1c_ref, t0_ref, nec_ref, *refs):
        _combine_kernel(t0_ref, nec_ref, *refs)

    return pl.pallas_call(
        body,
        grid_spec=grid_spec,
        out_shape=[jax.ShapeDtypeStruct((N_PROMPT_ROWS, D_MODEL), f32),
                   jax.ShapeDtypeStruct((DEC_BATCH, D_MODEL), f32)],
        compiler_params=_cparams("arbitrary"),
    )(t0c, t1c, t0, nec, h, dest_tok, cmb, g_final, *([y_sorted] * (2 * N_EXPERTS)))


def _routing_tables(dest_tok):
    i32 = jnp.int32
    cnt = (dest_tok >= 0).astype(i32).reshape(N_CHUNKS, MOE_CHUNK, N_EXPERTS).sum(axis=1)
    csum = jnp.cumsum(cnt, axis=0)
    rank_lo = jnp.concatenate([jnp.zeros((1, N_EXPERTS), i32), csum], axis=0).T
    n_e = csum[-1]
    nv = (n_e + MOE_VISIT_ROWS - 1) // MOE_VISIT_ROWS
    cum_nv = jnp.cumsum(nv)
    total = cum_nv[-1]
    v = jnp.arange(MOE_MAX_VISITS, dtype=i32)
    vv = jnp.minimum(v, jnp.maximum(total - 1, 0))
    vis_e = jnp.minimum(jnp.sum((vv[:, None] >= cum_nv[None, :]).astype(i32), axis=1), N_EXPERTS - 1)
    vis_k = vv - (cum_nv - nv)[vis_e]
    vis_ok = (v < total).astype(i32)
    rows = jnp.clip(n_e[vis_e] - vis_k * MOE_VISIT_ROWS, 0, MOE_VISIT_ROWS)
    vis_nsub = jnp.where(vis_ok == 1, (rows + MOE_SUB - 1) // MOE_SUB, 0)
    t0 = rank_lo[:, :N_CHUNKS] // MOE_CHUNK
    tmax = jnp.maximum(nv * (MOE_VISIT_ROWS // MOE_CHUNK) - 1, 0)[:, None]
    moe_tables = (vis_e, vis_k, vis_nsub.astype(i32), vis_ok, rank_lo.reshape(-1))
    win_tables = (t0.reshape(-1), jnp.minimum(t0, tmax).reshape(-1), jnp.minimum(t0 + 1, tmax).reshape(-1),
                  cnt.T.reshape(-1))
    return moe_tables, win_tables


def _ffn_kernel(x_ref, w1_ref, w3_ref, w2_ref, h_ref, g_ref, ho_ref, no_ref, acc_ref):
    f = pl.program_id(1)

    @pl.when(f == 0)
    def _():
        acc_ref[...] = h_ref[...]

    x = x_ref[...]
    a = jnp.dot(x, w1_ref[...].astype(bf16), preferred_element_type=f32)
    b = jnp.dot(x, w3_ref[...].astype(bf16), preferred_element_type=f32)
    act = (a * jax.nn.sigmoid(a) * b).astype(bf16)
    acc_ref[...] += jnp.dot(act, w2_ref[...].astype(bf16), preferred_element_type=f32)

    @pl.when(f == pl.num_programs(1) - 1)
    def _():
        hn = acc_ref[...]
        ho_ref[...] = hn
        no_ref[...] = _rms(hn, g_ref[...]).astype(no_ref.dtype)


def _ffn(xn, w1, w3, w2, layer_idx, h, g_next, tm=768, tf=256):
    row = lambda i, f: (i, 0)
    once = pl.Buffered(1)
    return pl.pallas_call(
        _ffn_kernel,
        grid=(N_ROWS // tm, D_FF // tf),
        in_specs=[pl.BlockSpec((tm, D_MODEL), row),
                  pl.BlockSpec((None, D_MODEL, tf), lambda i, f: (layer_idx, 0, f)),
                  pl.BlockSpec((None, D_MODEL, tf), lambda i, f: (layer_idx, 0, f)),
                  pl.BlockSpec((None, tf, D_MODEL), lambda i, f: (layer_idx, f, 0)),
                  pl.BlockSpec((tm, D_MODEL), row, pipeline_mode=once),
                  pl.BlockSpec((1, D_MODEL), lambda i, f: (0, 0))],
        out_specs=[pl.BlockSpec((tm, D_MODEL), row, pipeline_mode=once),
                   pl.BlockSpec((tm, D_MODEL), row, pipeline_mode=once)],
        out_shape=[jax.ShapeDtypeStruct((N_ROWS, D_MODEL), f32),
                   jax.ShapeDtypeStruct((N_ROWS, D_MODEL), bf16)],
        scratch_shapes=[pltpu.VMEM((tm, D_MODEL), f32)],
        compiler_params=_cparams("parallel", "arbitrary"),
    )(xn, w1, w3, w2, h, g_next)


def _rope_patterns(pos):
    half = ROT_DIM // 2
    inv_freq = jnp.power(ROPE_THETA, -jnp.arange(half, dtype=f32) * 2.0 / ROT_DIM)
    ang = pos.astype(f32)[:, None] * inv_freq[None, :]
    cos, sin = jnp.cos(ang), jnp.sin(ang)
    n = pos.shape[0]
    cos_h = jnp.concatenate([cos, cos, jnp.ones((n, HEAD_DIM - ROT_DIM), f32)], axis=1)
    sin_h = jnp.concatenate([-sin, sin, jnp.zeros((n, HEAD_DIM - ROT_DIM), f32)], axis=1)
    return jnp.concatenate([cos_h, cos_h, sin_h, sin_h], axis=1)


def _block_diag_ones():
    i = np.arange(SEG) // HEAD_DIM
    return jnp.asarray((i[:, None] == i[None, :]).astype(np.float32), dtype=bf16)


def kernel(x_prompt, x_sample, cache_k, cache_v, state_conv, meta_tokens, g_mix, w_in, conv_w, attn_sinks,
           g_attn_out, g_conv_out, w_out, g_ffn, dense_w1, dense_w3, dense_w2, moe_router, moe_w1, moe_w3,
           moe_w2, g_final):
    w_cache = cache_k.shape[2]
    h = jnp.concatenate([x_prompt.reshape(N_PROMPT_ROWS, D_MODEL), x_sample.reshape(DEC_BATCH, D_MODEL),
                         jnp.zeros((META_ROW_IN_BLOCK, D_MODEL), f32), meta_tokens.astype(f32)], axis=0)

    prompt_pos = N_META + jnp.arange(SEQ, dtype=jnp.int32)
    tail_pos = jnp.maximum(jnp.arange(ROW_BLOCK, dtype=jnp.int32) - META_ROW_IN_BLOCK, 0)
    rope_tab = _rope_patterns(jnp.concatenate([prompt_pos, tail_pos]))
    sample_tab = _rope_patterns(jnp.full((8,), PAST_LEN, jnp.int32))
    bd = _block_diag_ones()
    w_out_bf = w_out.astype(bf16)
    cache_k = cache_k.reshape(DEPTH, DEC_BATCH, w_cache, KV_DIM)
    cache_v = cache_v.reshape(DEPTH, DEC_BATCH, w_cache, KV_DIM)

    xn = _norm_rows(h, g_mix[0:1])
    nk_p, nv_p, nc_p, nk_s, nv_s, nc_s = [], [], [], [], [], []
    for l in range(DEPTH):
        z = _in_proj(xn, w_in, l)
        g_attn = g_attn_out[l:l + 1]
        an, krot = _attn_prompt(z, attn_sinks[l], rope_tab, g_attn)
        sinks_exp = jnp.repeat(attn_sinks[l].reshape(N_KV_HEADS, GROUP).T, HEAD_DIM, axis=1)
        an, nk, nv = _attn_sample(an, z, cache_k[l], cache_v[l], sample_tab, sinks_exp, g_attn, bd)
        state_l = state_conv[l].reshape(DEC_BATCH, (CONV_WIDTH - 1) * CONV_DIM)
        c, u = _conv_mix(z, state_l, conv_w[l], g_conv_out[l:l + 1], bd)
        h, xn = _out_proj(an, c, w_out_bf, l, h, g_ffn[l:l + 1])

        j = l // 2
        if l % 2 == 0:
            h, xn = _ffn(xn, dense_w1, dense_w3, dense_w2, j, h, g_mix[l + 1:l + 2])
        else:
            cmb, dest_tok = _router(h, g_ffn[l:l + 1], moe_router[j])
            moe_tables, win_tables = _routing_tables(dest_tok)
            dest_exp = dest_tok.T.reshape(N_EXPERTS, N_CHUNKS, 1, MOE_CHUNK)
            y_sorted = _moe_experts(xn, dest_exp, moe_tables, moe_w1, moe_w3, moe_w2, j)
            y_prompt, y_sample = _moe_combine(y_sorted, win_tables, h, dest_tok, cmb, g_final[None])

        zp = z[:N_PROMPT_ROWS].reshape(BATCH, SEQ, IN_DIM)
        nk_p.append(krot[:N_PROMPT_ROWS].reshape(BATCH, SEQ, N_KV_HEADS, HEAD_DIM)[:, SEQ - WINDOW:])
        nv_p.append(zp[:, SEQ - WINDOW:, ATTN_DIM + KV_DIM:ATTN_DIM + 2 * KV_DIM]
                    .reshape(BATCH, WINDOW, N_KV_HEADS, HEAD_DIM))
        nc_p.append(u[:N_PROMPT_ROWS].reshape(BATCH, SEQ, CONV_DIM)[:, SEQ - (CONV_WIDTH - 1):])
        nk_s.append(nk.reshape(DEC_BATCH, w_cache, N_KV_HEADS, HEAD_DIM))
        nv_s.append(nv.reshape(DEC_BATCH, w_cache, N_KV_HEADS, HEAD_DIM))
        nc_s.append(jnp.stack([state_conv[l][:, 1], u[SAMPLE_ROW0:SAMPLE_ROW0 + DEC_BATCH]], axis=1))

    return (y_prompt.reshape(BATCH, SEQ, D_MODEL), y_sample.reshape(DEC_BATCH, 1, D_MODEL), jnp.stack(nk_p), jnp.stack(nv_p), jnp.stack(nc_p),
            jnp.stack(nk_s), jnp.stack(nv_s), jnp.stack(nc_s))
```

```python
import functools

import jax
import jax.numpy as jnp
import numpy as np
from jax import lax
from jax.experimental import pallas as pl
from jax.experimental.pallas import tpu as pltpu

D_MODEL = 2048
BATCH = 4
SEQ = 2048
DEPTH = 2
DEC_BATCH = 128
PAST_LEN = 8192
HEAD_DIM = 64
ATTN_DIM = D_MODEL // 2
N_Q_HEADS = ATTN_DIM // HEAD_DIM
N_KV_HEADS = N_Q_HEADS // 4
GROUP = N_Q_HEADS // N_KV_HEADS
KV_DIM = N_KV_HEADS * HEAD_DIM
CONV_DIM = D_MODEL - ATTN_DIM
CONV_WIDTH = 3
IN_DIM = ATTN_DIM + 2 * KV_DIM + 3 * CONV_DIM
WINDOW = 128
ROPE_THETA = 500000.0
ROT_DIM = HEAD_DIM // 4
N_META = 16
D_FF = 7 * D_MODEL // 2
N_EXPERTS = 8
EPS = 1e-5

ROW_BLOCK = 128
N_PROMPT_ROWS = BATCH * SEQ
SAMPLE_ROW0 = N_PROMPT_ROWS
TAIL_ROW0 = SAMPLE_ROW0 + DEC_BATCH
N_ROWS = TAIL_ROW0 + ROW_BLOCK
N_ROW_BLOCKS = N_ROWS // ROW_BLOCK
BLOCKS_PER_SEQ = SEQ // ROW_BLOCK
SAMPLE_BLOCK = SAMPLE_ROW0 // ROW_BLOCK
META_BLOCK = TAIL_ROW0 // ROW_BLOCK
META_ROW_IN_BLOCK = ROW_BLOCK - N_META
SEG = 256

VMEM_LIMIT_V7X = 56 * 1024 * 1024

bf16 = jnp.bfloat16
f32 = jnp.float32


def _cparams(*sem):
    return pltpu.CompilerParams(dimension_semantics=sem, vmem_limit_bytes=VMEM_LIMIT_V7X)


def _rms(x, g):
    ms = jnp.mean(x * x, axis=-1, keepdims=True)
    return (x * lax.rsqrt(ms + EPS)) * g


def _segsum(x, bd):
    x1 = x.astype(bf16)
    r1 = x - x1.astype(f32)
    x2 = r1.astype(bf16)
    x3 = (r1 - x2.astype(f32)).astype(bf16)
    d = functools.partial(jnp.dot, preferred_element_type=f32)
    return d(x1, bd) + d(x2, bd) + d(x3, bd)


def _rope(x, cos_t, sin_t):
    n = x.shape[-1]
    lane = lax.broadcasted_iota(jnp.int32, x.shape, x.ndim - 1) % HEAD_DIM
    nxt = pltpu.roll(x, n - ROT_DIM // 2, axis=x.ndim - 1)
    prv = pltpu.roll(x, ROT_DIM // 2, axis=x.ndim - 1)
    return x * cos_t + jnp.where(lane < ROT_DIM // 2, nxt, prv) * sin_t


def _norm_kernel(xp_ref, tail_ref, g_ref, h_ref, o_ref):
    is_tail = pl.program_id(0) == pl.num_programs(0) - 1
    x = jnp.where(is_tail, tail_ref[...], xp_ref[...])
    h_ref[...] = x
    o_ref[...] = _rms(x, g_ref[...]).astype(o_ref.dtype)


def _norm_rows(x_prompt_rows, tail_rows, g):
    tm = tail_rows.shape[0]
    n_prompt_tiles = N_PROMPT_ROWS // tm
    row = lambda i: (i, 0)
    return pl.pallas_call(
        _norm_kernel,
        grid=(N_ROWS // tm,),
        in_specs=[pl.BlockSpec((tm, D_MODEL), lambda i: (jnp.minimum(i, n_prompt_tiles - 1), 0)),
                  pl.BlockSpec((tm, D_MODEL), lambda i: (0, 0)),
                  pl.BlockSpec((1, D_MODEL), lambda i: (0, 0))],
        out_specs=[pl.BlockSpec((tm, D_MODEL), row), pl.BlockSpec((tm, D_MODEL), row)],
        out_shape=[jax.ShapeDtypeStruct((N_ROWS, D_MODEL), f32),
                   jax.ShapeDtypeStruct((N_ROWS, D_MODEL), bf16)],
        compiler_params=_cparams("parallel"),
    )(x_prompt_rows, tail_rows, g)


def _inproj_kernel(x_ref, w_ref, o_ref, wbf_ref):
    @pl.when(pl.program_id(1) == 0)
    def _():
        wbf_ref[...] = w_ref[...].astype(bf16)

    o_ref[...] = jnp.dot(x_ref[...], wbf_ref[...], preferred_element_type=f32)


def _in_proj(xn, w_in, layer, tm=768, tn=768):
    return pl.pallas_call(
        _inproj_kernel,
        grid=(IN_DIM // tn, N_ROWS // tm),
        in_specs=[pl.BlockSpec((tm, D_MODEL), lambda j, i: (i, 0)),
                  pl.BlockSpec((None, D_MODEL, tn), lambda j, i: (layer, 0, j))],
        out_specs=pl.BlockSpec((tm, tn), lambda j, i: (i, j)),
        out_shape=jax.ShapeDtypeStruct((N_ROWS, IN_DIM), f32),
        scratch_shapes=[pltpu.VMEM((D_MODEL, tn), bf16)],
        compiler_params=_cparams("parallel", "arbitrary"),
    )(xn, w_in)


def _attn_prompt_kernel(sink_ref, q_ref, kc_ref, vc_ref, kp_ref, vp_ref, tc_ref, tp_ref, g_ref,
                        an_ref, nk_ref, nv_ref):
    s = pl.program_id(0)

    @pl.when(s == N_ROW_BLOCKS - 1)
    def _():
        an_ref[...] = jnp.zeros_like(an_ref)

    @pl.when(s < N_ROW_BLOCKS - 1)
    def _():
        _attn_prompt_block(s, sink_ref, q_ref, kc_ref, vc_ref, kp_ref, vp_ref, tc_ref, tp_ref, g_ref,
                           an_ref, nk_ref, nv_ref)


def _attn_prompt_block(s, sink_ref, q_ref, kc_ref, vc_ref, kp_ref, vp_ref, tc_ref, tp_ref, g_ref,
                       an_ref, nk_ref, nv_ref):
    is_meta = s == N_ROW_BLOCKS - 2
    j = s % BLOCKS_PER_SEQ
    tc = tc_ref[...]
    tp = tp_ref[...]
    q = _rope(q_ref[...], jnp.tile(tc[:, :128], (1, 8)), jnp.tile(tc[:, 128:], (1, 8)))
    kc = _rope(kc_ref[...], jnp.tile(tc[:, :128], (1, 2)), jnp.tile(tc[:, 128:], (1, 2)))
    kp = _rope(kp_ref[...], jnp.tile(tp[:, :128], (1, 2)), jnp.tile(tp[:, 128:], (1, 2)))

    @pl.when((j == BLOCKS_PER_SEQ - 1) & jnp.logical_not(is_meta))
    def _():
        nk_ref[...] = kc
        nv_ref[...] = vc_ref[...]

    kall = jnp.concatenate([kp, kc], axis=0).astype(bf16)
    vall = jnp.concatenate([vp_ref[...], vc_ref[...]], axis=0).astype(bf16)

    row = lax.broadcasted_iota(jnp.int32, (GROUP * ROW_BLOCK, 2 * ROW_BLOCK), 0) % ROW_BLOCK
    col = lax.broadcasted_iota(jnp.int32, (GROUP * ROW_BLOCK, 2 * ROW_BLOCK), 1)
    lo_prev = jnp.where(is_meta, ROW_BLOCK, jnp.where(j == 0, META_ROW_IN_BLOCK, 0))
    lo_cur = jnp.where(is_meta, META_ROW_IN_BLOCK, 0)
    ccur = col - ROW_BLOCK
    valid4 = (((col < ROW_BLOCK) & (col >= row) & (col >= lo_prev))
              | ((ccur >= 0) & (ccur <= row) & (ccur >= lo_cur)))
    half = lax.broadcasted_iota(jnp.int32, (ROW_BLOCK, 128), 1) // HEAD_DIM
    half4 = lax.broadcasted_iota(jnp.int32, (GROUP * ROW_BLOCK, 128), 1) // HEAD_DIM
    g_all = g_ref[...]

    for kvh in range(N_KV_HEADS):
        kside = kvh % 2
        kpair = kall[:, 128 * (kvh // 2):128 * (kvh // 2) + 128]
        vpair = vall[:, 128 * (kvh // 2):128 * (kvh // 2) + 128]
        qs, sinks = [], []
        for g in range(GROUP):
            h = GROUP * kvh + g
            x = q[:, 128 * (h // 2):128 * (h // 2) + 128]
            if h % 2 != kside:
                x = pltpu.roll(x, HEAD_DIM, axis=1)
            qs.append(jnp.where(half == kside, x, 0.0))
            sinks.append(jnp.full((ROW_BLOCK, 1), sink_ref[h], f32))
        q4 = jnp.concatenate(qs, axis=0).astype(bf16)
        sink = jnp.concatenate(sinks, axis=0)
        sc = lax.dot_general(q4, kpair, (((1,), (1,)), ((), ())), preferred_element_type=f32)
        sc = jnp.where(valid4, sc * (HEAD_DIM ** -0.5), -jnp.inf)
        m = jnp.maximum(jnp.max(sc, axis=-1, keepdims=True), sink)
        p = jnp.exp(sc - m)
        denom = jnp.sum(p, axis=-1, keepdims=True) + jnp.exp(sink - m)
        o = jnp.dot(p.astype(bf16), vpair, preferred_element_type=f32) / denom
        ms = jnp.sum(jnp.where(half4 == kside, o * o, 0.0), axis=-1, keepdims=True) * (1.0 / HEAD_DIM)
        on = o * lax.rsqrt(ms + EPS)
        for pair in range(2):
            parts = []
            for side in range(2):
                g = 2 * pair + side
                x = on[ROW_BLOCK * g:ROW_BLOCK * (g + 1)]
                if side != kside:
                    x = pltpu.roll(x, HEAD_DIM, axis=1)
                parts.append(x)
            blk = 2 * kvh + pair
            out = jnp.where(half == 0, parts[0], parts[1]) * g_all[:, 128 * blk:128 * blk + 128]
            an_ref[:, 128 * blk:128 * blk + 128] = out.astype(an_ref.dtype)


def _cur_block(s):
    return jnp.where(s == N_ROW_BLOCKS - 2, META_BLOCK, jnp.where(s == N_ROW_BLOCKS - 1, SAMPLE_BLOCK, s))


def _seq_of_step(s):
    return jnp.minimum(s // BLOCKS_PER_SEQ, BATCH - 1)


def _prev_block(s):
    return jnp.where((s % BLOCKS_PER_SEQ == 0) | (s == N_ROW_BLOCKS - 2), META_BLOCK, s - 1)


def _cur_tab(s):
    return jnp.where(s == N_ROW_BLOCKS - 2, BLOCKS_PER_SEQ, s % BLOCKS_PER_SEQ)


def _prev_tab(s):
    return jnp.where((s % BLOCKS_PER_SEQ == 0) | (s == N_ROW_BLOCKS - 2), BLOCKS_PER_SEQ, s % BLOCKS_PER_SEQ - 1)


def _attn_prompt(z, sinks, rope_tab, g_attn):
    kcol, vcol = ATTN_DIM // KV_DIM, ATTN_DIM // KV_DIM + 1
    return pl.pallas_call(
        _attn_prompt_kernel,
        grid=(N_ROW_BLOCKS,),
        in_specs=[pl.BlockSpec(memory_space=pltpu.SMEM),
                  pl.BlockSpec((ROW_BLOCK, ATTN_DIM), lambda s: (_cur_block(s), 0)),
                  pl.BlockSpec((ROW_BLOCK, KV_DIM), lambda s: (_cur_block(s), kcol)),
                  pl.BlockSpec((ROW_BLOCK, KV_DIM), lambda s: (_cur_block(s), vcol)),
                  pl.BlockSpec((ROW_BLOCK, KV_DIM), lambda s: (_prev_block(s), kcol)),
                  pl.BlockSpec((ROW_BLOCK, KV_DIM), lambda s: (_prev_block(s), vcol)),
                  pl.BlockSpec((ROW_BLOCK, 256), lambda s: (_cur_tab(s), 0)),
                  pl.BlockSpec((ROW_BLOCK, 256), lambda s: (_prev_tab(s), 0)),
                  pl.BlockSpec((1, ATTN_DIM), lambda s: (0, 0))],
        out_specs=[pl.BlockSpec((ROW_BLOCK, ATTN_DIM), lambda s: (_cur_block(s), 0)),
                   pl.BlockSpec((WINDOW, KV_DIM), lambda s: (_seq_of_step(s), 0)),
                   pl.BlockSpec((WINDOW, KV_DIM), lambda s: (_seq_of_step(s), 0))],
        out_shape=[jax.ShapeDtypeStruct((N_ROWS, ATTN_DIM), bf16),
                   jax.ShapeDtypeStruct((BATCH * WINDOW, KV_DIM), f32),
                   jax.ShapeDtypeStruct((BATCH * WINDOW, KV_DIM), f32)],
        compiler_params=_cparams("arbitrary"),
    )(sinks, z, z, z, z, z, rope_tab, rope_tab, g_attn)


def _attn_sample_kernel(an_in_ref, q_ref, k_ref, v_ref, ck_ref, cv_ref, tab_ref, sink_ref, g_ref, bd_ref,
                        an_ref, nk_ref, nv_ref):
    del an_in_ref
    bt = q_ref.shape[0]
    w = ck_ref.shape[1]
    tab = tab_ref[...]
    q = _rope(q_ref[...], jnp.tile(tab[0:1, :128], (1, 8)), jnp.tile(tab[0:1, 128:], (1, 8)))
    k = _rope(k_ref[...], jnp.tile(tab[0:1, :128], (1, 2)), jnp.tile(tab[0:1, 128:], (1, 2)))
    v = v_ref[...]
    ck = ck_ref[...]
    cv = cv_ref[...]
    bd = bd_ref[...]

    last = lax.broadcasted_iota(jnp.int32, (w, KV_DIM), 0) == w - 1
    for b in range(bt):
        nk_ref[b] = jnp.where(last, k[b:b + 1], pltpu.roll(ck[b], w - 1, axis=0))
        nv_ref[b] = jnp.where(last, v[b:b + 1], pltpu.roll(cv[b], w - 1, axis=0))

    seg = lax.broadcasted_iota(jnp.int32, (bt, SEG), 1) // HEAD_DIM
    scale = HEAD_DIM ** -0.5
    chunks = [jnp.zeros((bt, SEG), f32) for _ in range(N_KV_HEADS)]
    for g in range(GROUP):
        qg = jnp.zeros((bt, SEG), f32)
        for kvh in range(N_KV_HEADS):
            x = q[:, SEG * kvh:SEG * (kvh + 1)]
            sh = (HEAD_DIM * (kvh - g)) % SEG
            if sh:
                x = pltpu.roll(x, sh, axis=1)
            qg = jnp.where(seg == kvh, x, qg)
        prod = (ck * qg[:, None, :]).reshape(bt * w, SEG)
        sc = (_segsum(prod, bd) * scale).reshape(bt, w, SEG)
        sn = _segsum(qg * k, bd) * scale
        sink = sink_ref[g:g + 1, :]
        m = jnp.maximum(jnp.maximum(jnp.max(sc, axis=1), sn), sink)
        p = jnp.exp(sc - m[:, None, :])
        pn = jnp.exp(sn - m)
        denom = jnp.sum(p, axis=1) + pn + jnp.exp(sink - m)
        o = (jnp.sum(p * cv, axis=1) + pn * v) / denom
        ms = _segsum(o * o, bd) * (1.0 / HEAD_DIM)
        on = o * lax.rsqrt(ms + EPS)
        for kvh in range(N_KV_HEADS):
            sh = (HEAD_DIM * (g - kvh)) % SEG
            x = pltpu.roll(on, sh, axis=1) if sh else on
            chunks[kvh] = jnp.where(seg == g, x, chunks[kvh])
    out = jnp.concatenate(chunks, axis=1) * g_ref[...]
    an_ref[...] = out.astype(an_ref.dtype)


def _attn_sample(an, z, cache_k_l, cache_v_l, tab, sinks_exp, g_attn, bd, bt=16):
    w = cache_k_l.shape[1]
    row0 = SAMPLE_ROW0 // bt
    kcol, vcol = ATTN_DIM // KV_DIM, ATTN_DIM // KV_DIM + 1
    cache_spec = pl.BlockSpec((bt, w, KV_DIM), lambda i: (i, 0, 0))
    return pl.pallas_call(
        _attn_sample_kernel,
        grid=(DEC_BATCH // bt,),
        in_specs=[pl.BlockSpec(memory_space=pl.ANY),
                  pl.BlockSpec((bt, ATTN_DIM), lambda i: (row0 + i, 0)),
                  pl.BlockSpec((bt, KV_DIM), lambda i: (row0 + i, kcol)),
                  pl.BlockSpec((bt, KV_DIM), lambda i: (row0 + i, vcol)),
                  cache_spec, cache_spec,
                  pl.BlockSpec((8, 256), lambda i: (0, 0)),
                  pl.BlockSpec((GROUP, SEG), lambda i: (0, 0)),
                  pl.BlockSpec((1, ATTN_DIM), lambda i: (0, 0)),
                  pl.BlockSpec((SEG, SEG), lambda i: (0, 0))],
        out_specs=[pl.BlockSpec((bt, ATTN_DIM), lambda i: (row0 + i, 0)), cache_spec, cache_spec],
        out_shape=[jax.ShapeDtypeStruct((N_ROWS, ATTN_DIM), bf16),
                   jax.ShapeDtypeStruct(cache_k_l.shape, f32),
                   jax.ShapeDtypeStruct(cache_v_l.shape, f32)],
        input_output_aliases={0: 0},
        compiler_params=_cparams("parallel"),
    )(an, z, z, z, cache_k_l, cache_v_l, tab, sinks_exp, g_attn, bd)


def _conv_kernel(z1_ref, z2_ref, p1_ref, p2_ref, s0_ref, s1_ref, w_ref, g_ref, bd_ref, c_ref, utail_ref, ns_ref):
    i = pl.program_id(0)

    def split(b1, b2):
        gb = b1[:, :CONV_DIM]
        gc = jnp.concatenate([b1[:, CONV_DIM:], b2[:, :CONV_DIM // 2]], axis=1)
        return gb, gc * b2[:, CONV_DIM // 2:]

    gb, u = split(z1_ref[...], z2_ref[...])
    _, up = split(p1_ref[...], p2_ref[...])

    @pl.when(i < SAMPLE_BLOCK)
    def _():
        utail_ref[...] = u[ROW_BLOCK - 8:]

    @pl.when(i == SAMPLE_BLOCK)
    def _():
        ns_ref[:, :CONV_DIM] = s1_ref[...]
        ns_ref[:, CONV_DIM:] = u

    row = lax.broadcasted_iota(jnp.int32, u.shape, 0)
    u1 = jnp.where(row == 0, up[7:8], pltpu.roll(u, 1, axis=0))
    u2 = jnp.where(row == 0, up[6:7], jnp.where(row == 1, up[7:8], pltpu.roll(u, 2, axis=0)))
    is_sample = i == SAMPLE_BLOCK
    u1 = jnp.where(is_sample, s1_ref[...], u1)
    u2 = jnp.where(is_sample, s0_ref[...], u2)
    w = w_ref[...]
    y = u2 * w[0:1] + u1 * w[1:2] + u * w[2:3]
    t = gb * y
    bd = bd_ref[...]
    g = g_ref[...]
    for c in range(CONV_DIM // SEG):
        tc = t[:, SEG * c:SEG * (c + 1)]
        ms = _segsum(tc * tc, bd) * (1.0 / HEAD_DIM)
        c_ref[:, SEG * c:SEG * (c + 1)] = (tc * lax.rsqrt(ms + EPS) * g[:, SEG * c:SEG * (c + 1)]).astype(c_ref.dtype)


def _conv_prev_block(i):
    last8_of_meta = N_ROWS // 8 - 1
    return jnp.where(i % BLOCKS_PER_SEQ == 0, last8_of_meta, (ROW_BLOCK // 8) * i - 1)


def _conv_mix(z, state_l, conv_w_l, g_conv, bd):
    wide = 3 * CONV_DIM // 2
    return pl.pallas_call(
        _conv_kernel,
        grid=(N_ROW_BLOCKS,),
        in_specs=[pl.BlockSpec((ROW_BLOCK, wide), lambda i: (i, 1)),
                  pl.BlockSpec((ROW_BLOCK, wide), lambda i: (i, 2)),
                  pl.BlockSpec((8, wide), lambda i: (_conv_prev_block(i), 1)),
                  pl.BlockSpec((8, wide), lambda i: (_conv_prev_block(i), 2)),
                  pl.BlockSpec((DEC_BATCH, CONV_DIM), lambda i: (0, 0)),
                  pl.BlockSpec((DEC_BATCH, CONV_DIM), lambda i: (0, 1)),
                  pl.BlockSpec((CONV_WIDTH, CONV_DIM), lambda i: (0, 0)),
                  pl.BlockSpec((1, CONV_DIM), lambda i: (0, 0)),
                  pl.BlockSpec((SEG, SEG), lambda i: (0, 0))],
        out_specs=[pl.BlockSpec((ROW_BLOCK, CONV_DIM), lambda i: (i, 0)),
                   pl.BlockSpec((8, CONV_DIM), lambda i: (_seq_of_step(i), 0)),
                   pl.BlockSpec((DEC_BATCH, 2 * CONV_DIM), lambda i: (0, 0))],
        out_shape=[jax.ShapeDtypeStruct((N_ROWS, CONV_DIM), bf16),
                   jax.ShapeDtypeStruct((BATCH * 8, CONV_DIM), f32),
                   jax.ShapeDtypeStruct((DEC_BATCH, 2 * CONV_DIM), f32)],
        compiler_params=_cparams("arbitrary"),
    )(z, z, z, z, state_l, state_l, conv_w_l, g_conv, bd)


def _outproj_kernel(a_ref, c_ref, wa_ref, wc_ref, h_ref, g_ref, ho_ref, no_ref):
    acc = jnp.dot(a_ref[...], wa_ref[...], preferred_element_type=f32)
    acc += jnp.dot(c_ref[...], wc_ref[...], preferred_element_type=f32)
    hn = h_ref[...] + acc
    ho_ref[...] = hn
    no_ref[...] = _rms(hn, g_ref[...]).astype(no_ref.dtype)


def _out_proj(an, c, w_out_bf, layer, h, g_next, tm=384):
    row = lambda i: (i, 0)
    return pl.pallas_call(
        _outproj_kernel,
        grid=(N_ROWS // tm,),
        in_specs=[pl.BlockSpec((tm, ATTN_DIM), row),
                  pl.BlockSpec((tm, CONV_DIM), row),
                  pl.BlockSpec((None, ATTN_DIM, D_MODEL), lambda i: (layer, 0, 0)),
                  pl.BlockSpec((None, CONV_DIM, D_MODEL), lambda i: (layer, 1, 0)),
                  pl.BlockSpec((tm, D_MODEL), row),
                  pl.BlockSpec((1, D_MODEL), lambda i: (0, 0))],
        out_specs=[pl.BlockSpec((tm, D_MODEL), row), pl.BlockSpec((tm, D_MODEL), row)],
        out_shape=[jax.ShapeDtypeStruct((N_ROWS, D_MODEL), f32),
                   jax.ShapeDtypeStruct((N_ROWS, D_MODEL), bf16)],
        compiler_params=_cparams("parallel"),
    )(an, c, w_out_bf, w_out_bf, h, g_next)


def _router_kernel(h_ref, g_ref, r_ref, tri_ref, cmb_ref, dest_ref, carry_ref):
    i = pl.program_id(0)
    tm = h_ref.shape[0]

    @pl.when(i == 0)
    def _():
        carry_ref[...] = jnp.zeros_like(carry_ref)

    n = _rms(h_ref[...], g_ref[...])
    logits = jnp.dot(n, r_ref[...], preferred_element_type=f32, precision=lax.Precision.HIGHEST)
    idx = lax.broadcasted_iota(jnp.int32, logits.shape, 1)
    m1 = jnp.max(logits, axis=-1, keepdims=True)
    i1 = jnp.min(jnp.where(logits == m1, idx, N_EXPERTS), axis=-1, keepdims=True)
    rest = jnp.where(idx == i1, -jnp.inf, logits)
    m2 = jnp.max(rest, axis=-1, keepdims=True)
    i2 = jnp.min(jnp.where(rest == m2, idx, N_EXPERTS), axis=-1, keepdims=True)
    e2 = jnp.exp(m2 - m1)
    den = 1.0 + e2
    rowid = i * tm + lax.broadcasted_iota(jnp.int32, logits.shape, 0)
    real = (rowid < TAIL_ROW0) | (rowid >= N_ROWS - N_META)
    sel = ((idx == i1) | (idx == i2)) & real
    cmb_ref[...] = jnp.where(sel, jnp.where(idx == i1, 1.0 / den, e2 / den), 0.0)
    self = jnp.where(sel, 1.0, 0.0)
    incl = jnp.dot(tri_ref[...], self.astype(bf16), preferred_element_type=f32)
    carry = carry_ref[...]
    dest_ref[...] = jnp.where(sel, incl - self + carry, -1.0).astype(jnp.int32)
    carry_ref[...] = carry + incl[tm - 1:tm, :]


def _router(h, g, router, tm=768):
    tri = jnp.asarray(np.tril(np.ones((tm, tm), np.float32)), dtype=bf16)
    return pl.pallas_call(
        _router_kernel,
        grid=(N_ROWS // tm,),
        in_specs=[pl.BlockSpec((tm, D_MODEL), lambda i: (i, 0)),
                  pl.BlockSpec((1, D_MODEL), lambda i: (0, 0)),
                  pl.BlockSpec((D_MODEL, N_EXPERTS), lambda i: (0, 0)),
                  pl.BlockSpec((tm, tm), lambda i: (0, 0))],
        out_specs=[pl.BlockSpec((tm, N_EXPERTS), lambda i: (i, 0)),
                   pl.BlockSpec((tm, N_EXPERTS), lambda i: (i, 0))],
        out_shape=[jax.ShapeDtypeStruct((N_ROWS, N_EXPERTS), f32),
                   jax.ShapeDtypeStruct((N_ROWS, N_EXPERTS), jnp.int32)],
        scratch_shapes=[pltpu.VMEM((1, N_EXPERTS), f32)],
        compiler_params=_cparams("arbitrary"),
    )(h, g, router, tri)


MOE_CHUNK = 256
MOE_GATHER = 768
MOE_SUB = 768
MOE_VISIT_ROWS = 3 * MOE_SUB
MOE_CAP = 4 * MOE_VISIT_ROWS
MOE_TF = 256
N_CHUNKS = N_ROWS // MOE_CHUNK
N_GATHER = N_ROWS // MOE_GATHER
N_FSTEPS = D_FF // MOE_TF
MOE_MAX_VISITS = (2 * N_ROWS) // MOE_VISIT_ROWS + N_EXPERTS


def _moe_kernel(vis_e, vis_k, vis_nsub, vis_ok, rlo, xn_ref, dest_ref, w1_ref, w3_ref, w2_ref, y_ref, acc_ref):
    v, p = pl.program_id(0), pl.program_id(1)
    ok = vis_ok[v] == 1
    e = vis_e[v]
    base = vis_k[v] * MOE_VISIT_ROWS

    @pl.when(ok & (p == 0))
    def _():
        y_ref[...] = jnp.zeros_like(y_ref)

    @pl.when(ok & (p < N_GATHER))
    def _():
        lo = jnp.maximum(rlo[e * (N_GATHER + 1) + p] - base, 0)
        hi = jnp.minimum(rlo[e * (N_GATHER + 1) + p + 1] - base, MOE_VISIT_ROWS)

        @pl.when(hi > lo)
        def _():
            dest = dest_ref[...] - base
            win = lax.broadcasted_iota(jnp.int32, (MOE_CHUNK, MOE_GATHER), 0)

            def tile_body(t, carry):
                off = pl.multiple_of(t * MOE_CHUNK, MOE_CHUNK)
                onehot = jnp.where(win == dest - off, 1.0, 0.0).astype(bf16)
                rows = jnp.dot(onehot, xn_ref[...], preferred_element_type=f32)
                y_ref[pl.ds(off, MOE_CHUNK), :] += rows.astype(bf16)
                return carry

            lax.fori_loop(lo // MOE_CHUNK, (hi - 1) // MOE_CHUNK + 1, tile_body, 0)

    @pl.when(ok & (p >= N_GATHER))
    def _():
        f = p - N_GATHER
        w1 = w1_ref[...].astype(bf16)
        w3 = w3_ref[...].astype(bf16)
        w2 = w2_ref[...].astype(bf16)
        for r in range(MOE_VISIT_ROWS // MOE_SUB):
            rows = slice(r * MOE_SUB, (r + 1) * MOE_SUB)

            @pl.when(r < vis_nsub[v])
            def _():
                @pl.when(f == 0)
                def _():
                    acc_ref[rows, :] = jnp.zeros((MOE_SUB, D_MODEL), f32)

                x = y_ref[rows, :]
                a = jnp.dot(x, w1, preferred_element_type=f32)
                b = jnp.dot(x, w3, preferred_element_type=f32)
                act = (a * jax.nn.sigmoid(a) * b).astype(bf16)
                acc_ref[rows, :] += jnp.dot(act, w2, preferred_element_type=f32)

                @pl.when(f == N_FSTEPS - 1)
                def _():
                    y_ref[rows, :] = acc_ref[rows, :].astype(y_ref.dtype)


def _moe_experts(xn, dest_exp, tables, w1, w3, w2, layer_idx):
    def chunk(v, p, ve, vk, vn, vok, rlo):
        return jnp.where(vok[v] == 1, jnp.minimum(p, N_GATHER - 1), N_GATHER - 1)

    def fstep(v, p, ve, vk, vn, vok, rlo):
        return jnp.where(vok[v] == 1, jnp.maximum(p - N_GATHER, 0), N_FSTEPS - 1)

    grid_spec = pltpu.PrefetchScalarGridSpec(
        num_scalar_prefetch=5,
        grid=(MOE_MAX_VISITS, N_GATHER + N_FSTEPS),
        in_specs=[pl.BlockSpec((MOE_GATHER, D_MODEL), lambda v, p, *t: (chunk(v, p, *t), 0)),
                  pl.BlockSpec((None, None, 1, MOE_GATHER), lambda v, p, *t: (t[0][v], chunk(v, p, *t), 0, 0)),
                  pl.BlockSpec((None, None, D_MODEL, MOE_TF), lambda v, p, *t: (layer_idx, t[0][v], 0, fstep(v, p, *t))),
                  pl.BlockSpec((None, None, D_MODEL, MOE_TF), lambda v, p, *t: (layer_idx, t[0][v], 0, fstep(v, p, *t))),
                  pl.BlockSpec((None, None, MOE_TF, D_MODEL), lambda v, p, *t: (layer_idx, t[0][v], fstep(v, p, *t), 0))],
        out_specs=pl.BlockSpec((None, MOE_VISIT_ROWS, D_MODEL), lambda v, p, *t: (t[0][v], t[1][v], 0),
                               pipeline_mode=pl.Buffered(1)),
        scratch_shapes=[pltpu.VMEM((MOE_VISIT_ROWS, D_MODEL), f32)],
    )
    return pl.pallas_call(
        _moe_kernel,
        grid_spec=grid_spec,
        out_shape=jax.ShapeDtypeStruct((N_EXPERTS, MOE_CAP, D_MODEL), bf16),
        compiler_params=_cparams("arbitrary", "arbitrary"),
    )(*tables, xn, dest_exp, w1, w3, w2)


def _combine_kernel(t0, nwin, h_ref, dest_ref, cmb_ref, g_ref, *rest):
    ywin = rest[:2 * N_EXPERTS]
    yp_ref, ys_ref, acc_ref = rest[2 * N_EXPERTS:]
    c = pl.program_id(0)
    acc_ref[...] = h_ref[...]
    dest = dest_ref[...]
    cmb = cmb_ref[...]
    lane = lax.broadcasted_iota(jnp.int32, (MOE_CHUNK, MOE_CHUNK), 1)
    for e in range(N_EXPERTS):
        for w in range(2):
            @pl.when(nwin[e * N_CHUNKS + c] > w)
            def _():
                rel = dest[:, e:e + 1] - MOE_CHUNK * (t0[e * N_CHUNKS + c] + w)
                onehot = jnp.where(rel == lane, 1.0, 0.0).astype(bf16)
                got = jnp.dot(onehot, ywin[2 * e + w][...], preferred_element_type=f32)
                acc_ref[...] += cmb[:, e:e + 1] * got

    n = _rms(acc_ref[...], g_ref[...])
    n_prompt_chunks = N_PROMPT_ROWS // MOE_CHUNK

    @pl.when(c < n_prompt_chunks)
    def _():
        yp_ref[...] = n

    @pl.when(c == n_prompt_chunks)
    def _():
        ys_ref[...] = n[:DEC_BATCH]


def _moe_combine(y_sorted, win_tables, h, dest_tok, cmb, g_final):
    t0, t0c, t1c, nwin = win_tables
    row = lambda c, *t: (c, 0)
    n_prompt_chunks = N_PROMPT_ROWS // MOE_CHUNK
    wins = []
    for e in range(N_EXPERTS):
        wins.append(pl.BlockSpec((None, MOE_CHUNK, D_MODEL), lambda c, a, b, *t, e=e: (e, a[e * N_CHUNKS + c], 0)))
        wins.append(pl.BlockSpec((None, MOE_CHUNK, D_MODEL), lambda c, a, b, *t, e=e: (e, b[e * N_CHUNKS + c], 0)))
    grid_spec = pltpu.PrefetchScalarGridSpec(
        num_scalar_prefetch=4,
        grid=(N_CHUNKS,),
        in_specs=[pl.BlockSpec((MOE_CHUNK, D_MODEL), row),
                  pl.BlockSpec((MOE_CHUNK, N_EXPERTS), row),
                  pl.BlockSpec((MOE_CHUNK, N_EXPERTS), row),
                  pl.BlockSpec((1, D_MODEL), lambda c, *t: (0, 0))] + wins,
        out_specs=[pl.BlockSpec((MOE_CHUNK, D_MODEL), lambda c, *t: (jnp.minimum(c, n_prompt_chunks - 1), 0)),
                   pl.BlockSpec((DEC_BATCH, D_MODEL), lambda c, *t: (0, 0))],
        scratch_shapes=[pltpu.VMEM((MOE_CHUNK, D_MODEL), f32)],
    )

    def body(t0c_ref, t1c_ref, t0_ref, nwin_ref, *refs):
        _combine_kernel(t0_ref, nwin_ref, *refs)

    return pl.pallas_call(
        body,
        grid_spec=grid_spec,
        out_shape=[jax.ShapeDtypeStruct((N_PROMPT_ROWS, D_MODEL), f32),
                   jax.ShapeDtypeStruct((DEC_BATCH, D_MODEL), f32)],
        compiler_params=_cparams("arbitrary"),
    )(t0c, t1c, t0, nwin, h, dest_tok, cmb, g_final, *([y_sorted] * (2 * N_EXPERTS)))


def _routing_tables(dest_tok):
    i32 = jnp.int32
    cnt = (dest_tok >= 0).astype(i32).reshape(N_CHUNKS, MOE_CHUNK, N_EXPERTS).sum(axis=1)
    csum = jnp.cumsum(cnt, axis=0)
    rank_lo = jnp.concatenate([jnp.zeros((1, N_EXPERTS), i32), csum], axis=0).T
    n_e = csum[-1]
    nv = (n_e + MOE_VISIT_ROWS - 1) // MOE_VISIT_ROWS
    cum_nv = jnp.cumsum(nv)
    total = cum_nv[-1]
    v = jnp.arange(MOE_MAX_VISITS, dtype=i32)
    vv = jnp.minimum(v, jnp.maximum(total - 1, 0))
    vis_e = jnp.minimum(jnp.sum((vv[:, None] >= cum_nv[None, :]).astype(i32), axis=1), N_EXPERTS - 1)
    vis_k = vv - (cum_nv - nv)[vis_e]
    vis_ok = (v < total).astype(i32)
    rows = jnp.clip(n_e[vis_e] - vis_k * MOE_VISIT_ROWS, 0, MOE_VISIT_ROWS)
    vis_nsub = jnp.where(vis_ok == 1, (rows + MOE_SUB - 1) // MOE_SUB, 0)
    t0 = rank_lo[:, :N_CHUNKS] // MOE_CHUNK
    tmax = jnp.maximum(nv * (MOE_VISIT_ROWS // MOE_CHUNK) - 1, 0)[:, None]
    moe_tables = (vis_e, vis_k, vis_nsub.astype(i32), vis_ok, rank_lo[:, ::MOE_GATHER // MOE_CHUNK].reshape(-1))
    t_last = (rank_lo[:, 1:] - 1) // MOE_CHUNK
    nwin = jnp.where(cnt.T > 0, t_last - t0 + 1, 0)
    win_tables = (t0.reshape(-1), jnp.minimum(t0, tmax).reshape(-1), jnp.minimum(t0 + 1, tmax).reshape(-1),
                  nwin.reshape(-1))
    return moe_tables, win_tables


def _ffn_kernel(x_ref, w1_ref, w3_ref, w2_ref, h_ref, g_ref, ho_ref, no_ref, acc_ref):
    f = pl.program_id(1)

    @pl.when(f == 0)
    def _():
        acc_ref[...] = h_ref[...]

    x = x_ref[...]
    a = jnp.dot(x, w1_ref[...].astype(bf16), preferred_element_type=f32)
    b = jnp.dot(x, w3_ref[...].astype(bf16), preferred_element_type=f32)
    act = (a * jax.nn.sigmoid(a) * b).astype(bf16)
    acc_ref[...] += jnp.dot(act, w2_ref[...].astype(bf16), preferred_element_type=f32)

    @pl.when(f == pl.num_programs(1) - 1)
    def _():
        hn = acc_ref[...]
        ho_ref[...] = hn
        no_ref[...] = _rms(hn, g_ref[...]).astype(no_ref.dtype)


def _ffn(xn, w1, w3, w2, layer_idx, h, g_next, tm=768, tf=256):
    row = lambda i, f: (i, 0)
    once = pl.Buffered(1)
    return pl.pallas_call(
        _ffn_kernel,
        grid=(N_ROWS // tm, D_FF // tf),
        in_specs=[pl.BlockSpec((tm, D_MODEL), row),
                  pl.BlockSpec((None, D_MODEL, tf), lambda i, f: (layer_idx, 0, f)),
                  pl.BlockSpec((None, D_MODEL, tf), lambda i, f: (layer_idx, 0, f)),
                  pl.BlockSpec((None, tf, D_MODEL), lambda i, f: (layer_idx, f, 0)),
                  pl.BlockSpec((tm, D_MODEL), row, pipeline_mode=once),
                  pl.BlockSpec((1, D_MODEL), lambda i, f: (0, 0))],
        out_specs=[pl.BlockSpec((tm, D_MODEL), row, pipeline_mode=once),
                   pl.BlockSpec((tm, D_MODEL), row, pipeline_mode=once)],
        out_shape=[jax.ShapeDtypeStruct((N_ROWS, D_MODEL), f32),
                   jax.ShapeDtypeStruct((N_ROWS, D_MODEL), bf16)],
        scratch_shapes=[pltpu.VMEM((tm, D_MODEL), f32)],
        compiler_params=_cparams("parallel", "arbitrary"),
    )(xn, w1, w3, w2, h, g_next)


def _rope_patterns(pos):
    half = ROT_DIM // 2
    inv_freq = jnp.power(ROPE_THETA, -jnp.arange(half, dtype=f32) * 2.0 / ROT_DIM)
    ang = pos.astype(f32)[:, None] * inv_freq[None, :]
    cos, sin = jnp.cos(ang), jnp.sin(ang)
    n = pos.shape[0]
    cos_h = jnp.concatenate([cos, cos, jnp.ones((n, HEAD_DIM - ROT_DIM), f32)], axis=1)
    sin_h = jnp.concatenate([-sin, sin, jnp.zeros((n, HEAD_DIM - ROT_DIM), f32)], axis=1)
    return jnp.concatenate([cos_h, cos_h, sin_h, sin_h], axis=1)


def _block_diag_ones():
    i = np.arange(SEG) // HEAD_DIM
    return jnp.asarray((i[:, None] == i[None, :]).astype(np.float32), dtype=bf16)


def kernel(x_prompt, x_sample, cache_k, cache_v, state_conv, meta_tokens, g_mix, w_in, conv_w, attn_sinks,
           g_attn_out, g_conv_out, w_out, g_ffn, dense_w1, dense_w3, dense_w2, moe_router, moe_w1, moe_w3,
           moe_w2, g_final):
    w_cache = cache_k.shape[2]
    tail = jnp.concatenate([x_sample.reshape(DEC_BATCH, D_MODEL), jnp.zeros((META_ROW_IN_BLOCK, D_MODEL), f32),
                            meta_tokens.astype(f32)], axis=0)

    prompt_pos = N_META + jnp.arange(SEQ, dtype=jnp.int32)
    tail_pos = jnp.maximum(jnp.arange(ROW_BLOCK, dtype=jnp.int32) - META_ROW_IN_BLOCK, 0)
    rope_tab = _rope_patterns(jnp.concatenate([prompt_pos, tail_pos]))
    sample_tab = _rope_patterns(jnp.full((8,), PAST_LEN, jnp.int32))
    bd = _block_diag_ones()
    w_out_bf = w_out.astype(bf16)
    cache_k = cache_k.reshape(DEPTH, DEC_BATCH, w_cache, KV_DIM)
    cache_v = cache_v.reshape(DEPTH, DEC_BATCH, w_cache, KV_DIM)

    h, xn = _norm_rows(x_prompt.reshape(N_PROMPT_ROWS, D_MODEL), tail, g_mix[0:1])
    nk_p, nv_p, nc_p, nk_s, nv_s, nc_s = [], [], [], [], [], []
    for l in range(DEPTH):
        z = _in_proj(xn, w_in, l)
        g_attn = g_attn_out[l:l + 1]
        an, nk_prompt, nv_prompt = _attn_prompt(z, attn_sinks[l], rope_tab, g_attn)
        sinks_exp = jnp.repeat(attn_sinks[l].reshape(N_KV_HEADS, GROUP).T, HEAD_DIM, axis=1)
        an, nk, nv = _attn_sample(an, z, cache_k[l], cache_v[l], sample_tab, sinks_exp, g_attn, bd)
        state_l = state_conv[l].reshape(DEC_BATCH, (CONV_WIDTH - 1) * CONV_DIM)
        c, u_tail, conv_state = _conv_mix(z, state_l, conv_w[l], g_conv_out[l:l + 1], bd)
        h, xn = _out_proj(an, c, w_out_bf, l, h, g_ffn[l:l + 1])

        j = l // 2
        if l % 2 == 0:
            h, xn = _ffn(xn, dense_w1, dense_w3, dense_w2, j, h, g_mix[l + 1:l + 2])
        else:
            cmb, dest_tok = _router(h, g_ffn[l:l + 1], moe_router[j])
            moe_tables, win_tables = _routing_tables(dest_tok)
            dest_exp = dest_tok.T.reshape(N_EXPERTS, N_GATHER, 1, MOE_GATHER)
            y_sorted = _moe_experts(xn, dest_exp, moe_tables, moe_w1, moe_w3, moe_w2, j)
            y_prompt, y_sample = _moe_combine(y_sorted, win_tables, h, dest_tok, cmb, g_final[None])

        nk_p.append(nk_prompt.reshape(BATCH, WINDOW, N_KV_HEADS, HEAD_DIM))
        nv_p.append(nv_prompt.reshape(BATCH, WINDOW, N_KV_HEADS, HEAD_DIM))
        nc_p.append(u_tail.reshape(BATCH, 8, CONV_DIM)[:, 8 - (CONV_WIDTH - 1):])
        nk_s.append(nk.reshape(DEC_BATCH, w_cache, N_KV_HEADS, HEAD_DIM))
        nv_s.append(nv.reshape(DEC_BATCH, w_cache, N_KV_HEADS, HEAD_DIM))
        nc_s.append(conv_state.reshape(DEC_BATCH, CONV_WIDTH - 1, CONV_DIM))

    return (y_prompt.reshape(BATCH, SEQ, D_MODEL), y_sample.reshape(DEC_BATCH, 1, D_MODEL),
            jnp.stack(nk_p), jnp.stack(nv_p), jnp.stack(nc_p), jnp.stack(nk_s), jnp.stack(nv_s), jnp.stack(nc_s))
```

```python
import functools

import jax
import jax.numpy as jnp
import numpy as np
from jax import lax
from jax.experimental import pallas as pl
from jax.experimental.pallas import tpu as pltpu

D_MODEL = 2048
BATCH = 4
SEQ = 2048
DEPTH = 2
DEC_BATCH = 128
PAST_LEN = 8192
HEAD_DIM = 64
ATTN_DIM = D_MODEL // 2
N_Q_HEADS = ATTN_DIM // HEAD_DIM
N_KV_HEADS = N_Q_HEADS // 4
GROUP = N_Q_HEADS // N_KV_HEADS
KV_DIM = N_KV_HEADS * HEAD_DIM
CONV_DIM = D_MODEL - ATTN_DIM
CONV_WIDTH = 3
IN_DIM = ATTN_DIM + 2 * KV_DIM + 3 * CONV_DIM
WINDOW = 128
ROPE_THETA = 500000.0
ROT_DIM = HEAD_DIM // 4
N_META = 16
D_FF = 7 * D_MODEL // 2
N_EXPERTS = 8
EPS = 1e-5

ROW_BLOCK = 128
N_PROMPT_ROWS = BATCH * SEQ
SAMPLE_ROW0 = N_PROMPT_ROWS
TAIL_ROW0 = SAMPLE_ROW0 + DEC_BATCH
N_ROWS = TAIL_ROW0 + ROW_BLOCK
N_ROW_BLOCKS = N_ROWS // ROW_BLOCK
BLOCKS_PER_SEQ = SEQ // ROW_BLOCK
SAMPLE_BLOCK = SAMPLE_ROW0 // ROW_BLOCK
META_BLOCK = TAIL_ROW0 // ROW_BLOCK
META_ROW_IN_BLOCK = ROW_BLOCK - N_META
SEG = 256

VMEM_LIMIT_V7X = 56 * 1024 * 1024

bf16 = jnp.bfloat16
f32 = jnp.float32


def _cparams(*sem):
    return pltpu.CompilerParams(dimension_semantics=sem, vmem_limit_bytes=VMEM_LIMIT_V7X)


def _rms(x, g):
    ms = jnp.mean(x * x, axis=-1, keepdims=True)
    return (x * lax.rsqrt(ms + EPS)) * g


def _segsum(x, bd):
    x1 = x.astype(bf16)
    r1 = x - x1.astype(f32)
    x2 = r1.astype(bf16)
    x3 = (r1 - x2.astype(f32)).astype(bf16)
    d = functools.partial(jnp.dot, preferred_element_type=f32)
    return d(x1, bd) + d(x2, bd) + d(x3, bd)


def _rope(x, cos_t, sin_t):
    n = x.shape[-1]
    lane = lax.broadcasted_iota(jnp.int32, x.shape, x.ndim - 1) % HEAD_DIM
    nxt = pltpu.roll(x, n - ROT_DIM // 2, axis=x.ndim - 1)
    prv = pltpu.roll(x, ROT_DIM // 2, axis=x.ndim - 1)
    return x * cos_t + jnp.where(lane < ROT_DIM // 2, nxt, prv) * sin_t


def _norm_kernel(xp_ref, tail_ref, g_ref, h_ref, o_ref):
    is_tail = pl.program_id(0) == pl.num_programs(0) - 1
    x = jnp.where(is_tail, tail_ref[...], xp_ref[...])
    h_ref[...] = x
    o_ref[...] = _rms(x, g_ref[...]).astype(o_ref.dtype)


def _norm_rows(x_prompt_rows, tail_rows, g):
    tm = tail_rows.shape[0]
    n_prompt_tiles = N_PROMPT_ROWS // tm
    row = lambda i: (i, 0)
    return pl.pallas_call(
        _norm_kernel,
        grid=(N_ROWS // tm,),
        in_specs=[pl.BlockSpec((tm, D_MODEL), lambda i: (jnp.minimum(i, n_prompt_tiles - 1), 0)),
                  pl.BlockSpec((tm, D_MODEL), lambda i: (0, 0)),
                  pl.BlockSpec((1, D_MODEL), lambda i: (0, 0))],
        out_specs=[pl.BlockSpec((tm, D_MODEL), row), pl.BlockSpec((tm, D_MODEL), row)],
        out_shape=[jax.ShapeDtypeStruct((N_ROWS, D_MODEL), f32),
                   jax.ShapeDtypeStruct((N_ROWS, D_MODEL), bf16)],
        compiler_params=_cparams("parallel"),
    )(x_prompt_rows, tail_rows, g)


def _inproj_kernel(x_ref, w_ref, o_ref, wbf_ref):
    @pl.when(pl.program_id(1) == 0)
    def _():
        wbf_ref[...] = w_ref[...].astype(bf16)

    o_ref[...] = jnp.dot(x_ref[...], wbf_ref[...], preferred_element_type=f32)


def _in_proj(xn, w_in, layer, tm=768, tn=768):
    return pl.pallas_call(
        _inproj_kernel,
        grid=(IN_DIM // tn, N_ROWS // tm),
        in_specs=[pl.BlockSpec((tm, D_MODEL), lambda j, i: (i, 0)),
                  pl.BlockSpec((None, D_MODEL, tn), lambda j, i: (layer, 0, j))],
        out_specs=pl.BlockSpec((tm, tn), lambda j, i: (i, j)),
        out_shape=jax.ShapeDtypeStruct((N_ROWS, IN_DIM), f32),
        scratch_shapes=[pltpu.VMEM((D_MODEL, tn), bf16)],
        compiler_params=_cparams("parallel", "arbitrary"),
    )(xn, w_in)


def _attn_prompt_kernel(sink_ref, q_ref, kc_ref, vc_ref, kp_ref, vp_ref, tc_ref, tp_ref, g_ref,
                        an_ref, nk_ref, nv_ref):
    s = pl.program_id(0)
    is_meta = s == N_ROW_BLOCKS - 2
    j = s % BLOCKS_PER_SEQ
    tc = tc_ref[...]
    tp = tp_ref[...]
    q = _rope(q_ref[...], jnp.tile(tc[:, :128], (1, 8)), jnp.tile(tc[:, 128:], (1, 8)))
    kc = _rope(kc_ref[...], jnp.tile(tc[:, :128], (1, 2)), jnp.tile(tc[:, 128:], (1, 2)))
    kp = _rope(kp_ref[...], jnp.tile(tp[:, :128], (1, 2)), jnp.tile(tp[:, 128:], (1, 2)))
    nk_ref[...] = kc
    nv_ref[...] = vc_ref[...]

    kall = jnp.concatenate([kp, kc], axis=0).astype(bf16)
    vall = jnp.concatenate([vp_ref[...], vc_ref[...]], axis=0).astype(bf16)

    row = lax.broadcasted_iota(jnp.int32, (GROUP * ROW_BLOCK, 2 * ROW_BLOCK), 0) % ROW_BLOCK
    col = lax.broadcasted_iota(jnp.int32, (GROUP * ROW_BLOCK, 2 * ROW_BLOCK), 1)
    lo_prev = jnp.where(is_meta, ROW_BLOCK, jnp.where(j == 0, META_ROW_IN_BLOCK, 0))
    lo_cur = jnp.where(is_meta, META_ROW_IN_BLOCK, 0)
    ccur = col - ROW_BLOCK
    valid4 = (((col < ROW_BLOCK) & (col >= row) & (col >= lo_prev))
              | ((ccur >= 0) & (ccur <= row) & (ccur >= lo_cur)))
    half = lax.broadcasted_iota(jnp.int32, (ROW_BLOCK, 128), 1) // HEAD_DIM
    half4 = lax.broadcasted_iota(jnp.int32, (GROUP * ROW_BLOCK, 128), 1) // HEAD_DIM
    g_all = g_ref[...]

    for kvh in range(N_KV_HEADS):
        kside = kvh % 2
        kpair = kall[:, 128 * (kvh // 2):128 * (kvh // 2) + 128]
        vpair = vall[:, 128 * (kvh // 2):128 * (kvh // 2) + 128]
        qs, sinks = [], []
        for g in range(GROUP):
            h = GROUP * kvh + g
            x = q[:, 128 * (h // 2):128 * (h // 2) + 128]
            if h % 2 != kside:
                x = pltpu.roll(x, HEAD_DIM, axis=1)
            qs.append(jnp.where(half == kside, x, 0.0))
            sinks.append(jnp.full((ROW_BLOCK, 1), sink_ref[h], f32))
        q4 = jnp.concatenate(qs, axis=0).astype(bf16)
        sink = jnp.concatenate(sinks, axis=0)
        sc = lax.dot_general(q4, kpair, (((1,), (1,)), ((), ())), preferred_element_type=f32)
        sc = jnp.where(valid4, sc * (HEAD_DIM ** -0.5), -jnp.inf)
        m = jnp.maximum(jnp.max(sc, axis=-1, keepdims=True), sink)
        p = jnp.exp(sc - m)
        denom = jnp.sum(p, axis=-1, keepdims=True) + jnp.exp(sink - m)
        o = jnp.dot(p.astype(bf16), vpair, preferred_element_type=f32) / denom
        ms = jnp.sum(jnp.where(half4 == kside, o * o, 0.0), axis=-1, keepdims=True) * (1.0 / HEAD_DIM)
        on = o * lax.rsqrt(ms + EPS)
        for pair in range(2):
            parts = []
            for side in range(2):
                g = 2 * pair + side
                x = on[ROW_BLOCK * g:ROW_BLOCK * (g + 1)]
                if side != kside:
                    x = pltpu.roll(x, HEAD_DIM, axis=1)
                parts.append(x)
            blk = 2 * kvh + pair
            out = jnp.where(half == 0, parts[0], parts[1]) * g_all[:, 128 * blk:128 * blk + 128]
            an_ref[:, 128 * blk:128 * blk + 128] = out.astype(an_ref.dtype)


def _cur_block(s):
    return jnp.where(s == N_ROW_BLOCKS - 2, META_BLOCK, jnp.where(s == N_ROW_BLOCKS - 1, SAMPLE_BLOCK, s))


def _seq_of_step(s):
    return jnp.minimum(s // BLOCKS_PER_SEQ, BATCH - 1)


def _prev_block(s):
    return jnp.where((s % BLOCKS_PER_SEQ == 0) | (s == N_ROW_BLOCKS - 2), META_BLOCK, s - 1)


def _cur_tab(s):
    return jnp.where(s == N_ROW_BLOCKS - 2, BLOCKS_PER_SEQ, s % BLOCKS_PER_SEQ)


def _prev_tab(s):
    return jnp.where((s % BLOCKS_PER_SEQ == 0) | (s == N_ROW_BLOCKS - 2), BLOCKS_PER_SEQ, s % BLOCKS_PER_SEQ - 1)


def _attn_prompt(z, sinks, rope_tab, g_attn):
    kcol, vcol = ATTN_DIM // KV_DIM, ATTN_DIM // KV_DIM + 1
    return pl.pallas_call(
        _attn_prompt_kernel,
        grid=(N_ROW_BLOCKS,),
        in_specs=[pl.BlockSpec(memory_space=pltpu.SMEM),
                  pl.BlockSpec((ROW_BLOCK, ATTN_DIM), lambda s: (_cur_block(s), 0)),
                  pl.BlockSpec((ROW_BLOCK, KV_DIM), lambda s: (_cur_block(s), kcol)),
                  pl.BlockSpec((ROW_BLOCK, KV_DIM), lambda s: (_cur_block(s), vcol)),
                  pl.BlockSpec((ROW_BLOCK, KV_DIM), lambda s: (_prev_block(s), kcol)),
                  pl.BlockSpec((ROW_BLOCK, KV_DIM), lambda s: (_prev_block(s), vcol)),
                  pl.BlockSpec((ROW_BLOCK, 256), lambda s: (_cur_tab(s), 0)),
                  pl.BlockSpec((ROW_BLOCK, 256), lambda s: (_prev_tab(s), 0)),
                  pl.BlockSpec((1, ATTN_DIM), lambda s: (0, 0))],
        out_specs=[pl.BlockSpec((ROW_BLOCK, ATTN_DIM), lambda s: (_cur_block(s), 0)),
                   pl.BlockSpec((WINDOW, KV_DIM), lambda s: (s // BLOCKS_PER_SEQ, 0)),
                   pl.BlockSpec((WINDOW, KV_DIM), lambda s: (s // BLOCKS_PER_SEQ, 0))],
        out_shape=[jax.ShapeDtypeStruct((N_ROWS, ATTN_DIM), bf16),
                   jax.ShapeDtypeStruct(((BATCH + 1) * WINDOW, KV_DIM), f32),
                   jax.ShapeDtypeStruct(((BATCH + 1) * WINDOW, KV_DIM), f32)],
        compiler_params=_cparams("arbitrary"),
    )(sinks, z, z, z, z, z, rope_tab, rope_tab, g_attn)


def _attn_sample_kernel(q_ref, k_ref, v_ref, ck_ref, cv_ref, tab_ref, sink_ref, g_ref, bd_ref,
                        an_ref, nk_ref, nv_ref):
    bt = q_ref.shape[0]
    w = ck_ref.shape[1]
    tab = tab_ref[...]
    q = _rope(q_ref[...], jnp.tile(tab[0:1, :128], (1, 8)), jnp.tile(tab[0:1, 128:], (1, 8)))
    k = _rope(k_ref[...], jnp.tile(tab[0:1, :128], (1, 2)), jnp.tile(tab[0:1, 128:], (1, 2)))
    v = v_ref[...]
    ck = ck_ref[...]
    cv = cv_ref[...]
    bd = bd_ref[...]

    last = lax.broadcasted_iota(jnp.int32, (w, KV_DIM), 0) == w - 1
    for b in range(bt):
        nk_ref[b] = jnp.where(last, k[b:b + 1], pltpu.roll(ck[b], w - 1, axis=0))
        nv_ref[b] = jnp.where(last, v[b:b + 1], pltpu.roll(cv[b], w - 1, axis=0))

    seg = lax.broadcasted_iota(jnp.int32, (bt, SEG), 1) // HEAD_DIM
    scale = HEAD_DIM ** -0.5
    chunks = [jnp.zeros((bt, SEG), f32) for _ in range(N_KV_HEADS)]
    for g in range(GROUP):
        qg = jnp.zeros((bt, SEG), f32)
        for kvh in range(N_KV_HEADS):
            x = q[:, SEG * kvh:SEG * (kvh + 1)]
            sh = (HEAD_DIM * (kvh - g)) % SEG
            if sh:
                x = pltpu.roll(x, sh, axis=1)
            qg = jnp.where(seg == kvh, x, qg)
        prod = (ck * qg[:, None, :]).reshape(bt * w, SEG)
        sc = (_segsum(prod, bd) * scale).reshape(bt, w, SEG)
        sn = _segsum(qg * k, bd) * scale
        sink = sink_ref[g:g + 1, :]
        m = jnp.maximum(jnp.maximum(jnp.max(sc, axis=1), sn), sink)
        p = jnp.exp(sc - m[:, None, :])
        pn = jnp.exp(sn - m)
        denom = jnp.sum(p, axis=1) + pn + jnp.exp(sink - m)
        o = (jnp.sum(p * cv, axis=1) + pn * v) / denom
        ms = _segsum(o * o, bd) * (1.0 / HEAD_DIM)
        on = o * lax.rsqrt(ms + EPS)
        for kvh in range(N_KV_HEADS):
            sh = (HEAD_DIM * (g - kvh)) % SEG
            x = pltpu.roll(on, sh, axis=1) if sh else on
            chunks[kvh] = jnp.where(seg == g, x, chunks[kvh])
    out = jnp.concatenate(chunks, axis=1) * g_ref[...]
    an_ref[...] = out.astype(an_ref.dtype)


def _attn_sample(z, cache_k, cache_v, layer, tab, sinks_exp, g_attn, bd, bt=16):
    w = cache_k.shape[2]
    row0 = SAMPLE_ROW0 // bt
    kcol, vcol = ATTN_DIM // KV_DIM, ATTN_DIM // KV_DIM + 1
    cache_in = pl.BlockSpec((None, bt, w, KV_DIM), lambda i: (layer, i, 0, 0))
    cache_spec = pl.BlockSpec((bt, w, KV_DIM), lambda i: (i, 0, 0))
    return pl.pallas_call(
        _attn_sample_kernel,
        grid=(DEC_BATCH // bt,),
        in_specs=[pl.BlockSpec((bt, ATTN_DIM), lambda i: (row0 + i, 0)),
                  pl.BlockSpec((bt, KV_DIM), lambda i: (row0 + i, kcol)),
                  pl.BlockSpec((bt, KV_DIM), lambda i: (row0 + i, vcol)),
                  cache_in, cache_in,
                  pl.BlockSpec((8, 256), lambda i: (0, 0)),
                  pl.BlockSpec((GROUP, SEG), lambda i: (0, 0)),
                  pl.BlockSpec((1, ATTN_DIM), lambda i: (0, 0)),
                  pl.BlockSpec((SEG, SEG), lambda i: (0, 0))],
        out_specs=[pl.BlockSpec((bt, ATTN_DIM), lambda i: (i, 0)), cache_spec, cache_spec],
        out_shape=[jax.ShapeDtypeStruct((DEC_BATCH, ATTN_DIM), bf16),
                   jax.ShapeDtypeStruct(cache_k.shape[1:], f32),
                   jax.ShapeDtypeStruct(cache_v.shape[1:], f32)],
        compiler_params=_cparams("parallel"),
    )(z, z, z, cache_k, cache_v, tab, sinks_exp, g_attn, bd)


def _conv_kernel(z1_ref, z2_ref, p1_ref, p2_ref, s0_ref, s1_ref, w_ref, g_ref, bd_ref, c_ref, utail_ref, ns_ref):
    i = pl.program_id(0)

    def split(b1, b2):
        gb = b1[:, :CONV_DIM]
        gc = jnp.concatenate([b1[:, CONV_DIM:], b2[:, :CONV_DIM // 2]], axis=1)
        return gb, gc * b2[:, CONV_DIM // 2:]

    gb, u = split(z1_ref[...], z2_ref[...])
    _, up = split(p1_ref[...], p2_ref[...])

    @pl.when(i < SAMPLE_BLOCK)
    def _():
        utail_ref[...] = u[ROW_BLOCK - 8:]

    @pl.when(i == SAMPLE_BLOCK)
    def _():
        ns_ref[:, :CONV_DIM] = s1_ref[...]
        ns_ref[:, CONV_DIM:] = u

    row = lax.broadcasted_iota(jnp.int32, u.shape, 0)
    u1 = jnp.where(row == 0, up[7:8], pltpu.roll(u, 1, axis=0))
    u2 = jnp.where(row == 0, up[6:7], jnp.where(row == 1, up[7:8], pltpu.roll(u, 2, axis=0)))
    is_sample = i == SAMPLE_BLOCK
    u1 = jnp.where(is_sample, s1_ref[...], u1)
    u2 = jnp.where(is_sample, s0_ref[...], u2)
    w = w_ref[...]
    y = u2 * w[0:1] + u1 * w[1:2] + u * w[2:3]
    t = gb * y
    bd = bd_ref[...]
    g = g_ref[...]
    for c in range(CONV_DIM // SEG):
        tc = t[:, SEG * c:SEG * (c + 1)]
        ms = _segsum(tc * tc, bd) * (1.0 / HEAD_DIM)
        c_ref[:, SEG * c:SEG * (c + 1)] = (tc * lax.rsqrt(ms + EPS) * g[:, SEG * c:SEG * (c + 1)]).astype(c_ref.dtype)


def _conv_prev_block(i):
    last8_of_meta = N_ROWS // 8 - 1
    return jnp.where(i % BLOCKS_PER_SEQ == 0, last8_of_meta, (ROW_BLOCK // 8) * i - 1)


def _conv_mix(z, state_l, conv_w_l, g_conv, bd):
    wide = 3 * CONV_DIM // 2
    return pl.pallas_call(
        _conv_kernel,
        grid=(N_ROW_BLOCKS,),
        in_specs=[pl.BlockSpec((ROW_BLOCK, wide), lambda i: (i, 1)),
                  pl.BlockSpec((ROW_BLOCK, wide), lambda i: (i, 2)),
                  pl.BlockSpec((8, wide), lambda i: (_conv_prev_block(i), 1)),
                  pl.BlockSpec((8, wide), lambda i: (_conv_prev_block(i), 2)),
                  pl.BlockSpec((DEC_BATCH, CONV_DIM), lambda i: (0, 0)),
                  pl.BlockSpec((DEC_BATCH, CONV_DIM), lambda i: (0, 1)),
                  pl.BlockSpec((CONV_WIDTH, CONV_DIM), lambda i: (0, 0)),
                  pl.BlockSpec((1, CONV_DIM), lambda i: (0, 0)),
                  pl.BlockSpec((SEG, SEG), lambda i: (0, 0))],
        out_specs=[pl.BlockSpec((ROW_BLOCK, CONV_DIM), lambda i: (i, 0)),
                   pl.BlockSpec((8, CONV_DIM), lambda i: (_seq_of_step(i), 0)),
                   pl.BlockSpec((DEC_BATCH, 2 * CONV_DIM), lambda i: (0, 0))],
        out_shape=[jax.ShapeDtypeStruct((N_ROWS, CONV_DIM), bf16),
                   jax.ShapeDtypeStruct((BATCH * 8, CONV_DIM), f32),
                   jax.ShapeDtypeStruct((DEC_BATCH, 2 * CONV_DIM), f32)],
        compiler_params=_cparams("arbitrary"),
    )(z, z, z, z, state_l, state_l, conv_w_l, g_conv, bd)


def _outproj_kernel(a_ref, as_ref, c_ref, wa_ref, wc_ref, h_ref, g_ref, ho_ref, no_ref):
    tm = a_ref.shape[0]
    off = SAMPLE_ROW0 % tm
    a = a_ref[...]
    pieces = [jnp.zeros((off, ATTN_DIM), a.dtype)] if off else []
    pieces.append(as_ref[...])
    if tm - off - DEC_BATCH:
        pieces.append(jnp.zeros((tm - off - DEC_BATCH, ATTN_DIM), a.dtype))
    row = lax.broadcasted_iota(jnp.int32, a.shape, 0)
    is_decode = (pl.program_id(0) == SAMPLE_ROW0 // tm) & (row >= off) & (row < off + DEC_BATCH)
    a = jnp.where(is_decode, jnp.concatenate(pieces, axis=0), a)
    acc = jnp.dot(a, wa_ref[...], preferred_element_type=f32)
    acc += jnp.dot(c_ref[...], wc_ref[...], preferred_element_type=f32)
    hn = h_ref[...] + acc
    ho_ref[...] = hn
    no_ref[...] = _rms(hn, g_ref[...]).astype(no_ref.dtype)


def _out_proj(an, an_decode, c, w_out_bf, layer, h, g_next, tm=384):
    assert SAMPLE_ROW0 % tm + DEC_BATCH <= tm
    row = lambda i: (i, 0)
    return pl.pallas_call(
        _outproj_kernel,
        grid=(N_ROWS // tm,),
        in_specs=[pl.BlockSpec((tm, ATTN_DIM), row),
                  pl.BlockSpec((DEC_BATCH, ATTN_DIM), lambda i: (0, 0)),
                  pl.BlockSpec((tm, CONV_DIM), row),
                  pl.BlockSpec((None, ATTN_DIM, D_MODEL), lambda i: (layer, 0, 0)),
                  pl.BlockSpec((None, CONV_DIM, D_MODEL), lambda i: (layer, 1, 0)),
                  pl.BlockSpec((tm, D_MODEL), row),
                  pl.BlockSpec((1, D_MODEL), lambda i: (0, 0))],
        out_specs=[pl.BlockSpec((tm, D_MODEL), row), pl.BlockSpec((tm, D_MODEL), row)],
        out_shape=[jax.ShapeDtypeStruct((N_ROWS, D_MODEL), f32),
                   jax.ShapeDtypeStruct((N_ROWS, D_MODEL), bf16)],
        compiler_params=_cparams("parallel"),
    )(an, an_decode, c, w_out_bf, w_out_bf, h, g_next)


def _router_kernel(h_ref, g_ref, r_ref, tri_ref, cmb_ref, dest_ref, carry_ref):
    i = pl.program_id(0)
    tm = h_ref.shape[0]

    @pl.when(i == 0)
    def _():
        carry_ref[...] = jnp.zeros_like(carry_ref)

    n = _rms(h_ref[...], g_ref[...])
    logits = jnp.dot(n, r_ref[...], preferred_element_type=f32, precision=lax.Precision.HIGHEST)
    idx = lax.broadcasted_iota(jnp.int32, logits.shape, 1)
    m1 = jnp.max(logits, axis=-1, keepdims=True)
    i1 = jnp.min(jnp.where(logits == m1, idx, N_EXPERTS), axis=-1, keepdims=True)
    rest = jnp.where(idx == i1, -jnp.inf, logits)
    m2 = jnp.max(rest, axis=-1, keepdims=True)
    i2 = jnp.min(jnp.where(rest == m2, idx, N_EXPERTS), axis=-1, keepdims=True)
    e2 = jnp.exp(m2 - m1)
    den = 1.0 + e2
    rowid = i * tm + lax.broadcasted_iota(jnp.int32, logits.shape, 0)
    real = (rowid < TAIL_ROW0) | (rowid >= N_ROWS - N_META)
    sel = ((idx == i1) | (idx == i2)) & real
    cmb_ref[...] = jnp.where(sel, jnp.where(idx == i1, 1.0 / den, e2 / den), 0.0)
    self = jnp.where(sel, 1.0, 0.0)
    incl = jnp.dot(tri_ref[...], self.astype(bf16), preferred_element_type=f32)
    carry = carry_ref[...]
    dest_ref[...] = jnp.where(sel, incl - self + carry, -1.0).astype(jnp.int32)
    carry_ref[...] = carry + incl[tm - 1:tm, :]


def _router(h, g, router, tm=768):
    tri = jnp.asarray(np.tril(np.ones((tm, tm), np.float32)), dtype=bf16)
    return pl.pallas_call(
        _router_kernel,
        grid=(N_ROWS // tm,),
        in_specs=[pl.BlockSpec((tm, D_MODEL), lambda i: (i, 0)),
                  pl.BlockSpec((1, D_MODEL), lambda i: (0, 0)),
                  pl.BlockSpec((D_MODEL, N_EXPERTS), lambda i: (0, 0)),
                  pl.BlockSpec((tm, tm), lambda i: (0, 0))],
        out_specs=[pl.BlockSpec((tm, N_EXPERTS), lambda i: (i, 0)),
                   pl.BlockSpec((tm, N_EXPERTS), lambda i: (i, 0))],
        out_shape=[jax.ShapeDtypeStruct((N_ROWS, N_EXPERTS), f32),
                   jax.ShapeDtypeStruct((N_ROWS, N_EXPERTS), jnp.int32)],
        scratch_shapes=[pltpu.VMEM((1, N_EXPERTS), f32)],
        compiler_params=_cparams("arbitrary"),
    )(h, g, router, tri)


MOE_CHUNK = 256
MOE_GATHER = 768
MOE_SUB = 768
MOE_VISIT_ROWS = 3 * MOE_SUB
MOE_CAP = 4 * MOE_VISIT_ROWS
MOE_TF = 256
N_CHUNKS = N_ROWS // MOE_CHUNK
N_GATHER = N_ROWS // MOE_GATHER
N_FSTEPS = D_FF // MOE_TF
MOE_MAX_VISITS = (2 * N_ROWS) // MOE_VISIT_ROWS + N_EXPERTS


def _moe_kernel(vis_e, vis_k, vis_nsub, vis_ok, rlo, xn_ref, dest_ref, w1_ref, w3_ref, w2_ref, y_ref, acc_ref):
    v, p = pl.program_id(0), pl.program_id(1)
    ok = vis_ok[v] == 1
    e = vis_e[v]
    base = vis_k[v] * MOE_VISIT_ROWS

    @pl.when(ok & (p == 0))
    def _():
        y_ref[...] = jnp.zeros_like(y_ref)

    @pl.when(ok & (p < N_GATHER))
    def _():
        lo = jnp.maximum(rlo[e * (N_GATHER + 1) + p] - base, 0)
        hi = jnp.minimum(rlo[e * (N_GATHER + 1) + p + 1] - base, MOE_VISIT_ROWS)

        @pl.when(hi > lo)
        def _():
            dest = dest_ref[...] - base
            win = lax.broadcasted_iota(jnp.int32, (MOE_CHUNK, MOE_GATHER), 0)

            def tile_body(t, carry):
                off = pl.multiple_of(t * MOE_CHUNK, MOE_CHUNK)
                onehot = jnp.where(win == dest - off, 1.0, 0.0).astype(bf16)
                rows = jnp.dot(onehot, xn_ref[...], preferred_element_type=f32)
                y_ref[pl.ds(off, MOE_CHUNK), :] += rows.astype(bf16)
                return carry

            lax.fori_loop(lo // MOE_CHUNK, (hi - 1) // MOE_CHUNK + 1, tile_body, 0)

    @pl.when(ok & (p >= N_GATHER))
    def _():
        f = p - N_GATHER
        w1 = w1_ref[...].astype(bf16)
        w3 = w3_ref[...].astype(bf16)
        w2 = w2_ref[...].astype(bf16)
        for r in range(MOE_VISIT_ROWS // MOE_SUB):
            rows = slice(r * MOE_SUB, (r + 1) * MOE_SUB)

            @pl.when(r < vis_nsub[v])
            def _():
                @pl.when(f == 0)
                def _():
                    acc_ref[rows, :] = jnp.zeros((MOE_SUB, D_MODEL), f32)

                x = y_ref[rows, :]
                a = jnp.dot(x, w1, preferred_element_type=f32)
                b = jnp.dot(x, w3, preferred_element_type=f32)
                act = (a * jax.nn.sigmoid(a) * b).astype(bf16)
                acc_ref[rows, :] += jnp.dot(act, w2, preferred_element_type=f32)

                @pl.when(f == N_FSTEPS - 1)
                def _():
                    y_ref[rows, :] = acc_ref[rows, :].astype(y_ref.dtype)


def _moe_experts(xn, dest_exp, tables, n_visits, w1, w3, w2, layer_idx):
    def chunk(v, p, ve, vk, vn, vok, rlo):
        return jnp.where(vok[v] == 1, jnp.minimum(p, N_GATHER - 1), N_GATHER - 1)

    def fstep(v, p, ve, vk, vn, vok, rlo):
        return jnp.where(vok[v] == 1, jnp.maximum(p - N_GATHER, 0), N_FSTEPS - 1)

    grid_spec = pltpu.PrefetchScalarGridSpec(
        num_scalar_prefetch=5,
        grid=(n_visits, N_GATHER + N_FSTEPS),
        in_specs=[pl.BlockSpec((MOE_GATHER, D_MODEL), lambda v, p, *t: (chunk(v, p, *t), 0)),
                  pl.BlockSpec((None, None, 1, MOE_GATHER), lambda v, p, *t: (t[0][v], chunk(v, p, *t), 0, 0)),
                  pl.BlockSpec((None, None, D_MODEL, MOE_TF), lambda v, p, *t: (layer_idx, t[0][v], 0, fstep(v, p, *t))),
                  pl.BlockSpec((None, None, D_MODEL, MOE_TF), lambda v, p, *t: (layer_idx, t[0][v], 0, fstep(v, p, *t))),
                  pl.BlockSpec((None, None, MOE_TF, D_MODEL), lambda v, p, *t: (layer_idx, t[0][v], fstep(v, p, *t), 0))],
        out_specs=pl.BlockSpec((None, MOE_VISIT_ROWS, D_MODEL), lambda v, p, *t: (t[0][v], t[1][v], 0),
                               pipeline_mode=pl.Buffered(1)),
        scratch_shapes=[pltpu.VMEM((MOE_VISIT_ROWS, D_MODEL), f32)],
    )
    return pl.pallas_call(
        _moe_kernel,
        grid_spec=grid_spec,
        out_shape=jax.ShapeDtypeStruct((N_EXPERTS, MOE_CAP, D_MODEL), bf16),
        compiler_params=_cparams("arbitrary", "arbitrary"),
    )(*tables, xn, dest_exp, w1, w3, w2)


def _combine_kernel(t0, nwin, h_ref, dest_ref, cmb_ref, g_ref, *rest):
    ywin = rest[:2 * N_EXPERTS]
    yp_ref, ys_ref, acc_ref = rest[2 * N_EXPERTS:]
    c = pl.program_id(0)
    acc_ref[...] = h_ref[...]
    dest = dest_ref[...]
    cmb = cmb_ref[...]
    lane = lax.broadcasted_iota(jnp.int32, (MOE_CHUNK, MOE_CHUNK), 1)
    for e in range(N_EXPERTS):
        for w in range(2):
            @pl.when(nwin[e * N_CHUNKS + c] > w)
            def _():
                rel = dest[:, e:e + 1] - MOE_CHUNK * (t0[e * N_CHUNKS + c] + w)
                onehot = jnp.where(rel == lane, 1.0, 0.0).astype(bf16)
                got = jnp.dot(onehot, ywin[2 * e + w][...], preferred_element_type=f32)
                acc_ref[...] += cmb[:, e:e + 1] * got

    n = _rms(acc_ref[...], g_ref[...])
    n_prompt_chunks = N_PROMPT_ROWS // MOE_CHUNK

    @pl.when(c < n_prompt_chunks)
    def _():
        yp_ref[...] = n

    @pl.when(c == n_prompt_chunks)
    def _():
        ys_ref[...] = n[:DEC_BATCH]


def _moe_combine(y_sorted, win_tables, h, dest_tok, cmb, g_final):
    t0, t0c, t1c, nwin = win_tables
    row = lambda c, *t: (c, 0)
    n_prompt_chunks = N_PROMPT_ROWS // MOE_CHUNK
    wins = []
    for e in range(N_EXPERTS):
        wins.append(pl.BlockSpec((None, MOE_CHUNK, D_MODEL), lambda c, a, b, *t, e=e: (e, a[e * N_CHUNKS + c], 0)))
        wins.append(pl.BlockSpec((None, MOE_CHUNK, D_MODEL), lambda c, a, b, *t, e=e: (e, b[e * N_CHUNKS + c], 0)))
    grid_spec = pltpu.PrefetchScalarGridSpec(
        num_scalar_prefetch=4,
        grid=(N_CHUNKS,),
        in_specs=[pl.BlockSpec((MOE_CHUNK, D_MODEL), row),
                  pl.BlockSpec((MOE_CHUNK, N_EXPERTS), row),
                  pl.BlockSpec((MOE_CHUNK, N_EXPERTS), row),
                  pl.BlockSpec((1, D_MODEL), lambda c, *t: (0, 0))] + wins,
        out_specs=[pl.BlockSpec((MOE_CHUNK, D_MODEL), lambda c, *t: (jnp.minimum(c, n_prompt_chunks - 1), 0)),
                   pl.BlockSpec((DEC_BATCH, D_MODEL), lambda c, *t: (0, 0))],
        scratch_shapes=[pltpu.VMEM((MOE_CHUNK, D_MODEL), f32)],
    )

    def body(t0c_ref, t1c_ref, t0_ref, nwin_ref, *refs):
        _combine_kernel(t0_ref, nwin_ref, *refs)

    return pl.pallas_call(
        body,
        grid_spec=grid_spec,
        out_shape=[jax.ShapeDtypeStruct((N_PROMPT_ROWS, D_MODEL), f32),
                   jax.ShapeDtypeStruct((DEC_BATCH, D_MODEL), f32)],
        compiler_params=_cparams("arbitrary"),
    )(t0c, t1c, t0, nwin, h, dest_tok, cmb, g_final, *([y_sorted] * (2 * N_EXPERTS)))


def _routing_tables(dest_tok):
    i32 = jnp.int32
    cnt = (dest_tok >= 0).astype(i32).reshape(N_CHUNKS, MOE_CHUNK, N_EXPERTS).sum(axis=1)
    csum = jnp.cumsum(cnt, axis=0)
    rank_lo = jnp.concatenate([jnp.zeros((1, N_EXPERTS), i32), csum], axis=0).T
    n_e = csum[-1]
    nv = (n_e + MOE_VISIT_ROWS - 1) // MOE_VISIT_ROWS
    cum_nv = jnp.cumsum(nv)
    total = cum_nv[-1]
    v = jnp.arange(MOE_MAX_VISITS, dtype=i32)
    vv = jnp.minimum(v, jnp.maximum(total - 1, 0))
    vis_e = jnp.minimum(jnp.sum((vv[:, None] >= cum_nv[None, :]).astype(i32), axis=1), N_EXPERTS - 1)
    vis_k = vv - (cum_nv - nv)[vis_e]
    vis_ok = (v < total).astype(i32)
    rows = jnp.clip(n_e[vis_e] - vis_k * MOE_VISIT_ROWS, 0, MOE_VISIT_ROWS)
    vis_nsub = jnp.where(vis_ok == 1, (rows + MOE_SUB - 1) // MOE_SUB, 0)
    t0 = rank_lo[:, :N_CHUNKS] // MOE_CHUNK
    tmax = jnp.maximum(nv * (MOE_VISIT_ROWS // MOE_CHUNK) - 1, 0)[:, None]
    moe_tables = (vis_e, vis_k, vis_nsub.astype(i32), vis_ok, rank_lo[:, ::MOE_GATHER // MOE_CHUNK].reshape(-1))
    t_last = (rank_lo[:, 1:] - 1) // MOE_CHUNK
    nwin = jnp.where(cnt.T > 0, t_last - t0 + 1, 0)
    win_tables = (t0.reshape(-1), jnp.minimum(t0, tmax).reshape(-1), jnp.minimum(t0 + 1, tmax).reshape(-1),
                  nwin.reshape(-1))
    return moe_tables, win_tables, total


def _ffn_kernel(x_ref, w1_ref, w3_ref, w2_ref, h_ref, g_ref, ho_ref, no_ref):
    f = pl.program_id(1)

    @pl.when(f == 0)
    def _():
        ho_ref[...] = h_ref[...]

    x = x_ref[...]
    a = jnp.dot(x, w1_ref[...].astype(bf16), preferred_element_type=f32)
    b = jnp.dot(x, w3_ref[...].astype(bf16), preferred_element_type=f32)
    act = (a * jax.nn.sigmoid(a) * b).astype(bf16)
    ho_ref[...] += jnp.dot(act, w2_ref[...].astype(bf16), preferred_element_type=f32)

    @pl.when(f == pl.num_programs(1) - 1)
    def _():
        no_ref[...] = _rms(ho_ref[...], g_ref[...]).astype(no_ref.dtype)


def _ffn(xn, w1, w3, w2, layer_idx, h, g_next, tm=768, tf=512):
    row = lambda i, f: (i, 0)
    once = pl.Buffered(1)
    return pl.pallas_call(
        _ffn_kernel,
        grid=(N_ROWS // tm, D_FF // tf),
        in_specs=[pl.BlockSpec((tm, D_MODEL), row),
                  pl.BlockSpec((None, D_MODEL, tf), lambda i, f: (layer_idx, 0, f)),
                  pl.BlockSpec((None, D_MODEL, tf), lambda i, f: (layer_idx, 0, f)),
                  pl.BlockSpec((None, tf, D_MODEL), lambda i, f: (layer_idx, f, 0)),
                  pl.BlockSpec((tm, D_MODEL), row, pipeline_mode=once),
                  pl.BlockSpec((1, D_MODEL), lambda i, f: (0, 0))],
        out_specs=[pl.BlockSpec((tm, D_MODEL), row, pipeline_mode=once),
                   pl.BlockSpec((tm, D_MODEL), row, pipeline_mode=once)],
        out_shape=[jax.ShapeDtypeStruct((N_ROWS, D_MODEL), f32),
                   jax.ShapeDtypeStruct((N_ROWS, D_MODEL), bf16)],
        compiler_params=_cparams("parallel", "arbitrary"),
    )(xn, w1, w3, w2, h, g_next)


def _rope_patterns(pos):
    half = ROT_DIM // 2
    inv_freq = jnp.power(ROPE_THETA, -jnp.arange(half, dtype=f32) * 2.0 / ROT_DIM)
    ang = pos.astype(f32)[:, None] * inv_freq[None, :]
    cos, sin = jnp.cos(ang), jnp.sin(ang)
    n = pos.shape[0]
    cos_h = jnp.concatenate([cos, cos, jnp.ones((n, HEAD_DIM - ROT_DIM), f32)], axis=1)
    sin_h = jnp.concatenate([-sin, sin, jnp.zeros((n, HEAD_DIM - ROT_DIM), f32)], axis=1)
    return jnp.concatenate([cos_h, cos_h, sin_h, sin_h], axis=1)


def _block_diag_ones():
    i = np.arange(SEG) // HEAD_DIM
    return jnp.asarray((i[:, None] == i[None, :]).astype(np.float32), dtype=bf16)


def kernel(x_prompt, x_sample, cache_k, cache_v, state_conv, meta_tokens, g_mix, w_in, conv_w, attn_sinks,
           g_attn_out, g_conv_out, w_out, g_ffn, dense_w1, dense_w3, dense_w2, moe_router, moe_w1, moe_w3,
           moe_w2, g_final):
    w_cache = cache_k.shape[2]
    tail = jnp.concatenate([x_sample.reshape(DEC_BATCH, D_MODEL), jnp.zeros((META_ROW_IN_BLOCK, D_MODEL), f32),
                            meta_tokens.astype(f32)], axis=0)

    prompt_pos = N_META + jnp.arange(SEQ, dtype=jnp.int32)
    tail_pos = jnp.maximum(jnp.arange(ROW_BLOCK, dtype=jnp.int32) - META_ROW_IN_BLOCK, 0)
    rope_tab = _rope_patterns(jnp.concatenate([prompt_pos, tail_pos]))
    sample_tab = _rope_patterns(jnp.full((8,), PAST_LEN, jnp.int32))
    bd = _block_diag_ones()
    w_out_bf = w_out.astype(bf16)
    cache_k = cache_k.reshape(DEPTH, DEC_BATCH, w_cache, KV_DIM)
    cache_v = cache_v.reshape(DEPTH, DEC_BATCH, w_cache, KV_DIM)

    h, xn = _norm_rows(x_prompt.reshape(N_PROMPT_ROWS, D_MODEL), tail, g_mix[0:1])
    nk_p, nv_p, nc_p, nk_s, nv_s, nc_s = [], [], [], [], [], []
    for l in range(DEPTH):
        z = _in_proj(xn, w_in, l)
        g_attn = g_attn_out[l:l + 1]
        an, nk_prompt, nv_prompt = _attn_prompt(z, attn_sinks[l], rope_tab, g_attn)
        sinks_exp = jnp.repeat(attn_sinks[l].reshape(N_KV_HEADS, GROUP).T, HEAD_DIM, axis=1)
        an_decode, nk, nv = _attn_sample(z, cache_k, cache_v, l, sample_tab, sinks_exp, g_attn, bd)
        state_l = state_conv[l].reshape(DEC_BATCH, (CONV_WIDTH - 1) * CONV_DIM)
        c, u_tail, conv_state = _conv_mix(z, state_l, conv_w[l], g_conv_out[l:l + 1], bd)
        h, xn = _out_proj(an, an_decode, c, w_out_bf, l, h, g_ffn[l:l + 1])

        j = l // 2
        if l % 2 == 0:
            h, xn = _ffn(xn, dense_w1, dense_w3, dense_w2, j, h, g_mix[l + 1:l + 2])
        else:
            cmb, dest_tok = _router(h, g_ffn[l:l + 1], moe_router[j])
            moe_tables, win_tables, n_visits = _routing_tables(dest_tok)
            dest_exp = dest_tok.T.reshape(N_EXPERTS, N_GATHER, 1, MOE_GATHER)
            y_sorted = _moe_experts(xn, dest_exp, moe_tables, n_visits, moe_w1, moe_w3, moe_w2, j)
            y_prompt, y_sample = _moe_combine(y_sorted, win_tables, h, dest_tok, cmb, g_final[None])

        nk_p.append(nk_prompt[:BATCH * WINDOW].reshape(BATCH, WINDOW, N_KV_HEADS, HEAD_DIM))
        nv_p.append(nv_prompt[:BATCH * WINDOW].reshape(BATCH, WINDOW, N_KV_HEADS, HEAD_DIM))
        nc_p.append(u_tail.reshape(BATCH, 8, CONV_DIM)[:, 8 - (CONV_WIDTH - 1):])
        nk_s.append(nk.reshape(DEC_BATCH, w_cache, N_KV_HEADS, HEAD_DIM))
        nv_s.append(nv.reshape(DEC_BATCH, w_cache, N_KV_HEADS, HEAD_DIM))
        nc_s.append(conv_state.reshape(DEC_BATCH, CONV_WIDTH - 1, CONV_DIM))

    return (y_prompt.reshape(BATCH, SEQ, D_MODEL), y_sample.reshape(DEC_BATCH, 1, D_MODEL),
            jnp.stack(nk_p), jnp.stack(nv_p), jnp.stack(nc_p), jnp.stack(nk_s), jnp.stack(nv_s), jnp.stack(nc_s))
```

```python
import functools

import jax
import jax.numpy as jnp
import numpy as np
from jax import lax
from jax.experimental import pallas as pl
from jax.experimental.pallas import tpu as pltpu

D_MODEL = 2048
BATCH = 4
SEQ = 2048
DEPTH = 2
DEC_BATCH = 128
PAST_LEN = 8192
HEAD_DIM = 64
ATTN_DIM = D_MODEL // 2
N_Q_HEADS = ATTN_DIM // HEAD_DIM
N_KV_HEADS = N_Q_HEADS // 4
GROUP = N_Q_HEADS // N_KV_HEADS
KV_DIM = N_KV_HEADS * HEAD_DIM
CONV_DIM = D_MODEL - ATTN_DIM
CONV_WIDTH = 3
IN_DIM = ATTN_DIM + 2 * KV_DIM + 3 * CONV_DIM
WINDOW = 128
ROPE_THETA = 500000.0
ROT_DIM = HEAD_DIM // 4
N_META = 16
D_FF = 7 * D_MODEL // 2
N_EXPERTS = 8
EPS = 1e-5

ROW_BLOCK = 128
N_PROMPT_ROWS = BATCH * SEQ
SAMPLE_ROW0 = N_PROMPT_ROWS
TAIL_ROW0 = SAMPLE_ROW0 + DEC_BATCH
N_ROWS = TAIL_ROW0 + ROW_BLOCK
N_ROW_BLOCKS = N_ROWS // ROW_BLOCK
BLOCKS_PER_SEQ = SEQ // ROW_BLOCK
SAMPLE_BLOCK = SAMPLE_ROW0 // ROW_BLOCK
META_BLOCK = TAIL_ROW0 // ROW_BLOCK
META_ROW_IN_BLOCK = ROW_BLOCK - N_META
SEG = 256

VMEM_LIMIT_V7X = 56 * 1024 * 1024

bf16 = jnp.bfloat16
f32 = jnp.float32


def _cparams(*sem):
    return pltpu.CompilerParams(dimension_semantics=sem, vmem_limit_bytes=VMEM_LIMIT_V7X)


def _rms(x, g):
    ms = jnp.mean(x * x, axis=-1, keepdims=True)
    return (x * lax.rsqrt(ms + EPS)) * g


def _segsum(x, bd):
    x1 = x.astype(bf16)
    r1 = x - x1.astype(f32)
    x2 = r1.astype(bf16)
    x3 = (r1 - x2.astype(f32)).astype(bf16)
    d = functools.partial(jnp.dot, preferred_element_type=f32)
    return d(x1, bd) + d(x2, bd) + d(x3, bd)


def _rope(x, cos_t, sin_t):
    n = x.shape[-1]
    lane = lax.broadcasted_iota(jnp.int32, x.shape, x.ndim - 1) % HEAD_DIM
    nxt = pltpu.roll(x, n - ROT_DIM // 2, axis=x.ndim - 1)
    prv = pltpu.roll(x, ROT_DIM // 2, axis=x.ndim - 1)
    return x * cos_t + jnp.where(lane < ROT_DIM // 2, nxt, prv) * sin_t


def _norm_kernel(xp_ref, tail_ref, g_ref, h_ref, o_ref):
    is_tail = pl.program_id(0) == pl.num_programs(0) - 1
    x = jnp.where(is_tail, tail_ref[...], xp_ref[...])
    h_ref[...] = x
    o_ref[...] = _rms(x, g_ref[...]).astype(o_ref.dtype)


def _norm_rows(x_prompt_rows, tail_rows, g):
    tm = tail_rows.shape[0]
    n_prompt_tiles = N_PROMPT_ROWS // tm
    row = lambda i: (i, 0)
    return pl.pallas_call(
        _norm_kernel,
        grid=(N_ROWS // tm,),
        in_specs=[pl.BlockSpec((tm, D_MODEL), lambda i: (jnp.minimum(i, n_prompt_tiles - 1), 0)),
                  pl.BlockSpec((tm, D_MODEL), lambda i: (0, 0)),
                  pl.BlockSpec((1, D_MODEL), lambda i: (0, 0))],
        out_specs=[pl.BlockSpec((tm, D_MODEL), row), pl.BlockSpec((tm, D_MODEL), row)],
        out_shape=[jax.ShapeDtypeStruct((N_ROWS, D_MODEL), f32),
                   jax.ShapeDtypeStruct((N_ROWS, D_MODEL), bf16)],
        compiler_params=_cparams("parallel"),
    )(x_prompt_rows, tail_rows, g)


def _inproj_kernel(x_ref, w_ref, o_ref, wbf_ref):
    @pl.when(pl.program_id(1) == 0)
    def _():
        wbf_ref[...] = w_ref[...].astype(bf16)

    o_ref[...] = jnp.dot(x_ref[...], wbf_ref[...], preferred_element_type=f32)


def _in_proj(xn, w_in, layer, tm=768, tn=768):
    return pl.pallas_call(
        _inproj_kernel,
        grid=(IN_DIM // tn, N_ROWS // tm),
        in_specs=[pl.BlockSpec((tm, D_MODEL), lambda j, i: (i, 0)),
                  pl.BlockSpec((None, D_MODEL, tn), lambda j, i: (layer, 0, j))],
        out_specs=pl.BlockSpec((tm, tn), lambda j, i: (i, j)),
        out_shape=jax.ShapeDtypeStruct((N_ROWS, IN_DIM), f32),
        scratch_shapes=[pltpu.VMEM((D_MODEL, tn), bf16)],
        compiler_params=_cparams("parallel", "arbitrary"),
    )(xn, w_in)


def _attn_prompt_kernel(sink_ref, q_ref, kc_ref, vc_ref, kp_ref, vp_ref, tc_ref, tp_ref, g_ref,
                        an_ref, nk_ref, nv_ref):
    s = pl.program_id(0)
    is_meta = s == N_ROW_BLOCKS - 2
    j = s % BLOCKS_PER_SEQ
    tc = tc_ref[...]
    tp = tp_ref[...]
    q = _rope(q_ref[...], jnp.tile(tc[:, :128], (1, 8)), jnp.tile(tc[:, 128:], (1, 8)))
    kc = _rope(kc_ref[...], jnp.tile(tc[:, :128], (1, 2)), jnp.tile(tc[:, 128:], (1, 2)))
    kp = _rope(kp_ref[...], jnp.tile(tp[:, :128], (1, 2)), jnp.tile(tp[:, 128:], (1, 2)))
    nk_ref[...] = kc
    nv_ref[...] = vc_ref[...]

    kall = jnp.concatenate([kp, kc], axis=0).astype(bf16)
    vall = jnp.concatenate([vp_ref[...], vc_ref[...]], axis=0).astype(bf16)

    row = lax.broadcasted_iota(jnp.int32, (GROUP * ROW_BLOCK, 2 * ROW_BLOCK), 0) % ROW_BLOCK
    col = lax.broadcasted_iota(jnp.int32, (GROUP * ROW_BLOCK, 2 * ROW_BLOCK), 1)
    lo_prev = jnp.where(is_meta, ROW_BLOCK, jnp.where(j == 0, META_ROW_IN_BLOCK, 0))
    lo_cur = jnp.where(is_meta, META_ROW_IN_BLOCK, 0)
    ccur = col - ROW_BLOCK
    valid4 = (((col < ROW_BLOCK) & (col >= row) & (col >= lo_prev))
              | ((ccur >= 0) & (ccur <= row) & (ccur >= lo_cur)))
    half = lax.broadcasted_iota(jnp.int32, (ROW_BLOCK, 128), 1) // HEAD_DIM
    half4 = lax.broadcasted_iota(jnp.int32, (GROUP * ROW_BLOCK, 128), 1) // HEAD_DIM
    g_all = g_ref[...]

    for kvh in range(N_KV_HEADS):
        kside = kvh % 2
        kpair = kall[:, 128 * (kvh // 2):128 * (kvh // 2) + 128]
        vpair = vall[:, 128 * (kvh // 2):128 * (kvh // 2) + 128]
        qs, sinks = [], []
        for g in range(GROUP):
            h = GROUP * kvh + g
            x = q[:, 128 * (h // 2):128 * (h // 2) + 128]
            if h % 2 != kside:
                x = pltpu.roll(x, HEAD_DIM, axis=1)
            qs.append(jnp.where(half == kside, x, 0.0))
            sinks.append(jnp.full((ROW_BLOCK, 1), sink_ref[h], f32))
        q4 = jnp.concatenate(qs, axis=0).astype(bf16)
        sink = jnp.concatenate(sinks, axis=0)
        sc = lax.dot_general(q4, kpair, (((1,), (1,)), ((), ())), preferred_element_type=f32)
        sc = jnp.where(valid4, sc * (HEAD_DIM ** -0.5), -jnp.inf)
        m = jnp.maximum(jnp.max(sc, axis=-1, keepdims=True), sink)
        p = jnp.exp(sc - m)
        denom = jnp.sum(p, axis=-1, keepdims=True) + jnp.exp(sink - m)
        o = jnp.dot(p.astype(bf16), vpair, preferred_element_type=f32) / denom
        ms = jnp.sum(jnp.where(half4 == kside, o * o, 0.0), axis=-1, keepdims=True) * (1.0 / HEAD_DIM)
        on = o * lax.rsqrt(ms + EPS)
        for pair in range(2):
            parts = []
            for side in range(2):
                g = 2 * pair + side
                x = on[ROW_BLOCK * g:ROW_BLOCK * (g + 1)]
                if side != kside:
                    x = pltpu.roll(x, HEAD_DIM, axis=1)
                parts.append(x)
            blk = 2 * kvh + pair
            out = jnp.where(half == 0, parts[0], parts[1]) * g_all[:, 128 * blk:128 * blk + 128]
            an_ref[:, 128 * blk:128 * blk + 128] = out.astype(an_ref.dtype)


def _cur_block(s):
    return jnp.where(s == N_ROW_BLOCKS - 2, META_BLOCK, jnp.where(s == N_ROW_BLOCKS - 1, SAMPLE_BLOCK, s))


def _seq_of_step(s):
    return jnp.minimum(s // BLOCKS_PER_SEQ, BATCH - 1)


def _prev_block(s):
    return jnp.where((s % BLOCKS_PER_SEQ == 0) | (s == N_ROW_BLOCKS - 2), META_BLOCK, s - 1)


def _cur_tab(s):
    return jnp.where(s == N_ROW_BLOCKS - 2, BLOCKS_PER_SEQ, s % BLOCKS_PER_SEQ)


def _prev_tab(s):
    return jnp.where((s % BLOCKS_PER_SEQ == 0) | (s == N_ROW_BLOCKS - 2), BLOCKS_PER_SEQ, s % BLOCKS_PER_SEQ - 1)


def _attn_prompt(z, sinks, rope_tab, g_attn):
    kcol, vcol = ATTN_DIM // KV_DIM, ATTN_DIM // KV_DIM + 1
    return pl.pallas_call(
        _attn_prompt_kernel,
        grid=(N_ROW_BLOCKS,),
        in_specs=[pl.BlockSpec(memory_space=pltpu.SMEM),
                  pl.BlockSpec((ROW_BLOCK, ATTN_DIM), lambda s: (_cur_block(s), 0)),
                  pl.BlockSpec((ROW_BLOCK, KV_DIM), lambda s: (_cur_block(s), kcol)),
                  pl.BlockSpec((ROW_BLOCK, KV_DIM), lambda s: (_cur_block(s), vcol)),
                  pl.BlockSpec((ROW_BLOCK, KV_DIM), lambda s: (_prev_block(s), kcol)),
                  pl.BlockSpec((ROW_BLOCK, KV_DIM), lambda s: (_prev_block(s), vcol)),
                  pl.BlockSpec((ROW_BLOCK, 256), lambda s: (_cur_tab(s), 0)),
                  pl.BlockSpec((ROW_BLOCK, 256), lambda s: (_prev_tab(s), 0)),
                  pl.BlockSpec((1, ATTN_DIM), lambda s: (0, 0))],
        out_specs=[pl.BlockSpec((ROW_BLOCK, ATTN_DIM), lambda s: (_cur_block(s), 0)),
                   pl.BlockSpec((WINDOW, KV_DIM), lambda s: (s // BLOCKS_PER_SEQ, 0)),
                   pl.BlockSpec((WINDOW, KV_DIM), lambda s: (s // BLOCKS_PER_SEQ, 0))],
        out_shape=[jax.ShapeDtypeStruct((N_ROWS, ATTN_DIM), bf16),
                   jax.ShapeDtypeStruct(((BATCH + 1) * WINDOW, KV_DIM), f32),
                   jax.ShapeDtypeStruct(((BATCH + 1) * WINDOW, KV_DIM), f32)],
        compiler_params=_cparams("arbitrary"),
    )(sinks, z, z, z, z, z, rope_tab, rope_tab, g_attn)


def _attn_sample_kernel(q_ref, k_ref, v_ref, ck_ref, cv_ref, tab_ref, sink_ref, g_ref, bd_ref,
                        an_ref, nk_ref, nv_ref):
    bt = q_ref.shape[0]
    w = ck_ref.shape[1]
    tab = tab_ref[...]
    q = _rope(q_ref[...], jnp.tile(tab[0:1, :128], (1, 8)), jnp.tile(tab[0:1, 128:], (1, 8)))
    k = _rope(k_ref[...], jnp.tile(tab[0:1, :128], (1, 2)), jnp.tile(tab[0:1, 128:], (1, 2)))
    v = v_ref[...]
    ck = ck_ref[...]
    cv = cv_ref[...]
    bd = bd_ref[...]

    last = lax.broadcasted_iota(jnp.int32, (w, KV_DIM), 0) == w - 1
    for b in range(bt):
        nk_ref[b] = jnp.where(last, k[b:b + 1], pltpu.roll(ck[b], w - 1, axis=0))
        nv_ref[b] = jnp.where(last, v[b:b + 1], pltpu.roll(cv[b], w - 1, axis=0))

    seg = lax.broadcasted_iota(jnp.int32, (bt, SEG), 1) // HEAD_DIM
    scale = HEAD_DIM ** -0.5
    chunks = [jnp.zeros((bt, SEG), f32) for _ in range(N_KV_HEADS)]
    for g in range(GROUP):
        qg = jnp.zeros((bt, SEG), f32)
        for kvh in range(N_KV_HEADS):
            x = q[:, SEG * kvh:SEG * (kvh + 1)]
            sh = (HEAD_DIM * (kvh - g)) % SEG
            if sh:
                x = pltpu.roll(x, sh, axis=1)
            qg = jnp.where(seg == kvh, x, qg)
        prod = (ck * qg[:, None, :]).reshape(bt * w, SEG)
        sc = (_segsum(prod, bd) * scale).reshape(bt, w, SEG)
        sn = _segsum(qg * k, bd) * scale
        sink = sink_ref[g:g + 1, :]
        m = jnp.maximum(jnp.maximum(jnp.max(sc, axis=1), sn), sink)
        p = jnp.exp(sc - m[:, None, :])
        pn = jnp.exp(sn - m)
        denom = jnp.sum(p, axis=1) + pn + jnp.exp(sink - m)
        o = (jnp.sum(p * cv, axis=1) + pn * v) / denom
        ms = _segsum(o * o, bd) * (1.0 / HEAD_DIM)
        on = o * lax.rsqrt(ms + EPS)
        for kvh in range(N_KV_HEADS):
            sh = (HEAD_DIM * (g - kvh)) % SEG
            x = pltpu.roll(on, sh, axis=1) if sh else on
            chunks[kvh] = jnp.where(seg == g, x, chunks[kvh])
    out = jnp.concatenate(chunks, axis=1) * g_ref[...]
    an_ref[...] = out.astype(an_ref.dtype)


def _attn_sample(z, cache_k, cache_v, layer, tab, sinks_exp, g_attn, bd, bt=16):
    w = cache_k.shape[2]
    row0 = SAMPLE_ROW0 // bt
    kcol, vcol = ATTN_DIM // KV_DIM, ATTN_DIM // KV_DIM + 1
    cache_in = pl.BlockSpec((None, bt, w, KV_DIM), lambda i: (layer, i, 0, 0))
    cache_spec = pl.BlockSpec((bt, w, KV_DIM), lambda i: (i, 0, 0))
    return pl.pallas_call(
        _attn_sample_kernel,
        grid=(DEC_BATCH // bt,),
        in_specs=[pl.BlockSpec((bt, ATTN_DIM), lambda i: (row0 + i, 0)),
                  pl.BlockSpec((bt, KV_DIM), lambda i: (row0 + i, kcol)),
                  pl.BlockSpec((bt, KV_DIM), lambda i: (row0 + i, vcol)),
                  cache_in, cache_in,
                  pl.BlockSpec((8, 256), lambda i: (0, 0)),
                  pl.BlockSpec((GROUP, SEG), lambda i: (0, 0)),
                  pl.BlockSpec((1, ATTN_DIM), lambda i: (0, 0)),
                  pl.BlockSpec((SEG, SEG), lambda i: (0, 0))],
        out_specs=[pl.BlockSpec((bt, ATTN_DIM), lambda i: (i, 0)), cache_spec, cache_spec],
        out_shape=[jax.ShapeDtypeStruct((DEC_BATCH, ATTN_DIM), bf16),
                   jax.ShapeDtypeStruct(cache_k.shape[1:], f32),
                   jax.ShapeDtypeStruct(cache_v.shape[1:], f32)],
        compiler_params=_cparams("parallel"),
    )(z, z, z, cache_k, cache_v, tab, sinks_exp, g_attn, bd)


def _conv_kernel(z1_ref, z2_ref, p1_ref, p2_ref, s0_ref, s1_ref, w_ref, g_ref, bd_ref, c_ref, utail_ref, ns_ref):
    i = pl.program_id(0)

    def split(b1, b2):
        gb = b1[:, :CONV_DIM]
        gc = jnp.concatenate([b1[:, CONV_DIM:], b2[:, :CONV_DIM // 2]], axis=1)
        return gb, gc * b2[:, CONV_DIM // 2:]

    gb, u = split(z1_ref[...], z2_ref[...])
    _, up = split(p1_ref[...], p2_ref[...])

    @pl.when(i < SAMPLE_BLOCK)
    def _():
        utail_ref[...] = u[ROW_BLOCK - 8:]

    @pl.when(i == SAMPLE_BLOCK)
    def _():
        ns_ref[:, :CONV_DIM] = s1_ref[...]
        ns_ref[:, CONV_DIM:] = u

    row = lax.broadcasted_iota(jnp.int32, u.shape, 0)
    u1 = jnp.where(row == 0, up[7:8], pltpu.roll(u, 1, axis=0))
    u2 = jnp.where(row == 0, up[6:7], jnp.where(row == 1, up[7:8], pltpu.roll(u, 2, axis=0)))
    is_sample = i == SAMPLE_BLOCK
    u1 = jnp.where(is_sample, s1_ref[...], u1)
    u2 = jnp.where(is_sample, s0_ref[...], u2)
    w = w_ref[...]
    y = u2 * w[0:1] + u1 * w[1:2] + u * w[2:3]
    t = gb * y
    bd = bd_ref[...]
    g = g_ref[...]
    for c in range(CONV_DIM // SEG):
        tc = t[:, SEG * c:SEG * (c + 1)]
        ms = _segsum(tc * tc, bd) * (1.0 / HEAD_DIM)
        c_ref[:, SEG * c:SEG * (c + 1)] = (tc * lax.rsqrt(ms + EPS) * g[:, SEG * c:SEG * (c + 1)]).astype(c_ref.dtype)


def _conv_prev_block(i):
    last8_of_meta = N_ROWS // 8 - 1
    return jnp.where(i % BLOCKS_PER_SEQ == 0, last8_of_meta, (ROW_BLOCK // 8) * i - 1)


def _conv_mix(z, state_l, conv_w_l, g_conv, bd):
    wide = 3 * CONV_DIM // 2
    return pl.pallas_call(
        _conv_kernel,
        grid=(N_ROW_BLOCKS,),
        in_specs=[pl.BlockSpec((ROW_BLOCK, wide), lambda i: (i, 1)),
                  pl.BlockSpec((ROW_BLOCK, wide), lambda i: (i, 2)),
                  pl.BlockSpec((8, wide), lambda i: (_conv_prev_block(i), 1)),
                  pl.BlockSpec((8, wide), lambda i: (_conv_prev_block(i), 2)),
                  pl.BlockSpec((DEC_BATCH, CONV_DIM), lambda i: (0, 0)),
                  pl.BlockSpec((DEC_BATCH, CONV_DIM), lambda i: (0, 1)),
                  pl.BlockSpec((CONV_WIDTH, CONV_DIM), lambda i: (0, 0)),
                  pl.BlockSpec((1, CONV_DIM), lambda i: (0, 0)),
                  pl.BlockSpec((SEG, SEG), lambda i: (0, 0))],
        out_specs=[pl.BlockSpec((ROW_BLOCK, CONV_DIM), lambda i: (i, 0)),
                   pl.BlockSpec((8, CONV_DIM), lambda i: (_seq_of_step(i), 0)),
                   pl.BlockSpec((DEC_BATCH, 2 * CONV_DIM), lambda i: (0, 0))],
        out_shape=[jax.ShapeDtypeStruct((N_ROWS, CONV_DIM), bf16),
                   jax.ShapeDtypeStruct((BATCH * 8, CONV_DIM), f32),
                   jax.ShapeDtypeStruct((DEC_BATCH, 2 * CONV_DIM), f32)],
        compiler_params=_cparams("arbitrary"),
    )(z, z, z, z, state_l, state_l, conv_w_l, g_conv, bd)


def _outproj_kernel(a_ref, as_ref, c_ref, w_ref, h_ref, g_ref, ho_ref, no_ref, wbf_ref):
    @pl.when(pl.program_id(0) == 0)
    def _():
        wbf_ref[...] = w_ref[...].astype(bf16)

    tm = a_ref.shape[0]
    off = SAMPLE_ROW0 % tm
    a = a_ref[...]
    pieces = [jnp.zeros((off, ATTN_DIM), a.dtype)] if off else []
    pieces.append(as_ref[...])
    if tm - off - DEC_BATCH:
        pieces.append(jnp.zeros((tm - off - DEC_BATCH, ATTN_DIM), a.dtype))
    row = lax.broadcasted_iota(jnp.int32, a.shape, 0)
    is_decode = (pl.program_id(0) == SAMPLE_ROW0 // tm) & (row >= off) & (row < off + DEC_BATCH)
    a = jnp.where(is_decode, jnp.concatenate(pieces, axis=0), a)
    acc = jnp.dot(a, wbf_ref[:ATTN_DIM, :], preferred_element_type=f32)
    acc += jnp.dot(c_ref[...], wbf_ref[ATTN_DIM:, :], preferred_element_type=f32)
    hn = h_ref[...] + acc
    ho_ref[...] = hn
    no_ref[...] = _rms(hn, g_ref[...]).astype(no_ref.dtype)


def _out_proj(an, an_decode, c, w_out, layer, h, g_next, tm=384):
    assert SAMPLE_ROW0 % tm + DEC_BATCH <= tm
    row = lambda i: (i, 0)
    return pl.pallas_call(
        _outproj_kernel,
        grid=(N_ROWS // tm,),
        in_specs=[pl.BlockSpec((tm, ATTN_DIM), row),
                  pl.BlockSpec((DEC_BATCH, ATTN_DIM), lambda i: (0, 0)),
                  pl.BlockSpec((tm, CONV_DIM), row),
                  pl.BlockSpec((None, D_MODEL, D_MODEL), lambda i: (layer, 0, 0), pipeline_mode=pl.Buffered(1)),
                  pl.BlockSpec((tm, D_MODEL), row),
                  pl.BlockSpec((1, D_MODEL), lambda i: (0, 0))],
        out_specs=[pl.BlockSpec((tm, D_MODEL), row), pl.BlockSpec((tm, D_MODEL), row)],
        out_shape=[jax.ShapeDtypeStruct((N_ROWS, D_MODEL), f32),
                   jax.ShapeDtypeStruct((N_ROWS, D_MODEL), bf16)],
        scratch_shapes=[pltpu.VMEM((D_MODEL, D_MODEL), bf16)],
        compiler_params=_cparams("arbitrary"),
    )(an, an_decode, c, w_out, h, g_next)


def _router_kernel(h_ref, g_ref, r_ref, tri_ref, cmb_ref, dest_ref, carry_ref):
    i = pl.program_id(0)
    tm = h_ref.shape[0]

    @pl.when(i == 0)
    def _():
        carry_ref[...] = jnp.zeros_like(carry_ref)

    n = _rms(h_ref[...], g_ref[...])
    rt = r_ref[...]
    idx = lax.broadcasted_iota(jnp.int32, (tm, N_EXPERTS), 1)
    logits = jnp.zeros((tm, N_EXPERTS), f32)
    for e in range(N_EXPERTS):
        prod = n * rt[e:e + 1, :]
        part = prod[:, :128]
        for c in range(1, D_MODEL // 128):
            part = part + prod[:, 128 * c:128 * (c + 1)]
        logits = jnp.where(idx == e, jnp.sum(part, axis=-1, keepdims=True), logits)
    m1 = jnp.max(logits, axis=-1, keepdims=True)
    i1 = jnp.min(jnp.where(logits == m1, idx, N_EXPERTS), axis=-1, keepdims=True)
    rest = jnp.where(idx == i1, -jnp.inf, logits)
    m2 = jnp.max(rest, axis=-1, keepdims=True)
    i2 = jnp.min(jnp.where(rest == m2, idx, N_EXPERTS), axis=-1, keepdims=True)
    e2 = jnp.exp(m2 - m1)
    den = 1.0 + e2
    rowid = i * tm + lax.broadcasted_iota(jnp.int32, logits.shape, 0)
    real = (rowid < TAIL_ROW0) | (rowid >= N_ROWS - N_META)
    sel = ((idx == i1) | (idx == i2)) & real
    cmb_ref[...] = jnp.where(sel, jnp.where(idx == i1, 1.0 / den, e2 / den), 0.0)
    self = jnp.where(sel, 1.0, 0.0)
    incl = jnp.dot(tri_ref[...], self.astype(bf16), preferred_element_type=f32)
    carry = carry_ref[...]
    dest_ref[...] = jnp.where(sel, incl - self + carry, -1.0).astype(jnp.int32)
    carry_ref[...] = carry + incl[tm - 1:tm, :]


def _router(h, g, router, tm=768):
    tri = jnp.asarray(np.tril(np.ones((tm, tm), np.float32)), dtype=bf16)
    return pl.pallas_call(
        _router_kernel,
        grid=(N_ROWS // tm,),
        in_specs=[pl.BlockSpec((tm, D_MODEL), lambda i: (i, 0)),
                  pl.BlockSpec((1, D_MODEL), lambda i: (0, 0)),
                  pl.BlockSpec((N_EXPERTS, D_MODEL), lambda i: (0, 0)),
                  pl.BlockSpec((tm, tm), lambda i: (0, 0))],
        out_specs=[pl.BlockSpec((tm, N_EXPERTS), lambda i: (i, 0)),
                   pl.BlockSpec((tm, N_EXPERTS), lambda i: (i, 0))],
        out_shape=[jax.ShapeDtypeStruct((N_ROWS, N_EXPERTS), f32),
                   jax.ShapeDtypeStruct((N_ROWS, N_EXPERTS), jnp.int32)],
        scratch_shapes=[pltpu.VMEM((1, N_EXPERTS), f32)],
        compiler_params=_cparams("arbitrary"),
    )(h, g, router.T, tri)


MOE_CHUNK = 256
MOE_GATHER = 768
MOE_SUB_SIZES = (640, 704, 736, 768)
MOE_SUBS_PER_VISIT = 3
MOE_VISIT_ROWS = MOE_SUBS_PER_VISIT * MOE_SUB_SIZES[-1]
MOE_CAP = 4 * MOE_VISIT_ROWS
MOE_TF = 256
N_CHUNKS = N_ROWS // MOE_CHUNK
N_GATHER = N_ROWS // MOE_GATHER
N_FSTEPS = D_FF // MOE_TF
MOE_MAX_VISITS = (2 * N_ROWS) // MOE_VISIT_ROWS + N_EXPERTS


def _moe_kernel(vis_e, vis_k, vis_nsub, vis_cls, vis_ok, rlo, xn_ref, dest_ref, w1_ref, w3_ref, w2_ref, y_ref, acc_ref):
    v, p = pl.program_id(0), pl.program_id(1)
    ok = vis_ok[v] == 1
    e = vis_e[v]
    base = vis_k[v] * MOE_VISIT_ROWS

    @pl.when(ok & (p == 0))
    def _():
        y_ref[...] = jnp.zeros_like(y_ref)

    @pl.when(ok & (p < N_GATHER))
    def _():
        lo = jnp.maximum(rlo[e * (N_GATHER + 1) + p] - base, 0)
        hi = jnp.minimum(rlo[e * (N_GATHER + 1) + p + 1] - base, MOE_VISIT_ROWS)

        @pl.when(hi > lo)
        def _():
            dest = dest_ref[...] - base
            win = lax.broadcasted_iota(jnp.int32, (MOE_CHUNK, MOE_GATHER), 0)

            def tile_body(t, carry):
                off = pl.multiple_of(t * MOE_CHUNK, MOE_CHUNK)
                onehot = jnp.where(win == dest - off, 1.0, 0.0).astype(bf16)
                rows = jnp.dot(onehot, xn_ref[...], preferred_element_type=f32)
                y_ref[pl.ds(off, MOE_CHUNK), :] += rows.astype(bf16)
                return carry

            lax.fori_loop(lo // MOE_CHUNK, (hi - 1) // MOE_CHUNK + 1, tile_body, 0)

    @pl.when(ok & (p >= N_GATHER))
    def _():
        f = p - N_GATHER
        w1 = w1_ref[...].astype(bf16)
        w3 = w3_ref[...].astype(bf16)
        w2 = w2_ref[...].astype(bf16)
        for cls, sub in enumerate(MOE_SUB_SIZES):
            @pl.when(vis_cls[v] == cls)
            def _():
                def sub_body(r, carry):
                    rows = pl.ds(pl.multiple_of(r * sub, 16), sub)

                    @pl.when(f == 0)
                    def _():
                        acc_ref[rows, :] = jnp.zeros((sub, D_MODEL), f32)

                    x = y_ref[rows, :]
                    a = jnp.dot(x, w1, preferred_element_type=f32)
                    b = jnp.dot(x, w3, preferred_element_type=f32)
                    act = (a * jax.nn.sigmoid(a) * b).astype(bf16)
                    acc_ref[rows, :] += jnp.dot(act, w2, preferred_element_type=f32)

                    @pl.when(f == N_FSTEPS - 1)
                    def _():
                        y_ref[rows, :] = acc_ref[rows, :].astype(y_ref.dtype)

                    return carry

                lax.fori_loop(0, vis_nsub[v], sub_body, 0)


def _moe_experts(xn, dest_exp, tables, w1, w3, w2, layer_idx):
    def chunk(v, p, ve, vk, vn, vc, vok, rlo):
        return jnp.where(vok[v] == 1, jnp.minimum(p, N_GATHER - 1), N_GATHER - 1)

    def fstep(v, p, ve, vk, vn, vc, vok, rlo):
        return jnp.where(vok[v] == 1, jnp.maximum(p - N_GATHER, 0), N_FSTEPS - 1)

    grid_spec = pltpu.PrefetchScalarGridSpec(
        num_scalar_prefetch=6,
        grid=(MOE_MAX_VISITS, N_GATHER + N_FSTEPS),
        in_specs=[pl.BlockSpec((MOE_GATHER, D_MODEL), lambda v, p, *t: (chunk(v, p, *t), 0)),
                  pl.BlockSpec((None, None, 1, MOE_GATHER), lambda v, p, *t: (t[0][v], chunk(v, p, *t), 0, 0)),
                  pl.BlockSpec((None, None, D_MODEL, MOE_TF), lambda v, p, *t: (layer_idx, t[0][v], 0, fstep(v, p, *t))),
                  pl.BlockSpec((None, None, D_MODEL, MOE_TF), lambda v, p, *t: (layer_idx, t[0][v], 0, fstep(v, p, *t))),
                  pl.BlockSpec((None, None, MOE_TF, D_MODEL), lambda v, p, *t: (layer_idx, t[0][v], fstep(v, p, *t), 0))],
        out_specs=pl.BlockSpec((None, MOE_VISIT_ROWS, D_MODEL), lambda v, p, *t: (t[0][v], t[1][v], 0),
                               pipeline_mode=pl.Buffered(1)),
        scratch_shapes=[pltpu.VMEM((MOE_VISIT_ROWS, D_MODEL), f32)],
    )
    return pl.pallas_call(
        _moe_kernel,
        grid_spec=grid_spec,
        out_shape=jax.ShapeDtypeStruct((N_EXPERTS, MOE_CAP, D_MODEL), bf16),
        compiler_params=_cparams("arbitrary", "arbitrary"),
    )(*tables, xn, dest_exp, w1, w3, w2)


def _combine_kernel(t0, nwin, h_ref, dest_ref, cmb_ref, g_ref, *rest):
    ywin = rest[:2 * N_EXPERTS]
    yp_ref, ys_ref, acc_ref = rest[2 * N_EXPERTS:]
    c = pl.program_id(0)
    acc_ref[...] = h_ref[...]
    dest = dest_ref[...]
    cmb = cmb_ref[...]
    lane = lax.broadcasted_iota(jnp.int32, (MOE_CHUNK, MOE_CHUNK), 1)
    for e in range(N_EXPERTS):
        for w in range(2):
            @pl.when(nwin[e * N_CHUNKS + c] > w)
            def _():
                rel = dest[:, e:e + 1] - MOE_CHUNK * (t0[e * N_CHUNKS + c] + w)
                onehot = jnp.where(rel == lane, 1.0, 0.0).astype(bf16)
                got = jnp.dot(onehot, ywin[2 * e + w][...], preferred_element_type=f32)
                acc_ref[...] += cmb[:, e:e + 1] * got

    n = _rms(acc_ref[...], g_ref[...])
    n_prompt_chunks = N_PROMPT_ROWS // MOE_CHUNK

    @pl.when(c < n_prompt_chunks)
    def _():
        yp_ref[...] = n

    @pl.when(c == n_prompt_chunks)
    def _():
        ys_ref[...] = n[:DEC_BATCH]


def _moe_combine(y_sorted, win_tables, h, dest_tok, cmb, g_final):
    t0, t0c, t1c, nwin = win_tables
    row = lambda c, *t: (c, 0)
    n_prompt_chunks = N_PROMPT_ROWS // MOE_CHUNK
    wins = []
    for e in range(N_EXPERTS):
        wins.append(pl.BlockSpec((None, MOE_CHUNK, D_MODEL), lambda c, a, b, *t, e=e: (e, a[e * N_CHUNKS + c], 0)))
        wins.append(pl.BlockSpec((None, MOE_CHUNK, D_MODEL), lambda c, a, b, *t, e=e: (e, b[e * N_CHUNKS + c], 0)))
    grid_spec = pltpu.PrefetchScalarGridSpec(
        num_scalar_prefetch=4,
        grid=(N_CHUNKS,),
        in_specs=[pl.BlockSpec((MOE_CHUNK, D_MODEL), row),
                  pl.BlockSpec((MOE_CHUNK, N_EXPERTS), row),
                  pl.BlockSpec((MOE_CHUNK, N_EXPERTS), row),
                  pl.BlockSpec((1, D_MODEL), lambda c, *t: (0, 0))] + wins,
        out_specs=[pl.BlockSpec((MOE_CHUNK, D_MODEL), lambda c, *t: (jnp.minimum(c, n_prompt_chunks - 1), 0)),
                   pl.BlockSpec((DEC_BATCH, D_MODEL), lambda c, *t: (0, 0))],
        scratch_shapes=[pltpu.VMEM((MOE_CHUNK, D_MODEL), f32)],
    )

    def body(t0c_ref, t1c_ref, t0_ref, nwin_ref, *refs):
        _combine_kernel(t0_ref, nwin_ref, *refs)

    return pl.pallas_call(
        body,
        grid_spec=grid_spec,
        out_shape=[jax.ShapeDtypeStruct((N_PROMPT_ROWS, D_MODEL), f32),
                   jax.ShapeDtypeStruct((DEC_BATCH, D_MODEL), f32)],
        compiler_params=_cparams("arbitrary"),
    )(t0c, t1c, t0, nwin, h, dest_tok, cmb, g_final, *([y_sorted] * (2 * N_EXPERTS)))


def _routing_tables(dest_tok):
    i32 = jnp.int32
    cnt = (dest_tok >= 0).astype(i32).reshape(N_CHUNKS, MOE_CHUNK, N_EXPERTS).sum(axis=1)
    csum = jnp.cumsum(cnt, axis=0)
    rank_lo = jnp.concatenate([jnp.zeros((1, N_EXPERTS), i32), csum], axis=0).T
    n_e = csum[-1]
    nv = (n_e + MOE_VISIT_ROWS - 1) // MOE_VISIT_ROWS
    cum_nv = jnp.cumsum(nv)
    total = cum_nv[-1]
    v = jnp.arange(MOE_MAX_VISITS, dtype=i32)
    vv = jnp.minimum(v, jnp.maximum(total - 1, 0))
    vis_e = jnp.minimum(jnp.sum((vv[:, None] >= cum_nv[None, :]).astype(i32), axis=1), N_EXPERTS - 1)
    vis_k = vv - (cum_nv - nv)[vis_e]
    vis_ok = (v < total).astype(i32)
    rows = jnp.clip(n_e[vis_e] - vis_k * MOE_VISIT_ROWS, 0, MOE_VISIT_ROWS)
    sizes = jnp.asarray(MOE_SUB_SIZES, i32)
    n_sub = (rows[:, None] + sizes[None, :] - 1) // sizes[None, :]
    padded = jnp.where(n_sub <= MOE_SUBS_PER_VISIT, n_sub * sizes[None, :], 2 * MOE_VISIT_ROWS)
    vis_cls = jnp.argmin(padded, axis=1).astype(i32)
    vis_nsub = jnp.where(vis_ok == 1, jnp.take_along_axis(n_sub, vis_cls[:, None], axis=1)[:, 0], 0)
    t0 = rank_lo[:, :N_CHUNKS] // MOE_CHUNK
    tmax = jnp.maximum(nv * (MOE_VISIT_ROWS // MOE_CHUNK) - 1, 0)[:, None]
    moe_tables = (vis_e, vis_k, vis_nsub.astype(i32), vis_cls, vis_ok,
                  rank_lo[:, ::MOE_GATHER // MOE_CHUNK].reshape(-1))
    t_last = (rank_lo[:, 1:] - 1) // MOE_CHUNK
    nwin = jnp.where(cnt.T > 0, t_last - t0 + 1, 0)
    chunk_ids = jnp.arange(N_CHUNKS, dtype=i32)[None, :]
    last_two = lax.cummax(jnp.where(nwin == 2, chunk_ids, 0), axis=1)
    t1c = jnp.take_along_axis(jnp.minimum(t0 + 1, tmax), last_two, axis=1)
    win_tables = (t0.reshape(-1), jnp.minimum(t0, tmax).reshape(-1), t1c.reshape(-1), nwin.reshape(-1))
    return moe_tables, win_tables


def _ffn_kernel(x_ref, w1_ref, w3_ref, w2_ref, h_ref, g_ref, ho_ref, no_ref):
    f = pl.program_id(1)

    @pl.when(f == 0)
    def _():
        ho_ref[...] = h_ref[...]

    x = x_ref[...]
    a = jnp.dot(x, w1_ref[...].astype(bf16), preferred_element_type=f32)
    b = jnp.dot(x, w3_ref[...].astype(bf16), preferred_element_type=f32)
    act = (a * jax.nn.sigmoid(a) * b).astype(bf16)
    ho_ref[...] += jnp.dot(act, w2_ref[...].astype(bf16), preferred_element_type=f32)

    @pl.when(f == pl.num_programs(1) - 1)
    def _():
        no_ref[...] = _rms(ho_ref[...], g_ref[...]).astype(no_ref.dtype)


def _ffn(xn, w1, w3, w2, layer_idx, h, g_next, tm=768, tf=512):
    row = lambda i, f: (i, 0)
    once = pl.Buffered(1)
    return pl.pallas_call(
        _ffn_kernel,
        grid=(N_ROWS // tm, D_FF // tf),
        in_specs=[pl.BlockSpec((tm, D_MODEL), row),
                  pl.BlockSpec((None, D_MODEL, tf), lambda i, f: (layer_idx, 0, f)),
                  pl.BlockSpec((None, D_MODEL, tf), lambda i, f: (layer_idx, 0, f)),
                  pl.BlockSpec((None, tf, D_MODEL), lambda i, f: (layer_idx, f, 0)),
                  pl.BlockSpec((tm, D_MODEL), row, pipeline_mode=once),
                  pl.BlockSpec((1, D_MODEL), lambda i, f: (0, 0))],
        out_specs=[pl.BlockSpec((tm, D_MODEL), row, pipeline_mode=once),
                   pl.BlockSpec((tm, D_MODEL), row, pipeline_mode=once)],
        out_shape=[jax.ShapeDtypeStruct((N_ROWS, D_MODEL), f32),
                   jax.ShapeDtypeStruct((N_ROWS, D_MODEL), bf16)],
        compiler_params=_cparams("parallel", "arbitrary"),
    )(xn, w1, w3, w2, h, g_next)


def _rope_patterns(pos):
    half = ROT_DIM // 2
    inv_freq = jnp.power(ROPE_THETA, -jnp.arange(half, dtype=f32) * 2.0 / ROT_DIM)
    ang = pos.astype(f32)[:, None] * inv_freq[None, :]
    cos, sin = jnp.cos(ang), jnp.sin(ang)
    n = pos.shape[0]
    cos_h = jnp.concatenate([cos, cos, jnp.ones((n, HEAD_DIM - ROT_DIM), f32)], axis=1)
    sin_h = jnp.concatenate([-sin, sin, jnp.zeros((n, HEAD_DIM - ROT_DIM), f32)], axis=1)
    return jnp.concatenate([cos_h, cos_h, sin_h, sin_h], axis=1)


def _block_diag_ones():
    i = np.arange(SEG) // HEAD_DIM
    return jnp.asarray((i[:, None] == i[None, :]).astype(np.float32), dtype=bf16)


def kernel(x_prompt, x_sample, cache_k, cache_v, state_conv, meta_tokens, g_mix, w_in, conv_w, attn_sinks,
           g_attn_out, g_conv_out, w_out, g_ffn, dense_w1, dense_w3, dense_w2, moe_router, moe_w1, moe_w3,
           moe_w2, g_final):
    w_cache = cache_k.shape[2]
    tail = jnp.concatenate([x_sample.reshape(DEC_BATCH, D_MODEL), jnp.zeros((META_ROW_IN_BLOCK, D_MODEL), f32),
                            meta_tokens.astype(f32)], axis=0)

    prompt_pos = N_META + jnp.arange(SEQ, dtype=jnp.int32)
    tail_pos = jnp.maximum(jnp.arange(ROW_BLOCK, dtype=jnp.int32) - META_ROW_IN_BLOCK, 0)
    rope_tab = _rope_patterns(jnp.concatenate([prompt_pos, tail_pos]))
    sample_tab = _rope_patterns(jnp.full((8,), PAST_LEN, jnp.int32))
    bd = _block_diag_ones()
    cache_k = cache_k.reshape(DEPTH, DEC_BATCH, w_cache, KV_DIM)
    cache_v = cache_v.reshape(DEPTH, DEC_BATCH, w_cache, KV_DIM)

    h, xn = _norm_rows(x_prompt.reshape(N_PROMPT_ROWS, D_MODEL), tail, g_mix[0:1])
    nk_p, nv_p, nc_p, nk_s, nv_s, nc_s = [], [], [], [], [], []
    for l in range(DEPTH):
        z = _in_proj(xn, w_in, l)
        g_attn = g_attn_out[l:l + 1]
        an, nk_prompt, nv_prompt = _attn_prompt(z, attn_sinks[l], rope_tab, g_attn)
        sinks_exp = jnp.repeat(attn_sinks[l].reshape(N_KV_HEADS, GROUP).T, HEAD_DIM, axis=1)
        an_decode, nk, nv = _attn_sample(z, cache_k, cache_v, l, sample_tab, sinks_exp, g_attn, bd)
        state_l = state_conv[l].reshape(DEC_BATCH, (CONV_WIDTH - 1) * CONV_DIM)
        c, u_tail, conv_state = _conv_mix(z, state_l, conv_w[l], g_conv_out[l:l + 1], bd)
        h, xn = _out_proj(an, an_decode, c, w_out, l, h, g_ffn[l:l + 1])

        j = l // 2
        if l % 2 == 0:
            h, xn = _ffn(xn, dense_w1, dense_w3, dense_w2, j, h, g_mix[l + 1:l + 2])
        else:
            cmb, dest_tok = _router(h, g_ffn[l:l + 1], moe_router[j])
            moe_tables, win_tables = _routing_tables(dest_tok)
            dest_exp = dest_tok.T.reshape(N_EXPERTS, N_GATHER, 1, MOE_GATHER)
            y_sorted = _moe_experts(xn, dest_exp, moe_tables, moe_w1, moe_w3, moe_w2, j)
            y_prompt, y_sample = _moe_combine(y_sorted, win_tables, h, dest_tok, cmb, g_final[None])

        nk_p.append(nk_prompt[:BATCH * WINDOW].reshape(BATCH, WINDOW, N_KV_HEADS, HEAD_DIM))
        nv_p.append(nv_prompt[:BATCH * WINDOW].reshape(BATCH, WINDOW, N_KV_HEADS, HEAD_DIM))
        nc_p.append(u_tail.reshape(BATCH, 8, CONV_DIM)[:, 8 - (CONV_WIDTH - 1):])
        nk_s.append(nk.reshape(DEC_BATCH, w_cache, N_KV_HEADS, HEAD_DIM))
        nv_s.append(nv.reshape(DEC_BATCH, w_cache, N_KV_HEADS, HEAD_DIM))
        nc_s.append(conv_state.reshape(DEC_BATCH, CONV_WIDTH - 1, CONV_DIM))

    return (y_prompt.reshape(BATCH, SEQ, D_MODEL), y_sample.reshape(DEC_BATCH, 1, D_MODEL),
            jnp.stack(nk_p), jnp.stack(nv_p), jnp.stack(nc_p), jnp.stack(nk_s), jnp.stack(nv_s), jnp.stack(nc_s))
```

```python
import functools

import jax
import jax.numpy as jnp
import numpy as np
from jax import lax
from jax.experimental import pallas as pl
from jax.experimental.pallas import tpu as pltpu

D_MODEL = 2048
BATCH = 4
SEQ = 2048
DEPTH = 2
DEC_BATCH = 128
PAST_LEN = 8192
HEAD_DIM = 64
ATTN_DIM = D_MODEL // 2
N_Q_HEADS = ATTN_DIM // HEAD_DIM
N_KV_HEADS = N_Q_HEADS // 4
GROUP = N_Q_HEADS // N_KV_HEADS
KV_DIM = N_KV_HEADS * HEAD_DIM
CONV_DIM = D_MODEL - ATTN_DIM
CONV_WIDTH = 3
IN_DIM = ATTN_DIM + 2 * KV_DIM + 3 * CONV_DIM
WINDOW = 128
ROPE_THETA = 500000.0
ROT_DIM = HEAD_DIM // 4
N_META = 16
D_FF = 7 * D_MODEL // 2
N_EXPERTS = 8
EPS = 1e-5

ROW_BLOCK = 128
N_PROMPT_ROWS = BATCH * SEQ
SAMPLE_ROW0 = N_PROMPT_ROWS
TAIL_ROW0 = SAMPLE_ROW0 + DEC_BATCH
N_ROWS = TAIL_ROW0 + ROW_BLOCK
N_ROW_BLOCKS = N_ROWS // ROW_BLOCK
BLOCKS_PER_SEQ = SEQ // ROW_BLOCK
SAMPLE_BLOCK = SAMPLE_ROW0 // ROW_BLOCK
META_BLOCK = TAIL_ROW0 // ROW_BLOCK
META_ROW_IN_BLOCK = ROW_BLOCK - N_META
SEG = 256

VMEM_LIMIT_V7X = 56 * 1024 * 1024

bf16 = jnp.bfloat16
f32 = jnp.float32


def _cparams(*sem):
    return pltpu.CompilerParams(dimension_semantics=sem, vmem_limit_bytes=VMEM_LIMIT_V7X)


def _rms(x, g):
    ms = jnp.mean(x * x, axis=-1, keepdims=True)
    return (x * lax.rsqrt(ms + EPS)) * g


def _segsum(x, bd):
    x1 = x.astype(bf16)
    r1 = x - x1.astype(f32)
    x2 = r1.astype(bf16)
    x3 = (r1 - x2.astype(f32)).astype(bf16)
    d = functools.partial(jnp.dot, preferred_element_type=f32)
    return d(x1, bd) + d(x2, bd) + d(x3, bd)


def _rope(x, cos_t, sin_t):
    n = x.shape[-1]
    lane = lax.broadcasted_iota(jnp.int32, x.shape, x.ndim - 1) % HEAD_DIM
    nxt = pltpu.roll(x, n - ROT_DIM // 2, axis=x.ndim - 1)
    prv = pltpu.roll(x, ROT_DIM // 2, axis=x.ndim - 1)
    return x * cos_t + jnp.where(lane < ROT_DIM // 2, nxt, prv) * sin_t


def _norm_kernel(xp_ref, tail_ref, g_ref, h_ref, o_ref):
    is_tail = pl.program_id(0) == pl.num_programs(0) - 1
    x = jnp.where(is_tail, tail_ref[...], xp_ref[...])
    h_ref[...] = x
    o_ref[...] = _rms(x, g_ref[...]).astype(o_ref.dtype)


def _norm_rows(x_prompt_rows, tail_rows, g):
    tm = tail_rows.shape[0]
    n_prompt_tiles = N_PROMPT_ROWS // tm
    row = lambda i: (i, 0)
    return pl.pallas_call(
        _norm_kernel,
        grid=(N_ROWS // tm,),
        in_specs=[pl.BlockSpec((tm, D_MODEL), lambda i: (jnp.minimum(i, n_prompt_tiles - 1), 0)),
                  pl.BlockSpec((tm, D_MODEL), lambda i: (0, 0)),
                  pl.BlockSpec((1, D_MODEL), lambda i: (0, 0))],
        out_specs=[pl.BlockSpec((tm, D_MODEL), row), pl.BlockSpec((tm, D_MODEL), row)],
        out_shape=[jax.ShapeDtypeStruct((N_ROWS, D_MODEL), f32),
                   jax.ShapeDtypeStruct((N_ROWS, D_MODEL), bf16)],
        compiler_params=_cparams("parallel"),
    )(x_prompt_rows, tail_rows, g)


def _inproj_kernel(x_ref, w_ref, o_ref, wbf_ref):
    @pl.when(pl.program_id(1) == 0)
    def _():
        wbf_ref[...] = w_ref[...].astype(bf16)

    o_ref[...] = jnp.dot(x_ref[...], wbf_ref[...], preferred_element_type=f32)


def _in_proj(xn, w_in, layer, tm=768, tn=768):
    return pl.pallas_call(
        _inproj_kernel,
        grid=(IN_DIM // tn, N_ROWS // tm),
        in_specs=[pl.BlockSpec((tm, D_MODEL), lambda j, i: (i, 0)),
                  pl.BlockSpec((None, D_MODEL, tn), lambda j, i: (layer, 0, j))],
        out_specs=pl.BlockSpec((tm, tn), lambda j, i: (i, j)),
        out_shape=jax.ShapeDtypeStruct((N_ROWS, IN_DIM), f32),
        scratch_shapes=[pltpu.VMEM((D_MODEL, tn), bf16)],
        compiler_params=_cparams("parallel", "arbitrary"),
    )(xn, w_in)


def _attn_prompt_kernel(sink_ref, q_ref, kc_ref, vc_ref, kp_ref, vp_ref, tc_ref, tp_ref, g_ref,
                        an_ref, nk_ref, nv_ref):
    s = pl.program_id(0)
    is_meta = s == N_ROW_BLOCKS - 2
    j = s % BLOCKS_PER_SEQ
    tc = tc_ref[...]
    tp = tp_ref[...]
    q = _rope(q_ref[...] * (HEAD_DIM ** -0.5), jnp.tile(tc[:, :128], (1, 8)), jnp.tile(tc[:, 128:], (1, 8)))
    kc = _rope(kc_ref[...], jnp.tile(tc[:, :128], (1, 2)), jnp.tile(tc[:, 128:], (1, 2)))
    kp = _rope(kp_ref[...], jnp.tile(tp[:, :128], (1, 2)), jnp.tile(tp[:, 128:], (1, 2)))
    nk_ref[...] = kc
    nv_ref[...] = vc_ref[...]

    kall = jnp.concatenate([kp, kc], axis=0).astype(bf16)
    vall = jnp.concatenate([vp_ref[...], vc_ref[...]], axis=0)

    row = lax.broadcasted_iota(jnp.int32, (ROW_BLOCK, 2 * ROW_BLOCK), 0)
    col = lax.broadcasted_iota(jnp.int32, (ROW_BLOCK, 2 * ROW_BLOCK), 1)
    lo_prev = jnp.where(is_meta, ROW_BLOCK, jnp.where(j == 0, META_ROW_IN_BLOCK, 0))
    lo_cur = jnp.where(is_meta, META_ROW_IN_BLOCK, 0)
    ccur = col - ROW_BLOCK
    valid = (((col < ROW_BLOCK) & (col >= row) & (col >= lo_prev))
             | ((ccur >= 0) & (ccur <= row) & (ccur >= lo_cur)))
    bias = jnp.where(valid, 0.0, -jnp.inf)
    bias4 = jnp.concatenate([bias] * GROUP, axis=0)
    half = lax.broadcasted_iota(jnp.int32, (ROW_BLOCK, 128), 1) // HEAD_DIM
    half4 = lax.broadcasted_iota(jnp.int32, (GROUP * ROW_BLOCK, 128), 1) // HEAD_DIM
    halfkv = lax.broadcasted_iota(jnp.int32, (2 * ROW_BLOCK, 128), 1) // HEAD_DIM
    g_all = g_ref[...]

    kvs = range(N_KV_HEADS)
    q4s, sinks4, kpairs, vpairs = [], [], [], []
    for kvh in kvs:
        kside = kvh % 2
        kpairs.append(kall[:, 128 * (kvh // 2):128 * (kvh // 2) + 128])
        vpairs.append(jnp.where(halfkv == kside, vall[:, 128 * (kvh // 2):128 * (kvh // 2) + 128], 1.0).astype(bf16))
        qs, sinks = [], []
        for g in range(GROUP):
            h = GROUP * kvh + g
            x = q[:, 128 * (h // 2):128 * (h // 2) + 128]
            if h % 2 != kside:
                x = pltpu.roll(x, HEAD_DIM, axis=1)
            qs.append(jnp.where(half == kside, x, 0.0))
            sinks.append(jnp.full((ROW_BLOCK, 1), sink_ref[h], f32))
        q4s.append(jnp.concatenate(qs, axis=0).astype(bf16))
        sinks4.append(jnp.concatenate(sinks, axis=0))
    nt = (((1,), (1,)), ((), ()))
    scs = [lax.dot_general(q4s[k], kpairs[k], nt, preferred_element_type=f32) + bias4 for k in kvs]
    mxs = [jnp.maximum(jnp.max(scs[k], axis=-1, keepdims=True), sinks4[k]) for k in kvs]
    ps = [jnp.exp(scs[k] - mxs[k]).astype(bf16) for k in kvs]
    ovs = [jnp.dot(ps[k], vpairs[k], preferred_element_type=f32) for k in kvs]
    dens = [pltpu.roll(ovs[k], HEAD_DIM, axis=1) + jnp.exp(sinks4[k] - mxs[k]) for k in kvs]
    os_ = [ovs[k] * (1.0 / dens[k]) for k in kvs]
    mss = [jnp.sum(jnp.where(half4 == k % 2, os_[k] * os_[k], 0.0), axis=-1, keepdims=True) * (1.0 / HEAD_DIM)
           for k in kvs]
    ons = [os_[k] * lax.rsqrt(mss[k] + EPS) for k in kvs]
    for kvh in kvs:
        for pair in range(2):
            parts = []
            for side in range(2):
                g = 2 * pair + side
                x = ons[kvh][ROW_BLOCK * g:ROW_BLOCK * (g + 1)]
                if side != kvh % 2:
                    x = pltpu.roll(x, HEAD_DIM, axis=1)
                parts.append(x)
            blk = 2 * kvh + pair
            out = jnp.where(half == 0, parts[0], parts[1]) * g_all[:, 128 * blk:128 * blk + 128]
            an_ref[:, 128 * blk:128 * blk + 128] = out.astype(an_ref.dtype)


def _cur_block(s):
    return jnp.where(s == N_ROW_BLOCKS - 2, META_BLOCK, jnp.where(s == N_ROW_BLOCKS - 1, SAMPLE_BLOCK, s))


def _seq_of_step(s):
    return jnp.minimum(s // BLOCKS_PER_SEQ, BATCH - 1)


def _prev_block(s):
    return jnp.where((s % BLOCKS_PER_SEQ == 0) | (s == N_ROW_BLOCKS - 2), META_BLOCK, s - 1)


def _cur_tab(s):
    return jnp.where(s == N_ROW_BLOCKS - 2, BLOCKS_PER_SEQ, s % BLOCKS_PER_SEQ)


def _prev_tab(s):
    return jnp.where((s % BLOCKS_PER_SEQ == 0) | (s == N_ROW_BLOCKS - 2), BLOCKS_PER_SEQ, s % BLOCKS_PER_SEQ - 1)


def _attn_prompt(z, sinks, rope_tab, g_attn):
    kcol, vcol = ATTN_DIM // KV_DIM, ATTN_DIM // KV_DIM + 1
    return pl.pallas_call(
        _attn_prompt_kernel,
        grid=(N_ROW_BLOCKS,),
        in_specs=[pl.BlockSpec(memory_space=pltpu.SMEM),
                  pl.BlockSpec((ROW_BLOCK, ATTN_DIM), lambda s: (_cur_block(s), 0)),
                  pl.BlockSpec((ROW_BLOCK, KV_DIM), lambda s: (_cur_block(s), kcol)),
                  pl.BlockSpec((ROW_BLOCK, KV_DIM), lambda s: (_cur_block(s), vcol)),
                  pl.BlockSpec((ROW_BLOCK, KV_DIM), lambda s: (_prev_block(s), kcol)),
                  pl.BlockSpec((ROW_BLOCK, KV_DIM), lambda s: (_prev_block(s), vcol)),
                  pl.BlockSpec((ROW_BLOCK, 256), lambda s: (_cur_tab(s), 0)),
                  pl.BlockSpec((ROW_BLOCK, 256), lambda s: (_prev_tab(s), 0)),
                  pl.BlockSpec((1, ATTN_DIM), lambda s: (0, 0))],
        out_specs=[pl.BlockSpec((ROW_BLOCK, ATTN_DIM), lambda s: (_cur_block(s), 0)),
                   pl.BlockSpec((WINDOW, KV_DIM), lambda s: (s // BLOCKS_PER_SEQ, 0)),
                   pl.BlockSpec((WINDOW, KV_DIM), lambda s: (s // BLOCKS_PER_SEQ, 0))],
        out_shape=[jax.ShapeDtypeStruct((N_ROWS, ATTN_DIM), bf16),
                   jax.ShapeDtypeStruct(((BATCH + 1) * WINDOW, KV_DIM), f32),
                   jax.ShapeDtypeStruct(((BATCH + 1) * WINDOW, KV_DIM), f32)],
        compiler_params=_cparams("arbitrary"),
    )(sinks, z, z, z, z, z, rope_tab, rope_tab, g_attn)


def _attn_sample_kernel(q_ref, k_ref, v_ref, ck_ref, cv_ref, tab_ref, sink_ref, g_ref, bd_ref,
                        an_ref, nk_ref, nv_ref):
    bt = q_ref.shape[0]
    w = ck_ref.shape[1]
    tab = tab_ref[...]
    q = _rope(q_ref[...], jnp.tile(tab[0:1, :128], (1, 8)), jnp.tile(tab[0:1, 128:], (1, 8)))
    k = _rope(k_ref[...], jnp.tile(tab[0:1, :128], (1, 2)), jnp.tile(tab[0:1, 128:], (1, 2)))
    v = v_ref[...]
    ck = ck_ref[...]
    cv = cv_ref[...]
    bd = bd_ref[...]

    last = lax.broadcasted_iota(jnp.int32, (w, KV_DIM), 0) == w - 1
    for b in range(bt):
        nk_ref[b] = jnp.where(last, k[b:b + 1], pltpu.roll(ck[b], w - 1, axis=0))
        nv_ref[b] = jnp.where(last, v[b:b + 1], pltpu.roll(cv[b], w - 1, axis=0))

    seg = lax.broadcasted_iota(jnp.int32, (bt, SEG), 1) // HEAD_DIM
    scale = HEAD_DIM ** -0.5
    chunks = [jnp.zeros((bt, SEG), f32) for _ in range(N_KV_HEADS)]
    gs = range(GROUP)
    qgs = []
    for g in gs:
        qg = jnp.zeros((bt, SEG), f32)
        for kvh in range(N_KV_HEADS):
            x = q[:, SEG * kvh:SEG * (kvh + 1)]
            sh = (HEAD_DIM * (kvh - g)) % SEG
            if sh:
                x = pltpu.roll(x, sh, axis=1)
            qg = jnp.where(seg == kvh, x, qg)
        qgs.append(qg)
    scs = [(_segsum((ck * qgs[g][:, None, :]).reshape(bt * w, SEG), bd) * scale).reshape(bt, w, SEG) for g in gs]
    sns = [_segsum(qgs[g] * k, bd) * scale for g in gs]
    sinks = [sink_ref[g:g + 1, :] for g in gs]
    mxs = [jnp.maximum(jnp.maximum(jnp.max(scs[g], axis=1), sns[g]), sinks[g]) for g in gs]
    ps = [jnp.exp(scs[g] - mxs[g][:, None, :]) for g in gs]
    pns = [jnp.exp(sns[g] - mxs[g]) for g in gs]
    dens = [jnp.sum(ps[g], axis=1) + pns[g] + jnp.exp(sinks[g] - mxs[g]) for g in gs]
    os_ = [(jnp.sum(ps[g] * cv, axis=1) + pns[g] * v) / dens[g] for g in gs]
    mss = [_segsum(os_[g] * os_[g], bd) * (1.0 / HEAD_DIM) for g in gs]
    for g in gs:
        on = os_[g] * lax.rsqrt(mss[g] + EPS)
        for kvh in range(N_KV_HEADS):
            sh = (HEAD_DIM * (g - kvh)) % SEG
            x = pltpu.roll(on, sh, axis=1) if sh else on
            chunks[kvh] = jnp.where(seg == g, x, chunks[kvh])
    out = jnp.concatenate(chunks, axis=1) * g_ref[...]
    an_ref[...] = out.astype(an_ref.dtype)


def _attn_sample(z, cache_k, cache_v, layer, tab, sinks_exp, g_attn, bd, bt=16):
    w = cache_k.shape[2]
    row0 = SAMPLE_ROW0 // bt
    kcol, vcol = ATTN_DIM // KV_DIM, ATTN_DIM // KV_DIM + 1
    cache_in = pl.BlockSpec((None, bt, w, KV_DIM), lambda i: (layer, i, 0, 0))
    cache_spec = pl.BlockSpec((bt, w, KV_DIM), lambda i: (i, 0, 0))
    return pl.pallas_call(
        _attn_sample_kernel,
        grid=(DEC_BATCH // bt,),
        in_specs=[pl.BlockSpec((bt, ATTN_DIM), lambda i: (row0 + i, 0)),
                  pl.BlockSpec((bt, KV_DIM), lambda i: (row0 + i, kcol)),
                  pl.BlockSpec((bt, KV_DIM), lambda i: (row0 + i, vcol)),
                  cache_in, cache_in,
                  pl.BlockSpec((8, 256), lambda i: (0, 0)),
                  pl.BlockSpec((GROUP, SEG), lambda i: (0, 0)),
                  pl.BlockSpec((1, ATTN_DIM), lambda i: (0, 0)),
                  pl.BlockSpec((SEG, SEG), lambda i: (0, 0))],
        out_specs=[pl.BlockSpec((bt, ATTN_DIM), lambda i: (i, 0)), cache_spec, cache_spec],
        out_shape=[jax.ShapeDtypeStruct((DEC_BATCH, ATTN_DIM), bf16),
                   jax.ShapeDtypeStruct(cache_k.shape[1:], f32),
                   jax.ShapeDtypeStruct(cache_v.shape[1:], f32)],
        compiler_params=_cparams("parallel"),
    )(z, z, z, cache_k, cache_v, tab, sinks_exp, g_attn, bd)


def _conv_kernel(z1_ref, z2_ref, p1_ref, p2_ref, s0_ref, s1_ref, w_ref, g_ref, bd_ref, c_ref, utail_ref, ns_ref):
    i = pl.program_id(0)

    def split(b1, b2):
        gb = b1[:, :CONV_DIM]
        gc = jnp.concatenate([b1[:, CONV_DIM:], b2[:, :CONV_DIM // 2]], axis=1)
        return gb, gc * b2[:, CONV_DIM // 2:]

    gb, u = split(z1_ref[...], z2_ref[...])
    _, up = split(p1_ref[...], p2_ref[...])

    @pl.when(i < SAMPLE_BLOCK)
    def _():
        utail_ref[...] = u[ROW_BLOCK - 8:]

    @pl.when(i == SAMPLE_BLOCK)
    def _():
        ns_ref[:, :CONV_DIM] = s1_ref[...]
        ns_ref[:, CONV_DIM:] = u

    row = lax.broadcasted_iota(jnp.int32, u.shape, 0)
    u1 = jnp.where(row == 0, up[7:8], pltpu.roll(u, 1, axis=0))
    u2 = jnp.where(row == 0, up[6:7], jnp.where(row == 1, up[7:8], pltpu.roll(u, 2, axis=0)))
    is_sample = i == SAMPLE_BLOCK
    u1 = jnp.where(is_sample, s1_ref[...], u1)
    u2 = jnp.where(is_sample, s0_ref[...], u2)
    w = w_ref[...]
    y = u2 * w[0:1] + u1 * w[1:2] + u * w[2:3]
    t = gb * y
    bd = bd_ref[...]
    g = g_ref[...]
    for c in range(CONV_DIM // SEG):
        tc = t[:, SEG * c:SEG * (c + 1)]
        ms = _segsum(tc * tc, bd) * (1.0 / HEAD_DIM)
        c_ref[:, SEG * c:SEG * (c + 1)] = (tc * lax.rsqrt(ms + EPS) * g[:, SEG * c:SEG * (c + 1)]).astype(c_ref.dtype)


def _conv_prev_block(i):
    last8_of_meta = N_ROWS // 8 - 1
    return jnp.where(i % BLOCKS_PER_SEQ == 0, last8_of_meta, (ROW_BLOCK // 8) * i - 1)


def _conv_mix(z, state_l, conv_w_l, g_conv, bd):
    wide = 3 * CONV_DIM // 2
    return pl.pallas_call(
        _conv_kernel,
        grid=(N_ROW_BLOCKS,),
        in_specs=[pl.BlockSpec((ROW_BLOCK, wide), lambda i: (i, 1)),
                  pl.BlockSpec((ROW_BLOCK, wide), lambda i: (i, 2)),
                  pl.BlockSpec((8, wide), lambda i: (_conv_prev_block(i), 1)),
                  pl.BlockSpec((8, wide), lambda i: (_conv_prev_block(i), 2)),
                  pl.BlockSpec((DEC_BATCH, CONV_DIM), lambda i: (0, 0)),
                  pl.BlockSpec((DEC_BATCH, CONV_DIM), lambda i: (0, 1)),
                  pl.BlockSpec((CONV_WIDTH, CONV_DIM), lambda i: (0, 0)),
                  pl.BlockSpec((1, CONV_DIM), lambda i: (0, 0)),
                  pl.BlockSpec((SEG, SEG), lambda i: (0, 0))],
        out_specs=[pl.BlockSpec((ROW_BLOCK, CONV_DIM), lambda i: (i, 0)),
                   pl.BlockSpec((8, CONV_DIM), lambda i: (_seq_of_step(i), 0)),
                   pl.BlockSpec((DEC_BATCH, 2 * CONV_DIM), lambda i: (0, 0))],
        out_shape=[jax.ShapeDtypeStruct((N_ROWS, CONV_DIM), bf16),
                   jax.ShapeDtypeStruct((BATCH * 8, CONV_DIM), f32),
                   jax.ShapeDtypeStruct((DEC_BATCH, 2 * CONV_DIM), f32)],
        compiler_params=_cparams("arbitrary"),
    )(z, z, z, z, state_l, state_l, conv_w_l, g_conv, bd)


def _outproj_kernel(a_ref, as_ref, c_ref, w_ref, h_ref, g_ref, ho_ref, no_ref, wbf_ref):
    @pl.when(pl.program_id(0) == 0)
    def _():
        wbf_ref[...] = w_ref[...].astype(bf16)

    tm = a_ref.shape[0]
    off = SAMPLE_ROW0 % tm
    a = a_ref[...]
    pieces = [jnp.zeros((off, ATTN_DIM), a.dtype)] if off else []
    pieces.append(as_ref[...])
    if tm - off - DEC_BATCH:
        pieces.append(jnp.zeros((tm - off - DEC_BATCH, ATTN_DIM), a.dtype))
    row = lax.broadcasted_iota(jnp.int32, a.shape, 0)
    is_decode = (pl.program_id(0) == SAMPLE_ROW0 // tm) & (row >= off) & (row < off + DEC_BATCH)
    a = jnp.where(is_decode, jnp.concatenate(pieces, axis=0), a)
    acc = jnp.dot(a, wbf_ref[:ATTN_DIM, :], preferred_element_type=f32)
    acc += jnp.dot(c_ref[...], wbf_ref[ATTN_DIM:, :], preferred_element_type=f32)
    hn = h_ref[...] + acc
    ho_ref[...] = hn
    no_ref[...] = _rms(hn, g_ref[...]).astype(no_ref.dtype)


def _out_proj(an, an_decode, c, w_out, layer, h, g_next, tm=384):
    assert SAMPLE_ROW0 % tm + DEC_BATCH <= tm
    row = lambda i: (i, 0)
    return pl.pallas_call(
        _outproj_kernel,
        grid=(N_ROWS // tm,),
        in_specs=[pl.BlockSpec((tm, ATTN_DIM), row),
                  pl.BlockSpec((DEC_BATCH, ATTN_DIM), lambda i: (0, 0)),
                  pl.BlockSpec((tm, CONV_DIM), row),
                  pl.BlockSpec((None, D_MODEL, D_MODEL), lambda i: (layer, 0, 0), pipeline_mode=pl.Buffered(1)),
                  pl.BlockSpec((tm, D_MODEL), row),
                  pl.BlockSpec((1, D_MODEL), lambda i: (0, 0))],
        out_specs=[pl.BlockSpec((tm, D_MODEL), row), pl.BlockSpec((tm, D_MODEL), row)],
        out_shape=[jax.ShapeDtypeStruct((N_ROWS, D_MODEL), f32),
                   jax.ShapeDtypeStruct((N_ROWS, D_MODEL), bf16)],
        scratch_shapes=[pltpu.VMEM((D_MODEL, D_MODEL), bf16)],
        compiler_params=_cparams("arbitrary"),
    )(an, an_decode, c, w_out, h, g_next)


def _router_kernel(h_ref, g_ref, r_ref, tri_ref, cmb_ref, dest_ref, carry_ref):
    i = pl.program_id(0)
    tm = h_ref.shape[0]

    @pl.when(i == 0)
    def _():
        carry_ref[...] = jnp.zeros_like(carry_ref)

    n = _rms(h_ref[...], g_ref[...])
    rt = r_ref[...]
    idx = lax.broadcasted_iota(jnp.int32, (tm, N_EXPERTS), 1)
    logits = jnp.zeros((tm, N_EXPERTS), f32)
    for e in range(N_EXPERTS):
        prod = n * rt[e:e + 1, :]
        part = prod[:, :128]
        for c in range(1, D_MODEL // 128):
            part = part + prod[:, 128 * c:128 * (c + 1)]
        logits = jnp.where(idx == e, jnp.sum(part, axis=-1, keepdims=True), logits)
    m1 = jnp.max(logits, axis=-1, keepdims=True)
    i1 = jnp.min(jnp.where(logits == m1, idx, N_EXPERTS), axis=-1, keepdims=True)
    rest = jnp.where(idx == i1, -jnp.inf, logits)
    m2 = jnp.max(rest, axis=-1, keepdims=True)
    i2 = jnp.min(jnp.where(rest == m2, idx, N_EXPERTS), axis=-1, keepdims=True)
    e2 = jnp.exp(m2 - m1)
    den = 1.0 + e2
    rowid = i * tm + lax.broadcasted_iota(jnp.int32, logits.shape, 0)
    real = (rowid < TAIL_ROW0) | (rowid >= N_ROWS - N_META)
    sel = ((idx == i1) | (idx == i2)) & real
    cmb_ref[...] = jnp.where(sel, jnp.where(idx == i1, 1.0 / den, e2 / den), 0.0)
    self = jnp.where(sel, 1.0, 0.0)
    incl = jnp.dot(tri_ref[...], self.astype(bf16), preferred_element_type=f32)
    carry = carry_ref[...]
    dest_ref[...] = jnp.where(sel, incl - self + carry, -1.0).astype(jnp.int32)
    carry_ref[...] = carry + incl[tm - 1:tm, :]


def _router(h, g, router, tm=768):
    tri = jnp.asarray(np.tril(np.ones((tm, tm), np.float32)), dtype=bf16)
    return pl.pallas_call(
        _router_kernel,
        grid=(N_ROWS // tm,),
        in_specs=[pl.BlockSpec((tm, D_MODEL), lambda i: (i, 0)),
                  pl.BlockSpec((1, D_MODEL), lambda i: (0, 0)),
                  pl.BlockSpec((N_EXPERTS, D_MODEL), lambda i: (0, 0)),
                  pl.BlockSpec((tm, tm), lambda i: (0, 0))],
        out_specs=[pl.BlockSpec((tm, N_EXPERTS), lambda i: (i, 0)),
                   pl.BlockSpec((tm, N_EXPERTS), lambda i: (i, 0))],
        out_shape=[jax.ShapeDtypeStruct((N_ROWS, N_EXPERTS), f32),
                   jax.ShapeDtypeStruct((N_ROWS, N_EXPERTS), jnp.int32)],
        scratch_shapes=[pltpu.VMEM((1, N_EXPERTS), f32)],
        compiler_params=_cparams("arbitrary"),
    )(h, g, router.T, tri)


MOE_CHUNK = 256
MOE_GATHER = 768
MOE_SUB_SIZES = (640, 704, 736, 768)
MOE_SUBS_PER_VISIT = 3
MOE_VISIT_ROWS = MOE_SUBS_PER_VISIT * MOE_SUB_SIZES[-1]
MOE_CAP = 4 * MOE_VISIT_ROWS
MOE_TF = 256
N_CHUNKS = N_ROWS // MOE_CHUNK
N_GATHER = N_ROWS // MOE_GATHER
N_FSTEPS = D_FF // MOE_TF
MOE_MAX_VISITS = (2 * N_ROWS) // MOE_VISIT_ROWS + N_EXPERTS


def _moe_kernel(vis_e, vis_k, vis_nsub, vis_cls, vis_ok, rlo, xn_ref, dest_ref, w1_ref, w3_ref, w2_ref, y_ref, acc_ref):
    v, p = pl.program_id(0), pl.program_id(1)
    ok = vis_ok[v] == 1
    e = vis_e[v]
    base = vis_k[v] * MOE_VISIT_ROWS

    @pl.when(ok & (p == 0))
    def _():
        y_ref[...] = jnp.zeros_like(y_ref)

    @pl.when(ok & (p < N_GATHER))
    def _():
        lo = jnp.maximum(rlo[e * (N_GATHER + 1) + p] - base, 0)
        hi = jnp.minimum(rlo[e * (N_GATHER + 1) + p + 1] - base, MOE_VISIT_ROWS)

        @pl.when(hi > lo)
        def _():
            dest = dest_ref[...] - base
            win = lax.broadcasted_iota(jnp.int32, (MOE_CHUNK, MOE_GATHER), 0)

            def tile_body(t, carry):
                off = pl.multiple_of(t * MOE_CHUNK, MOE_CHUNK)
                onehot = jnp.where(win == dest - off, 1.0, 0.0).astype(bf16)
                rows = jnp.dot(onehot, xn_ref[...], preferred_element_type=f32)
                y_ref[pl.ds(off, MOE_CHUNK), :] += rows.astype(bf16)
                return carry

            lax.fori_loop(lo // MOE_CHUNK, (hi - 1) // MOE_CHUNK + 1, tile_body, 0)

    @pl.when(ok & (p >= N_GATHER))
    def _():
        f = p - N_GATHER
        w1 = w1_ref[...].astype(bf16)
        w3 = w3_ref[...].astype(bf16)
        w2 = w2_ref[...].astype(bf16)
        for cls, sub in enumerate(MOE_SUB_SIZES):
            @pl.when(vis_cls[v] == cls)
            def _():
                def sub_body(r, carry):
                    rows = pl.ds(pl.multiple_of(r * sub, 16), sub)

                    @pl.when(f == 0)
                    def _():
                        acc_ref[rows, :] = jnp.zeros((sub, D_MODEL), f32)

                    x = y_ref[rows, :]
                    a = jnp.dot(x, w1, preferred_element_type=f32)
                    b = jnp.dot(x, w3, preferred_element_type=f32)
                    act = (a * jax.nn.sigmoid(a) * b).astype(bf16)
                    acc_ref[rows, :] += jnp.dot(act, w2, preferred_element_type=f32)

                    @pl.when(f == N_FSTEPS - 1)
                    def _():
                        y_ref[rows, :] = acc_ref[rows, :].astype(y_ref.dtype)

                    return carry

                lax.fori_loop(0, vis_nsub[v], sub_body, 0)


def _moe_experts(xn, dest_exp, tables, w1, w3, w2, layer_idx):
    def chunk(v, p, ve, vk, vn, vc, vok, rlo):
        return jnp.where(vok[v] == 1, jnp.minimum(p, N_GATHER - 1), N_GATHER - 1)

    def fstep(v, p, ve, vk, vn, vc, vok, rlo):
        return jnp.where(vok[v] == 1, jnp.maximum(p - N_GATHER, 0), N_FSTEPS - 1)

    grid_spec = pltpu.PrefetchScalarGridSpec(
        num_scalar_prefetch=6,
        grid=(MOE_MAX_VISITS, N_GATHER + N_FSTEPS),
        in_specs=[pl.BlockSpec((MOE_GATHER, D_MODEL), lambda v, p, *t: (chunk(v, p, *t), 0)),
                  pl.BlockSpec((None, None, 1, MOE_GATHER), lambda v, p, *t: (t[0][v], chunk(v, p, *t), 0, 0)),
                  pl.BlockSpec((None, None, D_MODEL, MOE_TF), lambda v, p, *t: (layer_idx, t[0][v], 0, fstep(v, p, *t))),
                  pl.BlockSpec((None, None, D_MODEL, MOE_TF), lambda v, p, *t: (layer_idx, t[0][v], 0, fstep(v, p, *t))),
                  pl.BlockSpec((None, None, MOE_TF, D_MODEL), lambda v, p, *t: (layer_idx, t[0][v], fstep(v, p, *t), 0))],
        out_specs=pl.BlockSpec((None, MOE_VISIT_ROWS, D_MODEL), lambda v, p, *t: (t[0][v], t[1][v], 0),
                               pipeline_mode=pl.Buffered(1)),
        scratch_shapes=[pltpu.VMEM((MOE_VISIT_ROWS, D_MODEL), f32)],
    )
    return pl.pallas_call(
        _moe_kernel,
        grid_spec=grid_spec,
        out_shape=jax.ShapeDtypeStruct((N_EXPERTS, MOE_CAP, D_MODEL), bf16),
        compiler_params=_cparams("arbitrary", "arbitrary"),
    )(*tables, xn, dest_exp, w1, w3, w2)


def _combine_kernel(t0, nwin, h_ref, dest_ref, cmb_ref, g_ref, *rest):
    ywin = rest[:2 * N_EXPERTS]
    yp_ref, ys_ref, acc_ref = rest[2 * N_EXPERTS:]
    c = pl.program_id(0)
    acc_ref[...] = h_ref[...]
    dest = dest_ref[...]
    cmb = cmb_ref[...]
    lane = lax.broadcasted_iota(jnp.int32, (MOE_CHUNK, MOE_CHUNK), 1)
    for e in range(N_EXPERTS):
        for w in range(2):
            @pl.when(nwin[e * N_CHUNKS + c] > w)
            def _():
                rel = dest[:, e:e + 1] - MOE_CHUNK * (t0[e * N_CHUNKS + c] + w)
                onehot = jnp.where(rel == lane, 1.0, 0.0).astype(bf16)
                got = jnp.dot(onehot, ywin[2 * e + w][...], preferred_element_type=f32)
                acc_ref[...] += cmb[:, e:e + 1] * got

    n = _rms(acc_ref[...], g_ref[...])
    n_prompt_chunks = N_PROMPT_ROWS // MOE_CHUNK

    @pl.when(c < n_prompt_chunks)
    def _():
        yp_ref[...] = n

    @pl.when(c == n_prompt_chunks)
    def _():
        ys_ref[...] = n[:DEC_BATCH]


def _moe_combine(y_sorted, win_tables, h, dest_tok, cmb, g_final):
    t0, t0c, t1c, nwin = win_tables
    row = lambda c, *t: (c, 0)
    n_prompt_chunks = N_PROMPT_ROWS // MOE_CHUNK
    wins = []
    for e in range(N_EXPERTS):
        wins.append(pl.BlockSpec((None, MOE_CHUNK, D_MODEL), lambda c, a, b, *t, e=e: (e, a[e * N_CHUNKS + c], 0)))
        wins.append(pl.BlockSpec((None, MOE_CHUNK, D_MODEL), lambda c, a, b, *t, e=e: (e, b[e * N_CHUNKS + c], 0)))
    grid_spec = pltpu.PrefetchScalarGridSpec(
        num_scalar_prefetch=4,
        grid=(N_CHUNKS,),
        in_specs=[pl.BlockSpec((MOE_CHUNK, D_MODEL), row),
                  pl.BlockSpec((MOE_CHUNK, N_EXPERTS), row),
                  pl.BlockSpec((MOE_CHUNK, N_EXPERTS), row),
                  pl.BlockSpec((1, D_MODEL), lambda c, *t: (0, 0))] + wins,
        out_specs=[pl.BlockSpec((MOE_CHUNK, D_MODEL), lambda c, *t: (jnp.minimum(c, n_prompt_chunks - 1), 0)),
                   pl.BlockSpec((DEC_BATCH, D_MODEL), lambda c, *t: (0, 0))],
        scratch_shapes=[pltpu.VMEM((MOE_CHUNK, D_MODEL), f32)],
    )

    def body(t0c_ref, t1c_ref, t0_ref, nwin_ref, *refs):
        _combine_kernel(t0_ref, nwin_ref, *refs)

    return pl.pallas_call(
        body,
        grid_spec=grid_spec,
        out_shape=[jax.ShapeDtypeStruct((N_PROMPT_ROWS, D_MODEL), f32),
                   jax.ShapeDtypeStruct((DEC_BATCH, D_MODEL), f32)],
        compiler_params=_cparams("arbitrary"),
    )(t0c, t1c, t0, nwin, h, dest_tok, cmb, g_final, *([y_sorted] * (2 * N_EXPERTS)))


def _routing_tables(dest_tok):
    i32 = jnp.int32
    cnt = (dest_tok >= 0).astype(i32).reshape(N_CHUNKS, MOE_CHUNK, N_EXPERTS).sum(axis=1)
    csum = jnp.cumsum(cnt, axis=0)
    rank_lo = jnp.concatenate([jnp.zeros((1, N_EXPERTS), i32), csum], axis=0).T
    n_e = csum[-1]
    nv = (n_e + MOE_VISIT_ROWS - 1) // MOE_VISIT_ROWS
    cum_nv = jnp.cumsum(nv)
    total = cum_nv[-1]
    v = jnp.arange(MOE_MAX_VISITS, dtype=i32)
    vv = jnp.minimum(v, jnp.maximum(total - 1, 0))
    vis_e = jnp.minimum(jnp.sum((vv[:, None] >= cum_nv[None, :]).astype(i32), axis=1), N_EXPERTS - 1)
    vis_k = vv - (cum_nv - nv)[vis_e]
    vis_ok = (v < total).astype(i32)
    rows = jnp.clip(n_e[vis_e] - vis_k * MOE_VISIT_ROWS, 0, MOE_VISIT_ROWS)
    sizes = jnp.asarray(MOE_SUB_SIZES, i32)
    n_sub = (rows[:, None] + sizes[None, :] - 1) // sizes[None, :]
    padded = jnp.where(n_sub <= MOE_SUBS_PER_VISIT, n_sub * sizes[None, :], 2 * MOE_VISIT_ROWS)
    vis_cls = jnp.argmin(padded, axis=1).astype(i32)
    vis_nsub = jnp.where(vis_ok == 1, jnp.take_along_axis(n_sub, vis_cls[:, None], axis=1)[:, 0], 0)
    t0 = rank_lo[:, :N_CHUNKS] // MOE_CHUNK
    tmax = jnp.maximum(nv * (MOE_VISIT_ROWS // MOE_CHUNK) - 1, 0)[:, None]
    moe_tables = (vis_e, vis_k, vis_nsub.astype(i32), vis_cls, vis_ok,
                  rank_lo[:, ::MOE_GATHER // MOE_CHUNK].reshape(-1))
    t_last = (rank_lo[:, 1:] - 1) // MOE_CHUNK
    nwin = jnp.where(cnt.T > 0, t_last - t0 + 1, 0)
    chunk_ids = jnp.arange(N_CHUNKS, dtype=i32)[None, :]
    last_two = lax.cummax(jnp.where(nwin == 2, chunk_ids, 0), axis=1)
    t1c = jnp.take_along_axis(jnp.minimum(t0 + 1, tmax), last_two, axis=1)
    win_tables = (t0.reshape(-1), jnp.minimum(t0, tmax).reshape(-1), t1c.reshape(-1), nwin.reshape(-1))
    return moe_tables, win_tables


def _ffn_kernel(x_ref, w1_ref, w3_ref, w2_ref, h_ref, g_ref, ho_ref, no_ref):
    f = pl.program_id(1)

    @pl.when(f == 0)
    def _():
        ho_ref[...] = h_ref[...]

    x = x_ref[...]
    a = jnp.dot(x, w1_ref[...].astype(bf16), preferred_element_type=f32)
    b = jnp.dot(x, w3_ref[...].astype(bf16), preferred_element_type=f32)
    act = (a * jax.nn.sigmoid(a) * b).astype(bf16)
    ho_ref[...] += jnp.dot(act, w2_ref[...].astype(bf16), preferred_element_type=f32)

    @pl.when(f == pl.num_programs(1) - 1)
    def _():
        no_ref[...] = _rms(ho_ref[...], g_ref[...]).astype(no_ref.dtype)


def _ffn(xn, w1, w3, w2, layer_idx, h, g_next, tm=768, tf=512):
    row = lambda i, f: (i, 0)
    once = pl.Buffered(1)
    return pl.pallas_call(
        _ffn_kernel,
        grid=(N_ROWS // tm, D_FF // tf),
        in_specs=[pl.BlockSpec((tm, D_MODEL), row),
                  pl.BlockSpec((None, D_MODEL, tf), lambda i, f: (layer_idx, 0, f)),
                  pl.BlockSpec((None, D_MODEL, tf), lambda i, f: (layer_idx, 0, f)),
                  pl.BlockSpec((None, tf, D_MODEL), lambda i, f: (layer_idx, f, 0)),
                  pl.BlockSpec((tm, D_MODEL), row, pipeline_mode=once),
                  pl.BlockSpec((1, D_MODEL), lambda i, f: (0, 0))],
        out_specs=[pl.BlockSpec((tm, D_MODEL), row, pipeline_mode=once),
                   pl.BlockSpec((tm, D_MODEL), row, pipeline_mode=once)],
        out_shape=[jax.ShapeDtypeStruct((N_ROWS, D_MODEL), f32),
                   jax.ShapeDtypeStruct((N_ROWS, D_MODEL), bf16)],
        compiler_params=_cparams("parallel", "arbitrary"),
    )(xn, w1, w3, w2, h, g_next)


def _rope_patterns(pos):
    half = ROT_DIM // 2
    inv_freq = jnp.power(ROPE_THETA, -jnp.arange(half, dtype=f32) * 2.0 / ROT_DIM)
    ang = pos.astype(f32)[:, None] * inv_freq[None, :]
    cos, sin = jnp.cos(ang), jnp.sin(ang)
    n = pos.shape[0]
    cos_h = jnp.concatenate([cos, cos, jnp.ones((n, HEAD_DIM - ROT_DIM), f32)], axis=1)
    sin_h = jnp.concatenate([-sin, sin, jnp.zeros((n, HEAD_DIM - ROT_DIM), f32)], axis=1)
    return jnp.concatenate([cos_h, cos_h, sin_h, sin_h], axis=1)


def _block_diag_ones():
    i = np.arange(SEG) // HEAD_DIM
    return jnp.asarray((i[:, None] == i[None, :]).astype(np.float32), dtype=bf16)


def kernel(x_prompt, x_sample, cache_k, cache_v, state_conv, meta_tokens, g_mix, w_in, conv_w, attn_sinks,
           g_attn_out, g_conv_out, w_out, g_ffn, dense_w1, dense_w3, dense_w2, moe_router, moe_w1, moe_w3,
           moe_w2, g_final):
    w_cache = cache_k.shape[2]
    tail = jnp.concatenate([x_sample.reshape(DEC_BATCH, D_MODEL), jnp.zeros((META_ROW_IN_BLOCK, D_MODEL), f32),
                            meta_tokens.astype(f32)], axis=0)

    prompt_pos = N_META + jnp.arange(SEQ, dtype=jnp.int32)
    tail_pos = jnp.maximum(jnp.arange(ROW_BLOCK, dtype=jnp.int32) - META_ROW_IN_BLOCK, 0)
    rope_tab = _rope_patterns(jnp.concatenate([prompt_pos, tail_pos]))
    sample_tab = _rope_patterns(jnp.full((8,), PAST_LEN, jnp.int32))
    bd = _block_diag_ones()
    cache_k = cache_k.reshape(DEPTH, DEC_BATCH, w_cache, KV_DIM)
    cache_v = cache_v.reshape(DEPTH, DEC_BATCH, w_cache, KV_DIM)

    h, xn = _norm_rows(x_prompt.reshape(N_PROMPT_ROWS, D_MODEL), tail, g_mix[0:1])
    nk_p, nv_p, nc_p, nk_s, nv_s, nc_s = [], [], [], [], [], []
    for l in range(DEPTH):
        z = _in_proj(xn, w_in, l)
        g_attn = g_attn_out[l:l + 1]
        an, nk_prompt, nv_prompt = _attn_prompt(z, attn_sinks[l], rope_tab, g_attn)
        sinks_exp = jnp.repeat(attn_sinks[l].reshape(N_KV_HEADS, GROUP).T, HEAD_DIM, axis=1)
        an_decode, nk, nv = _attn_sample(z, cache_k, cache_v, l, sample_tab, sinks_exp, g_attn, bd)
        state_l = state_conv[l].reshape(DEC_BATCH, (CONV_WIDTH - 1) * CONV_DIM)
        c, u_tail, conv_state = _conv_mix(z, state_l, conv_w[l], g_conv_out[l:l + 1], bd)
        h, xn = _out_proj(an, an_decode, c, w_out, l, h, g_ffn[l:l + 1])

        j = l // 2
        if l % 2 == 0:
            h, xn = _ffn(xn, dense_w1, dense_w3, dense_w2, j, h, g_mix[l + 1:l + 2])
        else:
            cmb, dest_tok = _router(h, g_ffn[l:l + 1], moe_router[j])
            moe_tables, win_tables = _routing_tables(dest_tok)
            dest_exp = dest_tok.T.reshape(N_EXPERTS, N_GATHER, 1, MOE_GATHER)
            y_sorted = _moe_experts(xn, dest_exp, moe_tables, moe_w1, moe_w3, moe_w2, j)
            y_prompt, y_sample = _moe_combine(y_sorted, win_tables, h, dest_tok, cmb, g_final[None])

        nk_p.append(nk_prompt[:BATCH * WINDOW].reshape(BATCH, WINDOW, N_KV_HEADS, HEAD_DIM))
        nv_p.append(nv_prompt[:BATCH * WINDOW].reshape(BATCH, WINDOW, N_KV_HEADS, HEAD_DIM))
        nc_p.append(u_tail.reshape(BATCH, 8, CONV_DIM)[:, 8 - (CONV_WIDTH - 1):])
        nk_s.append(nk.reshape(DEC_BATCH, w_cache, N_KV_HEADS, HEAD_DIM))
        nv_s.append(nv.reshape(DEC_BATCH, w_cache, N_KV_HEADS, HEAD_DIM))
        nc_s.append(conv_state.reshape(DEC_BATCH, CONV_WIDTH - 1, CONV_DIM))

    return (y_prompt.reshape(BATCH, SEQ, D_MODEL), y_sample.reshape(DEC_BATCH, 1, D_MODEL),
            jnp.stack(nk_p), jnp.stack(nv_p), jnp.stack(nc_p), jnp.stack(nk_s), jnp.stack(nv_s), jnp.stack(nc_s))
```

```python
import functools

import jax
import jax.numpy as jnp
import numpy as np
from jax import lax
from jax.experimental import pallas as pl
from jax.experimental.pallas import tpu as pltpu

D_MODEL = 2048
BATCH = 4
SEQ = 2048
DEPTH = 2
DEC_BATCH = 128
PAST_LEN = 8192
HEAD_DIM = 64
ATTN_DIM = D_MODEL // 2
N_Q_HEADS = ATTN_DIM // HEAD_DIM
N_KV_HEADS = N_Q_HEADS // 4
GROUP = N_Q_HEADS // N_KV_HEADS
KV_DIM = N_KV_HEADS * HEAD_DIM
CONV_DIM = D_MODEL - ATTN_DIM
CONV_WIDTH = 3
IN_DIM = ATTN_DIM + 2 * KV_DIM + 3 * CONV_DIM
WINDOW = 128
ROPE_THETA = 500000.0
ROT_DIM = HEAD_DIM // 4
N_META = 16
D_FF = 7 * D_MODEL // 2
N_EXPERTS = 8
EPS = 1e-5

ROW_BLOCK = 128
N_PROMPT_ROWS = BATCH * SEQ
SAMPLE_ROW0 = N_PROMPT_ROWS
TAIL_ROW0 = SAMPLE_ROW0 + DEC_BATCH
N_ROWS = TAIL_ROW0 + ROW_BLOCK
N_ROW_BLOCKS = N_ROWS // ROW_BLOCK
BLOCKS_PER_SEQ = SEQ // ROW_BLOCK
SAMPLE_BLOCK = SAMPLE_ROW0 // ROW_BLOCK
META_BLOCK = TAIL_ROW0 // ROW_BLOCK
META_ROW_IN_BLOCK = ROW_BLOCK - N_META
SEG = 256

VMEM_LIMIT_V7X = 56 * 1024 * 1024

bf16 = jnp.bfloat16
f32 = jnp.float32


def _cparams(*sem):
    return pltpu.CompilerParams(dimension_semantics=sem, vmem_limit_bytes=VMEM_LIMIT_V7X)


def _rms(x, g):
    ms = jnp.mean(x * x, axis=-1, keepdims=True)
    return (x * lax.rsqrt(ms + EPS)) * g


def _segsum(x, bd):
    x1 = x.astype(bf16)
    r1 = x - x1.astype(f32)
    x2 = r1.astype(bf16)
    x3 = (r1 - x2.astype(f32)).astype(bf16)
    d = functools.partial(jnp.dot, preferred_element_type=f32)
    return d(x1, bd) + d(x2, bd) + d(x3, bd)


def _rope(x, cos_t, sin_t):
    n = x.shape[-1]
    lane = lax.broadcasted_iota(jnp.int32, x.shape, x.ndim - 1) % HEAD_DIM
    nxt = pltpu.roll(x, n - ROT_DIM // 2, axis=x.ndim - 1)
    prv = pltpu.roll(x, ROT_DIM // 2, axis=x.ndim - 1)
    return x * cos_t + jnp.where(lane < ROT_DIM // 2, nxt, prv) * sin_t


def _norm_kernel(xp_ref, tail_ref, g_ref, h_ref, o_ref):
    is_tail = pl.program_id(0) == pl.num_programs(0) - 1
    x = jnp.where(is_tail, tail_ref[...], xp_ref[...])
    h_ref[...] = x
    o_ref[...] = _rms(x, g_ref[...]).astype(o_ref.dtype)


def _norm_rows(x_prompt_rows, tail_rows, g):
    tm = tail_rows.shape[0]
    n_prompt_tiles = N_PROMPT_ROWS // tm
    row = lambda i: (i, 0)
    return pl.pallas_call(
        _norm_kernel,
        grid=(N_ROWS // tm,),
        in_specs=[pl.BlockSpec((tm, D_MODEL), lambda i: (jnp.minimum(i, n_prompt_tiles - 1), 0)),
                  pl.BlockSpec((tm, D_MODEL), lambda i: (0, 0)),
                  pl.BlockSpec((1, D_MODEL), lambda i: (0, 0))],
        out_specs=[pl.BlockSpec((tm, D_MODEL), row), pl.BlockSpec((tm, D_MODEL), row)],
        out_shape=[jax.ShapeDtypeStruct((N_ROWS, D_MODEL), f32),
                   jax.ShapeDtypeStruct((N_ROWS, D_MODEL), bf16)],
        compiler_params=_cparams("parallel"),
    )(x_prompt_rows, tail_rows, g)


def _inproj_kernel(x_ref, w_ref, o_ref, wbf_ref):
    @pl.when(pl.program_id(1) == 0)
    def _():
        wbf_ref[...] = w_ref[...].astype(bf16)

    o_ref[...] = jnp.dot(x_ref[...], wbf_ref[...], preferred_element_type=f32)


def _in_proj(xn, w_in, layer, tm=768, tn=1536):
    return pl.pallas_call(
        _inproj_kernel,
        grid=(IN_DIM // tn, N_ROWS // tm),
        in_specs=[pl.BlockSpec((tm, D_MODEL), lambda j, i: (i, 0)),
                  pl.BlockSpec((None, D_MODEL, tn), lambda j, i: (layer, 0, j))],
        out_specs=pl.BlockSpec((tm, tn), lambda j, i: (i, j)),
        out_shape=jax.ShapeDtypeStruct((N_ROWS, IN_DIM), f32),
        scratch_shapes=[pltpu.VMEM((D_MODEL, tn), bf16)],
        compiler_params=_cparams("parallel", "arbitrary"),
    )(xn, w_in)


def _attn_prompt_kernel(sink_ref, q_ref, kc_ref, vc_ref, kp_ref, vp_ref, tc_ref, tp_ref, g_ref,
                        an_ref, nk_ref, nv_ref):
    s = pl.program_id(0)
    is_meta = s == N_ROW_BLOCKS - 2
    j = s % BLOCKS_PER_SEQ
    tc = tc_ref[...]
    tp = tp_ref[...]
    q = _rope(q_ref[...] * (HEAD_DIM ** -0.5), jnp.tile(tc[:, :128], (1, 8)), jnp.tile(tc[:, 128:], (1, 8)))
    kc = _rope(kc_ref[...], jnp.tile(tc[:, :128], (1, 2)), jnp.tile(tc[:, 128:], (1, 2)))
    kp = _rope(kp_ref[...], jnp.tile(tp[:, :128], (1, 2)), jnp.tile(tp[:, 128:], (1, 2)))
    nk_ref[...] = kc
    nv_ref[...] = vc_ref[...]

    kall = jnp.concatenate([kp, kc], axis=0).astype(bf16)
    vall = jnp.concatenate([vp_ref[...], vc_ref[...]], axis=0)

    row = lax.broadcasted_iota(jnp.int32, (ROW_BLOCK, 2 * ROW_BLOCK), 0)
    col = lax.broadcasted_iota(jnp.int32, (ROW_BLOCK, 2 * ROW_BLOCK), 1)
    lo_prev = jnp.where(is_meta, ROW_BLOCK, jnp.where(j == 0, META_ROW_IN_BLOCK, 0))
    lo_cur = jnp.where(is_meta, META_ROW_IN_BLOCK, 0)
    ccur = col - ROW_BLOCK
    valid = (((col < ROW_BLOCK) & (col >= row) & (col >= lo_prev))
             | ((ccur >= 0) & (ccur <= row) & (ccur >= lo_cur)))
    bias = jnp.where(valid, 0.0, -jnp.inf)
    bias4 = jnp.concatenate([bias] * GROUP, axis=0)
    half = lax.broadcasted_iota(jnp.int32, (ROW_BLOCK, 128), 1) // HEAD_DIM
    half4 = lax.broadcasted_iota(jnp.int32, (GROUP * ROW_BLOCK, 128), 1) // HEAD_DIM
    halfkv = lax.broadcasted_iota(jnp.int32, (2 * ROW_BLOCK, 128), 1) // HEAD_DIM
    g_all = g_ref[...]

    kvs = range(N_KV_HEADS)
    q4s, sinks4, kpairs, vpairs = [], [], [], []
    for kvh in kvs:
        kside = kvh % 2
        kpairs.append(kall[:, 128 * (kvh // 2):128 * (kvh // 2) + 128])
        vpairs.append(jnp.where(halfkv == kside, vall[:, 128 * (kvh // 2):128 * (kvh // 2) + 128], 1.0).astype(bf16))
        qs, sinks = [], []
        for g in range(GROUP):
            h = GROUP * kvh + g
            x = q[:, 128 * (h // 2):128 * (h // 2) + 128]
            if h % 2 != kside:
                x = pltpu.roll(x, HEAD_DIM, axis=1)
            qs.append(jnp.where(half == kside, x, 0.0))
            sinks.append(jnp.full((ROW_BLOCK, 1), sink_ref[h], f32))
        q4s.append(jnp.concatenate(qs, axis=0).astype(bf16))
        sinks4.append(jnp.concatenate(sinks, axis=0))
    nt = (((1,), (1,)), ((), ()))
    scs = [lax.dot_general(q4s[k], kpairs[k], nt, preferred_element_type=f32) + bias4 for k in kvs]
    mxs = [jnp.maximum(jnp.max(scs[k], axis=-1, keepdims=True), sinks4[k]) for k in kvs]
    ps = [jnp.exp(scs[k] - mxs[k]).astype(bf16) for k in kvs]
    ovs = [jnp.dot(ps[k], vpairs[k], preferred_element_type=f32) for k in kvs]
    dens = [pltpu.roll(ovs[k], HEAD_DIM, axis=1) + jnp.exp(sinks4[k] - mxs[k]) for k in kvs]
    os_ = [ovs[k] * (1.0 / dens[k]) for k in kvs]
    mss = [jnp.sum(jnp.where(half4 == k % 2, os_[k] * os_[k], 0.0), axis=-1, keepdims=True) * (1.0 / HEAD_DIM)
           for k in kvs]
    ons = [os_[k] * lax.rsqrt(mss[k] + EPS) for k in kvs]
    for kvh in kvs:
        for pair in range(2):
            parts = []
            for side in range(2):
                g = 2 * pair + side
                x = ons[kvh][ROW_BLOCK * g:ROW_BLOCK * (g + 1)]
                if side != kvh % 2:
                    x = pltpu.roll(x, HEAD_DIM, axis=1)
                parts.append(x)
            blk = 2 * kvh + pair
            out = jnp.where(half == 0, parts[0], parts[1]) * g_all[:, 128 * blk:128 * blk + 128]
            an_ref[:, 128 * blk:128 * blk + 128] = out.astype(an_ref.dtype)


def _cur_block(s):
    return jnp.where(s == N_ROW_BLOCKS - 2, META_BLOCK, jnp.where(s == N_ROW_BLOCKS - 1, SAMPLE_BLOCK, s))


def _seq_of_step(s):
    return jnp.minimum(s // BLOCKS_PER_SEQ, BATCH - 1)


def _prev_block(s):
    return jnp.where((s % BLOCKS_PER_SEQ == 0) | (s == N_ROW_BLOCKS - 2), META_BLOCK, s - 1)


def _cur_tab(s):
    return jnp.where(s == N_ROW_BLOCKS - 2, BLOCKS_PER_SEQ, s % BLOCKS_PER_SEQ)


def _prev_tab(s):
    return jnp.where((s % BLOCKS_PER_SEQ == 0) | (s == N_ROW_BLOCKS - 2), BLOCKS_PER_SEQ, s % BLOCKS_PER_SEQ - 1)


def _attn_prompt(z, sinks, rope_tab, g_attn):
    kcol, vcol = ATTN_DIM // KV_DIM, ATTN_DIM // KV_DIM + 1
    return pl.pallas_call(
        _attn_prompt_kernel,
        grid=(N_ROW_BLOCKS,),
        in_specs=[pl.BlockSpec(memory_space=pltpu.SMEM),
                  pl.BlockSpec((ROW_BLOCK, ATTN_DIM), lambda s: (_cur_block(s), 0)),
                  pl.BlockSpec((ROW_BLOCK, KV_DIM), lambda s: (_cur_block(s), kcol)),
                  pl.BlockSpec((ROW_BLOCK, KV_DIM), lambda s: (_cur_block(s), vcol)),
                  pl.BlockSpec((ROW_BLOCK, KV_DIM), lambda s: (_prev_block(s), kcol)),
                  pl.BlockSpec((ROW_BLOCK, KV_DIM), lambda s: (_prev_block(s), vcol)),
                  pl.BlockSpec((ROW_BLOCK, 256), lambda s: (_cur_tab(s), 0)),
                  pl.BlockSpec((ROW_BLOCK, 256), lambda s: (_prev_tab(s), 0)),
                  pl.BlockSpec((1, ATTN_DIM), lambda s: (0, 0))],
        out_specs=[pl.BlockSpec((ROW_BLOCK, ATTN_DIM), lambda s: (_cur_block(s), 0)),
                   pl.BlockSpec((WINDOW, KV_DIM), lambda s: (s // BLOCKS_PER_SEQ, 0)),
                   pl.BlockSpec((WINDOW, KV_DIM), lambda s: (s // BLOCKS_PER_SEQ, 0))],
        out_shape=[jax.ShapeDtypeStruct((N_ROWS, ATTN_DIM), bf16),
                   jax.ShapeDtypeStruct(((BATCH + 1) * WINDOW, KV_DIM), f32),
                   jax.ShapeDtypeStruct(((BATCH + 1) * WINDOW, KV_DIM), f32)],
        compiler_params=_cparams("arbitrary"),
    )(sinks, z, z, z, z, z, rope_tab, rope_tab, g_attn)


def _attn_sample_kernel(q_ref, k_ref, v_ref, ck_ref, cv_ref, tab_ref, sink_ref, g_ref, bd_ref,
                        an_ref, nk_ref, nv_ref):
    bt = q_ref.shape[0]
    w = ck_ref.shape[1]
    tab = tab_ref[...]
    q = _rope(q_ref[...], jnp.tile(tab[0:1, :128], (1, 8)), jnp.tile(tab[0:1, 128:], (1, 8)))
    k = _rope(k_ref[...], jnp.tile(tab[0:1, :128], (1, 2)), jnp.tile(tab[0:1, 128:], (1, 2)))
    v = v_ref[...]
    ck = ck_ref[...]
    cv = cv_ref[...]
    bd = bd_ref[...]

    last = lax.broadcasted_iota(jnp.int32, (w, KV_DIM), 0) == w - 1
    for b in range(bt):
        nk_ref[b] = jnp.where(last, k[b:b + 1], pltpu.roll(ck[b], w - 1, axis=0))
        nv_ref[b] = jnp.where(last, v[b:b + 1], pltpu.roll(cv[b], w - 1, axis=0))

    seg = lax.broadcasted_iota(jnp.int32, (bt, SEG), 1) // HEAD_DIM
    scale = HEAD_DIM ** -0.5
    chunks = [jnp.zeros((bt, SEG), f32) for _ in range(N_KV_HEADS)]
    gs = range(GROUP)
    qgs = []
    for g in gs:
        qg = jnp.zeros((bt, SEG), f32)
        for kvh in range(N_KV_HEADS):
            x = q[:, SEG * kvh:SEG * (kvh + 1)]
            sh = (HEAD_DIM * (kvh - g)) % SEG
            if sh:
                x = pltpu.roll(x, sh, axis=1)
            qg = jnp.where(seg == kvh, x, qg)
        qgs.append(qg)
    scs = [(_segsum((ck * qgs[g][:, None, :]).reshape(bt * w, SEG), bd) * scale).reshape(bt, w, SEG) for g in gs]
    sns = [_segsum(qgs[g] * k, bd) * scale for g in gs]
    sinks = [sink_ref[g:g + 1, :] for g in gs]
    mxs = [jnp.maximum(jnp.maximum(jnp.max(scs[g], axis=1), sns[g]), sinks[g]) for g in gs]
    ps = [jnp.exp(scs[g] - mxs[g][:, None, :]) for g in gs]
    pns = [jnp.exp(sns[g] - mxs[g]) for g in gs]
    dens = [jnp.sum(ps[g], axis=1) + pns[g] + jnp.exp(sinks[g] - mxs[g]) for g in gs]
    os_ = [(jnp.sum(ps[g] * cv, axis=1) + pns[g] * v) / dens[g] for g in gs]
    mss = [_segsum(os_[g] * os_[g], bd) * (1.0 / HEAD_DIM) for g in gs]
    for g in gs:
        on = os_[g] * lax.rsqrt(mss[g] + EPS)
        for kvh in range(N_KV_HEADS):
            sh = (HEAD_DIM * (g - kvh)) % SEG
            x = pltpu.roll(on, sh, axis=1) if sh else on
            chunks[kvh] = jnp.where(seg == g, x, chunks[kvh])
    out = jnp.concatenate(chunks, axis=1) * g_ref[...]
    an_ref[...] = out.astype(an_ref.dtype)


def _attn_sample(z, cache_k, cache_v, layer, tab, sinks_exp, g_attn, bd, bt=16):
    w = cache_k.shape[2]
    row0 = SAMPLE_ROW0 // bt
    kcol, vcol = ATTN_DIM // KV_DIM, ATTN_DIM // KV_DIM + 1
    cache_in = pl.BlockSpec((None, bt, w, KV_DIM), lambda i: (layer, i, 0, 0))
    cache_spec = pl.BlockSpec((bt, w, KV_DIM), lambda i: (i, 0, 0))
    return pl.pallas_call(
        _attn_sample_kernel,
        grid=(DEC_BATCH // bt,),
        in_specs=[pl.BlockSpec((bt, ATTN_DIM), lambda i: (row0 + i, 0)),
                  pl.BlockSpec((bt, KV_DIM), lambda i: (row0 + i, kcol)),
                  pl.BlockSpec((bt, KV_DIM), lambda i: (row0 + i, vcol)),
                  cache_in, cache_in,
                  pl.BlockSpec((8, 256), lambda i: (0, 0)),
                  pl.BlockSpec((GROUP, SEG), lambda i: (0, 0)),
                  pl.BlockSpec((1, ATTN_DIM), lambda i: (0, 0)),
                  pl.BlockSpec((SEG, SEG), lambda i: (0, 0))],
        out_specs=[pl.BlockSpec((bt, ATTN_DIM), lambda i: (i, 0)), cache_spec, cache_spec],
        out_shape=[jax.ShapeDtypeStruct((DEC_BATCH, ATTN_DIM), bf16),
                   jax.ShapeDtypeStruct(cache_k.shape[1:], f32),
                   jax.ShapeDtypeStruct(cache_v.shape[1:], f32)],
        compiler_params=_cparams("parallel"),
    )(z, z, z, cache_k, cache_v, tab, sinks_exp, g_attn, bd)


def _conv_kernel(z1_ref, z2_ref, p1_ref, p2_ref, s0_ref, s1_ref, w_ref, g_ref, bd_ref, c_ref, utail_ref, ns_ref):
    i = pl.program_id(0)

    def split(b1, b2):
        gb = b1[:, :CONV_DIM]
        gc = jnp.concatenate([b1[:, CONV_DIM:], b2[:, :CONV_DIM // 2]], axis=1)
        return gb, gc * b2[:, CONV_DIM // 2:]

    gb, u = split(z1_ref[...], z2_ref[...])
    _, up = split(p1_ref[...], p2_ref[...])

    @pl.when(i < SAMPLE_BLOCK)
    def _():
        utail_ref[...] = u[ROW_BLOCK - 8:]

    @pl.when(i == SAMPLE_BLOCK)
    def _():
        ns_ref[:, :CONV_DIM] = s1_ref[...]
        ns_ref[:, CONV_DIM:] = u

    row = lax.broadcasted_iota(jnp.int32, u.shape, 0)
    u1 = jnp.where(row == 0, up[7:8], pltpu.roll(u, 1, axis=0))
    u2 = jnp.where(row == 0, up[6:7], jnp.where(row == 1, up[7:8], pltpu.roll(u, 2, axis=0)))
    is_sample = i == SAMPLE_BLOCK
    u1 = jnp.where(is_sample, s1_ref[...], u1)
    u2 = jnp.where(is_sample, s0_ref[...], u2)
    w = w_ref[...]
    y = u2 * w[0:1] + u1 * w[1:2] + u * w[2:3]
    t = gb * y
    bd = bd_ref[...]
    g = g_ref[...]
    for c in range(CONV_DIM // SEG):
        tc = t[:, SEG * c:SEG * (c + 1)]
        ms = _segsum(tc * tc, bd) * (1.0 / HEAD_DIM)
        c_ref[:, SEG * c:SEG * (c + 1)] = (tc * lax.rsqrt(ms + EPS) * g[:, SEG * c:SEG * (c + 1)]).astype(c_ref.dtype)


def _conv_prev_block(i):
    last8_of_meta = N_ROWS // 8 - 1
    return jnp.where(i % BLOCKS_PER_SEQ == 0, last8_of_meta, (ROW_BLOCK // 8) * i - 1)


def _conv_mix(z, state_l, conv_w_l, g_conv, bd):
    wide = 3 * CONV_DIM // 2
    return pl.pallas_call(
        _conv_kernel,
        grid=(N_ROW_BLOCKS,),
        in_specs=[pl.BlockSpec((ROW_BLOCK, wide), lambda i: (i, 1)),
                  pl.BlockSpec((ROW_BLOCK, wide), lambda i: (i, 2)),
                  pl.BlockSpec((8, wide), lambda i: (_conv_prev_block(i), 1)),
                  pl.BlockSpec((8, wide), lambda i: (_conv_prev_block(i), 2)),
                  pl.BlockSpec((DEC_BATCH, CONV_DIM), lambda i: (0, 0)),
                  pl.BlockSpec((DEC_BATCH, CONV_DIM), lambda i: (0, 1)),
                  pl.BlockSpec((CONV_WIDTH, CONV_DIM), lambda i: (0, 0)),
                  pl.BlockSpec((1, CONV_DIM), lambda i: (0, 0)),
                  pl.BlockSpec((SEG, SEG), lambda i: (0, 0))],
        out_specs=[pl.BlockSpec((ROW_BLOCK, CONV_DIM), lambda i: (i, 0)),
                   pl.BlockSpec((8, CONV_DIM), lambda i: (_seq_of_step(i), 0)),
                   pl.BlockSpec((DEC_BATCH, 2 * CONV_DIM), lambda i: (0, 0))],
        out_shape=[jax.ShapeDtypeStruct((N_ROWS, CONV_DIM), bf16),
                   jax.ShapeDtypeStruct((BATCH * 8, CONV_DIM), f32),
                   jax.ShapeDtypeStruct((DEC_BATCH, 2 * CONV_DIM), f32)],
        compiler_params=_cparams("arbitrary"),
    )(z, z, z, z, state_l, state_l, conv_w_l, g_conv, bd)


def _outproj_kernel(a_ref, as_ref, c_ref, w_ref, h_ref, g_ref, ho_ref, no_ref, wbf_ref):
    @pl.when(pl.program_id(0) == 0)
    def _():
        wbf_ref[...] = w_ref[...].astype(bf16)

    tm = a_ref.shape[0]
    off = SAMPLE_ROW0 % tm
    a = a_ref[...]
    pieces = [jnp.zeros((off, ATTN_DIM), a.dtype)] if off else []
    pieces.append(as_ref[...])
    if tm - off - DEC_BATCH:
        pieces.append(jnp.zeros((tm - off - DEC_BATCH, ATTN_DIM), a.dtype))
    row = lax.broadcasted_iota(jnp.int32, a.shape, 0)
    is_decode = (pl.program_id(0) == SAMPLE_ROW0 // tm) & (row >= off) & (row < off + DEC_BATCH)
    a = jnp.where(is_decode, jnp.concatenate(pieces, axis=0), a)
    acc = jnp.dot(a, wbf_ref[:ATTN_DIM, :], preferred_element_type=f32)
    acc += jnp.dot(c_ref[...], wbf_ref[ATTN_DIM:, :], preferred_element_type=f32)
    hn = h_ref[...] + acc
    ho_ref[...] = hn
    no_ref[...] = _rms(hn, g_ref[...]).astype(no_ref.dtype)


def _out_proj(an, an_decode, c, w_out, layer, h, g_next, tm=384):
    assert SAMPLE_ROW0 % tm + DEC_BATCH <= tm
    row = lambda i: (i, 0)
    return pl.pallas_call(
        _outproj_kernel,
        grid=(N_ROWS // tm,),
        in_specs=[pl.BlockSpec((tm, ATTN_DIM), row),
                  pl.BlockSpec((DEC_BATCH, ATTN_DIM), lambda i: (0, 0)),
                  pl.BlockSpec((tm, CONV_DIM), row),
                  pl.BlockSpec((None, D_MODEL, D_MODEL), lambda i: (layer, 0, 0), pipeline_mode=pl.Buffered(1)),
                  pl.BlockSpec((tm, D_MODEL), row),
                  pl.BlockSpec((1, D_MODEL), lambda i: (0, 0))],
        out_specs=[pl.BlockSpec((tm, D_MODEL), row), pl.BlockSpec((tm, D_MODEL), row)],
        out_shape=[jax.ShapeDtypeStruct((N_ROWS, D_MODEL), f32),
                   jax.ShapeDtypeStruct((N_ROWS, D_MODEL), bf16)],
        scratch_shapes=[pltpu.VMEM((D_MODEL, D_MODEL), bf16)],
        compiler_params=_cparams("arbitrary"),
    )(an, an_decode, c, w_out, h, g_next)


def _router_kernel(h_ref, g_ref, r_ref, tri_ref, cmb_ref, dest_ref, carry_ref):
    i = pl.program_id(0)
    tm = h_ref.shape[0]

    @pl.when(i == 0)
    def _():
        carry_ref[...] = jnp.zeros_like(carry_ref)

    n = _rms(h_ref[...], g_ref[...])
    rt = r_ref[...]
    idx = lax.broadcasted_iota(jnp.int32, (tm, N_EXPERTS), 1)
    logits = jnp.zeros((tm, N_EXPERTS), f32)
    for e in range(N_EXPERTS):
        prod = n * rt[e:e + 1, :]
        part = prod[:, :128]
        for c in range(1, D_MODEL // 128):
            part = part + prod[:, 128 * c:128 * (c + 1)]
        logits = jnp.where(idx == e, jnp.sum(part, axis=-1, keepdims=True), logits)
    m1 = jnp.max(logits, axis=-1, keepdims=True)
    i1 = jnp.min(jnp.where(logits == m1, idx, N_EXPERTS), axis=-1, keepdims=True)
    rest = jnp.where(idx == i1, -jnp.inf, logits)
    m2 = jnp.max(rest, axis=-1, keepdims=True)
    i2 = jnp.min(jnp.where(rest == m2, idx, N_EXPERTS), axis=-1, keepdims=True)
    e2 = jnp.exp(m2 - m1)
    den = 1.0 + e2
    rowid = i * tm + lax.broadcasted_iota(jnp.int32, logits.shape, 0)
    real = (rowid < TAIL_ROW0) | (rowid >= N_ROWS - N_META)
    sel = ((idx == i1) | (idx == i2)) & real
    cmb_ref[...] = jnp.where(sel, jnp.where(idx == i1, 1.0 / den, e2 / den), 0.0)
    self = jnp.where(sel, 1.0, 0.0)
    incl = jnp.dot(tri_ref[...], self.astype(bf16), preferred_element_type=f32)
    carry = carry_ref[...]
    dest_ref[...] = jnp.where(sel, incl - self + carry, -1.0).astype(jnp.int32)
    carry_ref[...] = carry + incl[tm - 1:tm, :]


def _router(h, g, router, tm=768):
    tri = jnp.asarray(np.tril(np.ones((tm, tm), np.float32)), dtype=bf16)
    return pl.pallas_call(
        _router_kernel,
        grid=(N_ROWS // tm,),
        in_specs=[pl.BlockSpec((tm, D_MODEL), lambda i: (i, 0)),
                  pl.BlockSpec((1, D_MODEL), lambda i: (0, 0)),
                  pl.BlockSpec((N_EXPERTS, D_MODEL), lambda i: (0, 0)),
                  pl.BlockSpec((tm, tm), lambda i: (0, 0))],
        out_specs=[pl.BlockSpec((tm, N_EXPERTS), lambda i: (i, 0)),
                   pl.BlockSpec((tm, N_EXPERTS), lambda i: (i, 0))],
        out_shape=[jax.ShapeDtypeStruct((N_ROWS, N_EXPERTS), f32),
                   jax.ShapeDtypeStruct((N_ROWS, N_EXPERTS), jnp.int32)],
        scratch_shapes=[pltpu.VMEM((1, N_EXPERTS), f32)],
        compiler_params=_cparams("arbitrary"),
    )(h, g, router.T, tri)


MOE_CHUNK = 256
MOE_GATHER = 768
MOE_SUB_SIZES = (640, 704, 736, 768)
MOE_SUBS_PER_VISIT = 3
MOE_VISIT_ROWS = MOE_SUBS_PER_VISIT * MOE_SUB_SIZES[-1]
MOE_CAP = 4 * MOE_VISIT_ROWS
MOE_TF = 256
N_CHUNKS = N_ROWS // MOE_CHUNK
N_GATHER = N_ROWS // MOE_GATHER
N_FSTEPS = D_FF // MOE_TF
MOE_MAX_VISITS = (2 * N_ROWS) // MOE_VISIT_ROWS + N_EXPERTS


def _moe_kernel(vis_e, vis_k, vis_nsub, vis_cls, vis_ok, rlo, xn_ref, dest_ref, w1_ref, w3_ref, w2_ref, y_ref, acc_ref):
    v, p = pl.program_id(0), pl.program_id(1)
    ok = vis_ok[v] == 1
    e = vis_e[v]
    base = vis_k[v] * MOE_VISIT_ROWS

    @pl.when(ok & (p == 0))
    def _():
        y_ref[...] = jnp.zeros_like(y_ref)

    @pl.when(ok & (p < N_GATHER))
    def _():
        lo = jnp.maximum(rlo[e * (N_GATHER + 1) + p] - base, 0)
        hi = jnp.minimum(rlo[e * (N_GATHER + 1) + p + 1] - base, MOE_VISIT_ROWS)

        @pl.when(hi > lo)
        def _():
            dest = dest_ref[...] - base
            win = lax.broadcasted_iota(jnp.int32, (MOE_CHUNK, MOE_GATHER), 0)

            def tile_body(t, carry):
                off = pl.multiple_of(t * MOE_CHUNK, MOE_CHUNK)
                onehot = jnp.where(win == dest - off, 1.0, 0.0).astype(bf16)
                rows = jnp.dot(onehot, xn_ref[...], preferred_element_type=f32)
                y_ref[pl.ds(off, MOE_CHUNK), :] += rows.astype(bf16)
                return carry

            t_first = lo // MOE_CHUNK
            t_last = (hi - 1) // MOE_CHUNK

            @pl.when(t_last - t_first < 2)
            def _():
                off = pl.multiple_of(jnp.minimum(t_first, MOE_VISIT_ROWS // MOE_CHUNK - 2) * MOE_CHUNK, MOE_CHUNK)
                win2 = lax.broadcasted_iota(jnp.int32, (2 * MOE_CHUNK, MOE_GATHER), 0)
                onehot = jnp.where(win2 == dest - off, 1.0, 0.0).astype(bf16)
                rows = jnp.dot(onehot, xn_ref[...], preferred_element_type=f32)
                y_ref[pl.ds(off, 2 * MOE_CHUNK), :] += rows.astype(bf16)

            @pl.when(t_last - t_first >= 2)
            def _():
                lax.fori_loop(t_first, t_last + 1, tile_body, 0)

    @pl.when(ok & (p >= N_GATHER))
    def _():
        f = p - N_GATHER
        w1 = w1_ref[...].astype(bf16)
        w3 = w3_ref[...].astype(bf16)
        w2 = w2_ref[...].astype(bf16)
        for cls, sub in enumerate(MOE_SUB_SIZES):
            @pl.when(vis_cls[v] == cls)
            def _():
                def sub_body(r, carry):
                    rows = pl.ds(pl.multiple_of(r * sub, 16), sub)

                    @pl.when(f == 0)
                    def _():
                        acc_ref[rows, :] = jnp.zeros((sub, D_MODEL), f32)

                    x = y_ref[rows, :]
                    a = jnp.dot(x, w1, preferred_element_type=f32)
                    b = jnp.dot(x, w3, preferred_element_type=f32)
                    act = (a * jax.nn.sigmoid(a) * b).astype(bf16)
                    acc_ref[rows, :] += jnp.dot(act, w2, preferred_element_type=f32)

                    @pl.when(f == N_FSTEPS - 1)
                    def _():
                        y_ref[rows, :] = acc_ref[rows, :].astype(y_ref.dtype)

                    return carry

                lax.fori_loop(0, vis_nsub[v], sub_body, 0)


def _moe_experts(xn, dest_exp, tables, w1, w3, w2, layer_idx):
    def chunk(v, p, ve, vk, vn, vc, vok, rlo):
        return jnp.where(vok[v] == 1, jnp.minimum(p, N_GATHER - 1), N_GATHER - 1)

    def fstep(v, p, ve, vk, vn, vc, vok, rlo):
        return jnp.where(vok[v] == 1, jnp.maximum(p - N_GATHER, 0), N_FSTEPS - 1)

    grid_spec = pltpu.PrefetchScalarGridSpec(
        num_scalar_prefetch=6,
        grid=(MOE_MAX_VISITS, N_GATHER + N_FSTEPS),
        in_specs=[pl.BlockSpec((MOE_GATHER, D_MODEL), lambda v, p, *t: (chunk(v, p, *t), 0)),
                  pl.BlockSpec((None, None, 1, MOE_GATHER), lambda v, p, *t: (t[0][v], chunk(v, p, *t), 0, 0)),
                  pl.BlockSpec((None, None, D_MODEL, MOE_TF), lambda v, p, *t: (layer_idx, t[0][v], 0, fstep(v, p, *t))),
                  pl.BlockSpec((None, None, D_MODEL, MOE_TF), lambda v, p, *t: (layer_idx, t[0][v], 0, fstep(v, p, *t))),
                  pl.BlockSpec((None, None, MOE_TF, D_MODEL), lambda v, p, *t: (layer_idx, t[0][v], fstep(v, p, *t), 0))],
        out_specs=pl.BlockSpec((None, MOE_VISIT_ROWS, D_MODEL), lambda v, p, *t: (t[0][v], t[1][v], 0),
                               pipeline_mode=pl.Buffered(1)),
        scratch_shapes=[pltpu.VMEM((MOE_VISIT_ROWS, D_MODEL), f32)],
    )
    return pl.pallas_call(
        _moe_kernel,
        grid_spec=grid_spec,
        out_shape=jax.ShapeDtypeStruct((N_EXPERTS, MOE_CAP, D_MODEL), bf16),
        compiler_params=_cparams("arbitrary", "arbitrary"),
    )(*tables, xn, dest_exp, w1, w3, w2)


def _combine_kernel(t0, nwin, h_ref, dest_ref, cmb_ref, g_ref, *rest):
    ywin = rest[:2 * N_EXPERTS]
    yp_ref, ys_ref, acc_ref = rest[2 * N_EXPERTS:]
    c = pl.program_id(0)
    acc_ref[...] = h_ref[...]
    dest = dest_ref[...]
    cmb = cmb_ref[...]
    lane = lax.broadcasted_iota(jnp.int32, (MOE_CHUNK, MOE_CHUNK), 1)
    for e in range(N_EXPERTS):
        for w in range(2):
            @pl.when(nwin[e * N_CHUNKS + c] > w)
            def _():
                rel = dest[:, e:e + 1] - MOE_CHUNK * (t0[e * N_CHUNKS + c] + w)
                onehot = jnp.where(rel == lane, 1.0, 0.0).astype(bf16)
                got = jnp.dot(onehot, ywin[2 * e + w][...], preferred_element_type=f32)
                acc_ref[...] += cmb[:, e:e + 1] * got

    n = _rms(acc_ref[...], g_ref[...])
    n_prompt_chunks = N_PROMPT_ROWS // MOE_CHUNK

    @pl.when(c < n_prompt_chunks)
    def _():
        yp_ref[...] = n

    @pl.when(c == n_prompt_chunks)
    def _():
        ys_ref[...] = n[:DEC_BATCH]


def _moe_combine(y_sorted, win_tables, h, dest_tok, cmb, g_final):
    t0, t0c, t1c, nwin = win_tables
    row = lambda c, *t: (c, 0)
    n_prompt_chunks = N_PROMPT_ROWS // MOE_CHUNK
    wins = []
    for e in range(N_EXPERTS):
        wins.append(pl.BlockSpec((None, MOE_CHUNK, D_MODEL), lambda c, a, b, *t, e=e: (e, a[e * N_CHUNKS + c], 0)))
        wins.append(pl.BlockSpec((None, MOE_CHUNK, D_MODEL), lambda c, a, b, *t, e=e: (e, b[e * N_CHUNKS + c], 0)))
    grid_spec = pltpu.PrefetchScalarGridSpec(
        num_scalar_prefetch=4,
        grid=(N_CHUNKS,),
        in_specs=[pl.BlockSpec((MOE_CHUNK, D_MODEL), row),
                  pl.BlockSpec((MOE_CHUNK, N_EXPERTS), row),
                  pl.BlockSpec((MOE_CHUNK, N_EXPERTS), row),
                  pl.BlockSpec((1, D_MODEL), lambda c, *t: (0, 0))] + wins,
        out_specs=[pl.BlockSpec((MOE_CHUNK, D_MODEL), lambda c, *t: (jnp.minimum(c, n_prompt_chunks - 1), 0)),
                   pl.BlockSpec((DEC_BATCH, D_MODEL), lambda c, *t: (0, 0))],
        scratch_shapes=[pltpu.VMEM((MOE_CHUNK, D_MODEL), f32)],
    )

    def body(t0c_ref, t1c_ref, t0_ref, nwin_ref, *refs):
        _combine_kernel(t0_ref, nwin_ref, *refs)

    return pl.pallas_call(
        body,
        grid_spec=grid_spec,
        out_shape=[jax.ShapeDtypeStruct((N_PROMPT_ROWS, D_MODEL), f32),
                   jax.ShapeDtypeStruct((DEC_BATCH, D_MODEL), f32)],
        compiler_params=_cparams("arbitrary"),
    )(t0c, t1c, t0, nwin, h, dest_tok, cmb, g_final, *([y_sorted] * (2 * N_EXPERTS)))


def _routing_tables(dest_tok):
    i32 = jnp.int32
    cnt = (dest_tok >= 0).astype(i32).reshape(N_CHUNKS, MOE_CHUNK, N_EXPERTS).sum(axis=1)
    csum = jnp.cumsum(cnt, axis=0)
    rank_lo = jnp.concatenate([jnp.zeros((1, N_EXPERTS), i32), csum], axis=0).T
    n_e = csum[-1]
    nv = (n_e + MOE_VISIT_ROWS - 1) // MOE_VISIT_ROWS
    cum_nv = jnp.cumsum(nv)
    total = cum_nv[-1]
    v = jnp.arange(MOE_MAX_VISITS, dtype=i32)
    vv = jnp.minimum(v, jnp.maximum(total - 1, 0))
    vis_e = jnp.minimum(jnp.sum((vv[:, None] >= cum_nv[None, :]).astype(i32), axis=1), N_EXPERTS - 1)
    vis_k = vv - (cum_nv - nv)[vis_e]
    vis_ok = (v < total).astype(i32)
    rows = jnp.clip(n_e[vis_e] - vis_k * MOE_VISIT_ROWS, 0, MOE_VISIT_ROWS)
    sizes = jnp.asarray(MOE_SUB_SIZES, i32)
    n_sub = (rows[:, None] + sizes[None, :] - 1) // sizes[None, :]
    padded = jnp.where(n_sub <= MOE_SUBS_PER_VISIT, n_sub * sizes[None, :], 2 * MOE_VISIT_ROWS)
    vis_cls = jnp.argmin(padded, axis=1).astype(i32)
    vis_nsub = jnp.where(vis_ok == 1, jnp.take_along_axis(n_sub, vis_cls[:, None], axis=1)[:, 0], 0)
    t0 = rank_lo[:, :N_CHUNKS] // MOE_CHUNK
    tmax = jnp.maximum(nv * (MOE_VISIT_ROWS // MOE_CHUNK) - 1, 0)[:, None]
    moe_tables = (vis_e, vis_k, vis_nsub.astype(i32), vis_cls, vis_ok,
                  rank_lo[:, ::MOE_GATHER // MOE_CHUNK].reshape(-1))
    t_last = (rank_lo[:, 1:] - 1) // MOE_CHUNK
    nwin = jnp.where(cnt.T > 0, t_last - t0 + 1, 0)
    chunk_ids = jnp.arange(N_CHUNKS, dtype=i32)[None, :]
    last_two = lax.cummax(jnp.where(nwin == 2, chunk_ids, 0), axis=1)
    t1c = jnp.take_along_axis(jnp.minimum(t0 + 1, tmax), last_two, axis=1)
    win_tables = (t0.reshape(-1), jnp.minimum(t0, tmax).reshape(-1), t1c.reshape(-1), nwin.reshape(-1))
    return moe_tables, win_tables


def _ffn_kernel(x_ref, w1_ref, w3_ref, w2_ref, h_ref, g_ref, ho_ref, no_ref):
    f = pl.program_id(1)

    @pl.when(f == 0)
    def _():
        ho_ref[...] = h_ref[...]

    x = x_ref[...]
    a = jnp.dot(x, w1_ref[...].astype(bf16), preferred_element_type=f32)
    b = jnp.dot(x, w3_ref[...].astype(bf16), preferred_element_type=f32)
    act = (a * jax.nn.sigmoid(a) * b).astype(bf16)
    ho_ref[...] += jnp.dot(act, w2_ref[...].astype(bf16), preferred_element_type=f32)

    @pl.when(f == pl.num_programs(1) - 1)
    def _():
        no_ref[...] = _rms(ho_ref[...], g_ref[...]).astype(no_ref.dtype)


def _ffn(xn, w1, w3, w2, layer_idx, h, g_next, tm=1408, tf=256):
    row = lambda i, f: (i, 0)
    once = pl.Buffered(1)
    return pl.pallas_call(
        _ffn_kernel,
        grid=(N_ROWS // tm, D_FF // tf),
        in_specs=[pl.BlockSpec((tm, D_MODEL), row, pipeline_mode=once),
                  pl.BlockSpec((None, D_MODEL, tf), lambda i, f: (layer_idx, 0, f)),
                  pl.BlockSpec((None, D_MODEL, tf), lambda i, f: (layer_idx, 0, f)),
                  pl.BlockSpec((None, tf, D_MODEL), lambda i, f: (layer_idx, f, 0)),
                  pl.BlockSpec((tm, D_MODEL), row, pipeline_mode=once),
                  pl.BlockSpec((1, D_MODEL), lambda i, f: (0, 0))],
        out_specs=[pl.BlockSpec((tm, D_MODEL), row, pipeline_mode=once),
                   pl.BlockSpec((tm, D_MODEL), row, pipeline_mode=once)],
        out_shape=[jax.ShapeDtypeStruct((N_ROWS, D_MODEL), f32),
                   jax.ShapeDtypeStruct((N_ROWS, D_MODEL), bf16)],
        compiler_params=_cparams("parallel", "arbitrary"),
    )(xn, w1, w3, w2, h, g_next)


def _rope_patterns(pos):
    half = ROT_DIM // 2
    inv_freq = jnp.power(ROPE_THETA, -jnp.arange(half, dtype=f32) * 2.0 / ROT_DIM)
    ang = pos.astype(f32)[:, None] * inv_freq[None, :]
    cos, sin = jnp.cos(ang), jnp.sin(ang)
    n = pos.shape[0]
    cos_h = jnp.concatenate([cos, cos, jnp.ones((n, HEAD_DIM - ROT_DIM), f32)], axis=1)
    sin_h = jnp.concatenate([-sin, sin, jnp.zeros((n, HEAD_DIM - ROT_DIM), f32)], axis=1)
    return jnp.concatenate([cos_h, cos_h, sin_h, sin_h], axis=1)


def _block_diag_ones():
    i = np.arange(SEG) // HEAD_DIM
    return jnp.asarray((i[:, None] == i[None, :]).astype(np.float32), dtype=bf16)


def kernel(x_prompt, x_sample, cache_k, cache_v, state_conv, meta_tokens, g_mix, w_in, conv_w, attn_sinks,
           g_attn_out, g_conv_out, w_out, g_ffn, dense_w1, dense_w3, dense_w2, moe_router, moe_w1, moe_w3,
           moe_w2, g_final):
    w_cache = cache_k.shape[2]
    tail = jnp.concatenate([x_sample.reshape(DEC_BATCH, D_MODEL), jnp.zeros((META_ROW_IN_BLOCK, D_MODEL), f32),
                            meta_tokens.astype(f32)], axis=0)

    prompt_pos = N_META + jnp.arange(SEQ, dtype=jnp.int32)
    tail_pos = jnp.maximum(jnp.arange(ROW_BLOCK, dtype=jnp.int32) - META_ROW_IN_BLOCK, 0)
    rope_tab = _rope_patterns(jnp.concatenate([prompt_pos, tail_pos]))
    sample_tab = _rope_patterns(jnp.full((8,), PAST_LEN, jnp.int32))
    bd = _block_diag_ones()
    cache_k = cache_k.reshape(DEPTH, DEC_BATCH, w_cache, KV_DIM)
    cache_v = cache_v.reshape(DEPTH, DEC_BATCH, w_cache, KV_DIM)

    h, xn = _norm_rows(x_prompt.reshape(N_PROMPT_ROWS, D_MODEL), tail, g_mix[0:1])
    nk_p, nv_p, nc_p, nk_s, nv_s, nc_s = [], [], [], [], [], []
    for l in range(DEPTH):
        z = _in_proj(xn, w_in, l)
        g_attn = g_attn_out[l:l + 1]
        an, nk_prompt, nv_prompt = _attn_prompt(z, attn_sinks[l], rope_tab, g_attn)
        sinks_exp = jnp.repeat(attn_sinks[l].reshape(N_KV_HEADS, GROUP).T, HEAD_DIM, axis=1)
        an_decode, nk, nv = _attn_sample(z, cache_k, cache_v, l, sample_tab, sinks_exp, g_attn, bd)
        state_l = state_conv[l].reshape(DEC_BATCH, (CONV_WIDTH - 1) * CONV_DIM)
        c, u_tail, conv_state = _conv_mix(z, state_l, conv_w[l], g_conv_out[l:l + 1], bd)
        h, xn = _out_proj(an, an_decode, c, w_out, l, h, g_ffn[l:l + 1])

        j = l // 2
        if l % 2 == 0:
            h, xn = _ffn(xn, dense_w1, dense_w3, dense_w2, j, h, g_mix[l + 1:l + 2])
        else:
            cmb, dest_tok = _router(h, g_ffn[l:l + 1], moe_router[j])
            moe_tables, win_tables = _routing_tables(dest_tok)
            dest_exp = dest_tok.T.reshape(N_EXPERTS, N_GATHER, 1, MOE_GATHER)
            y_sorted = _moe_experts(xn, dest_exp, moe_tables, moe_w1, moe_w3, moe_w2, j)
            y_prompt, y_sample = _moe_combine(y_sorted, win_tables, h, dest_tok, cmb, g_final[None])

        nk_p.append(nk_prompt[:BATCH * WINDOW].reshape(BATCH, WINDOW, N_KV_HEADS, HEAD_DIM))
        nv_p.append(nv_prompt[:BATCH * WINDOW].reshape(BATCH, WINDOW, N_KV_HEADS, HEAD_DIM))
        nc_p.append(u_tail.reshape(BATCH, 8, CONV_DIM)[:, 8 - (CONV_WIDTH - 1):])
        nk_s.append(nk.reshape(DEC_BATCH, w_cache, N_KV_HEADS, HEAD_DIM))
        nv_s.append(nv.reshape(DEC_BATCH, w_cache, N_KV_HEADS, HEAD_DIM))
        nc_s.append(conv_state.reshape(DEC_BATCH, CONV_WIDTH - 1, CONV_DIM))

    return (y_prompt.reshape(BATCH, SEQ, D_MODEL), y_sample.reshape(DEC_BATCH, 1, D_MODEL),
            jnp.stack(nk_p), jnp.stack(nv_p), jnp.stack(nc_p), jnp.stack(nk_s), jnp.stack(nv_s), jnp.stack(nc_s))
```

```python
import functools

import jax
import jax.numpy as jnp
import numpy as np
from jax import lax
from jax.experimental import pallas as pl
from jax.experimental.pallas import tpu as pltpu

D_MODEL = 2048
BATCH = 4
SEQ = 2048
DEPTH = 2
DEC_BATCH = 128
PAST_LEN = 8192
HEAD_DIM = 64
ATTN_DIM = D_MODEL // 2
N_Q_HEADS = ATTN_DIM // HEAD_DIM
N_KV_HEADS = N_Q_HEADS // 4
GROUP = N_Q_HEADS // N_KV_HEADS
KV_DIM = N_KV_HEADS * HEAD_DIM
CONV_DIM = D_MODEL - ATTN_DIM
CONV_WIDTH = 3
IN_DIM = ATTN_DIM + 2 * KV_DIM + 3 * CONV_DIM
WINDOW = 128
ROPE_THETA = 500000.0
ROT_DIM = HEAD_DIM // 4
N_META = 16
D_FF = 7 * D_MODEL // 2
N_EXPERTS = 8
EPS = 1e-5

ROW_BLOCK = 128
N_PROMPT_ROWS = BATCH * SEQ
SAMPLE_ROW0 = N_PROMPT_ROWS
TAIL_ROW0 = SAMPLE_ROW0 + DEC_BATCH
N_ROWS = TAIL_ROW0 + ROW_BLOCK
N_ROW_BLOCKS = N_ROWS // ROW_BLOCK
BLOCKS_PER_SEQ = SEQ // ROW_BLOCK
SAMPLE_BLOCK = SAMPLE_ROW0 // ROW_BLOCK
META_BLOCK = TAIL_ROW0 // ROW_BLOCK
META_ROW_IN_BLOCK = ROW_BLOCK - N_META
SEG = 256

VMEM_LIMIT_V7X = 56 * 1024 * 1024

bf16 = jnp.bfloat16
f32 = jnp.float32


def _cparams(*sem):
    return pltpu.CompilerParams(dimension_semantics=sem, vmem_limit_bytes=VMEM_LIMIT_V7X)


def _rms(x, g):
    ms = jnp.mean(x * x, axis=-1, keepdims=True)
    return (x * lax.rsqrt(ms + EPS)) * g


def _segsum(x, bd):
    x1 = x.astype(bf16)
    r1 = x - x1.astype(f32)
    x2 = r1.astype(bf16)
    x3 = (r1 - x2.astype(f32)).astype(bf16)
    d = functools.partial(jnp.dot, preferred_element_type=f32)
    return d(x1, bd) + d(x2, bd) + d(x3, bd)


def _rope(x, cos_t, sin_t):
    n = x.shape[-1]
    lane = lax.broadcasted_iota(jnp.int32, x.shape, x.ndim - 1) % HEAD_DIM
    nxt = pltpu.roll(x, n - ROT_DIM // 2, axis=x.ndim - 1)
    prv = pltpu.roll(x, ROT_DIM // 2, axis=x.ndim - 1)
    return x * cos_t + jnp.where(lane < ROT_DIM // 2, nxt, prv) * sin_t


def _norm_kernel(xp_ref, tail_ref, g_ref, h_ref, o_ref):
    is_tail = pl.program_id(0) == pl.num_programs(0) - 1
    x = jnp.where(is_tail, tail_ref[...], xp_ref[...])
    h_ref[...] = x
    o_ref[...] = _rms(x, g_ref[...]).astype(o_ref.dtype)


def _norm_rows(x_prompt_rows, tail_rows, g):
    tm = tail_rows.shape[0]
    n_prompt_tiles = N_PROMPT_ROWS // tm
    row = lambda i: (i, 0)
    return pl.pallas_call(
        _norm_kernel,
        grid=(N_ROWS // tm,),
        in_specs=[pl.BlockSpec((tm, D_MODEL), lambda i: (jnp.minimum(i, n_prompt_tiles - 1), 0)),
                  pl.BlockSpec((tm, D_MODEL), lambda i: (0, 0)),
                  pl.BlockSpec((1, D_MODEL), lambda i: (0, 0))],
        out_specs=[pl.BlockSpec((tm, D_MODEL), row), pl.BlockSpec((tm, D_MODEL), row)],
        out_shape=[jax.ShapeDtypeStruct((N_ROWS, D_MODEL), f32),
                   jax.ShapeDtypeStruct((N_ROWS, D_MODEL), bf16)],
        compiler_params=_cparams("parallel"),
    )(x_prompt_rows, tail_rows, g)


def _inproj_kernel(x_ref, w_ref, o_ref, wbf_ref):
    @pl.when(pl.program_id(1) == 0)
    def _():
        wbf_ref[...] = w_ref[...].astype(bf16)

    o_ref[...] = jnp.dot(x_ref[...], wbf_ref[...], preferred_element_type=f32)


def _in_proj(xn, w_in, layer, tm=768, tn=1536):
    return pl.pallas_call(
        _inproj_kernel,
        grid=(IN_DIM // tn, N_ROWS // tm),
        in_specs=[pl.BlockSpec((tm, D_MODEL), lambda j, i: (i, 0)),
                  pl.BlockSpec((None, D_MODEL, tn), lambda j, i: (layer, 0, j))],
        out_specs=pl.BlockSpec((tm, tn), lambda j, i: (i, j)),
        out_shape=jax.ShapeDtypeStruct((N_ROWS, IN_DIM), f32),
        scratch_shapes=[pltpu.VMEM((D_MODEL, tn), bf16)],
        compiler_params=_cparams("parallel", "arbitrary"),
    )(xn, w_in)


def _mixer_kernel(sink_ref, q_ref, kc_ref, vc_ref, kp_ref, vp_ref, tc_ref, tp_ref, g_ref,
                  z1_ref, z2_ref, p1_ref, p2_ref, s0_ref, s1_ref, cw_ref, gc_ref, bd_ref,
                  an_ref, nk_ref, nv_ref, c_ref, utail_ref, ns_ref):
    s = pl.program_id(0)
    _conv_block(s == N_ROW_BLOCKS - 1, z1_ref, z2_ref, p1_ref, p2_ref, s0_ref, s1_ref, cw_ref, gc_ref, bd_ref,
                c_ref, utail_ref, ns_ref)
    is_meta = s == N_ROW_BLOCKS - 2
    j = s % BLOCKS_PER_SEQ
    tc = tc_ref[...]
    tp = tp_ref[...]
    q = _rope(q_ref[...] * (HEAD_DIM ** -0.5), jnp.tile(tc[:, :128], (1, 8)), jnp.tile(tc[:, 128:], (1, 8)))
    kc = _rope(kc_ref[...], jnp.tile(tc[:, :128], (1, 2)), jnp.tile(tc[:, 128:], (1, 2)))
    kp = _rope(kp_ref[...], jnp.tile(tp[:, :128], (1, 2)), jnp.tile(tp[:, 128:], (1, 2)))
    nk_ref[...] = kc
    nv_ref[...] = vc_ref[...]

    kall = jnp.concatenate([kp, kc], axis=0).astype(bf16)
    vall = jnp.concatenate([vp_ref[...], vc_ref[...]], axis=0)

    row = lax.broadcasted_iota(jnp.int32, (ROW_BLOCK, 2 * ROW_BLOCK), 0)
    col = lax.broadcasted_iota(jnp.int32, (ROW_BLOCK, 2 * ROW_BLOCK), 1)
    lo_prev = jnp.where(is_meta, ROW_BLOCK, jnp.where(j == 0, META_ROW_IN_BLOCK, 0))
    lo_cur = jnp.where(is_meta, META_ROW_IN_BLOCK, 0)
    ccur = col - ROW_BLOCK
    valid = (((col < ROW_BLOCK) & (col >= row) & (col >= lo_prev))
             | ((ccur >= 0) & (ccur <= row) & (ccur >= lo_cur)))
    bias = jnp.where(valid, 0.0, -jnp.inf)
    bias4 = jnp.concatenate([bias] * GROUP, axis=0)
    half = lax.broadcasted_iota(jnp.int32, (ROW_BLOCK, 128), 1) // HEAD_DIM
    half4 = lax.broadcasted_iota(jnp.int32, (GROUP * ROW_BLOCK, 128), 1) // HEAD_DIM
    halfkv = lax.broadcasted_iota(jnp.int32, (2 * ROW_BLOCK, 128), 1) // HEAD_DIM
    g_all = g_ref[...]

    kvs = range(N_KV_HEADS)
    q4s, sinks4, kpairs, vpairs = [], [], [], []
    for kvh in kvs:
        kside = kvh % 2
        kpairs.append(kall[:, 128 * (kvh // 2):128 * (kvh // 2) + 128])
        vpairs.append(jnp.where(halfkv == kside, vall[:, 128 * (kvh // 2):128 * (kvh // 2) + 128], 1.0).astype(bf16))
        qs, sinks = [], []
        for g in range(GROUP):
            h = GROUP * kvh + g
            x = q[:, 128 * (h // 2):128 * (h // 2) + 128]
            if h % 2 != kside:
                x = pltpu.roll(x, HEAD_DIM, axis=1)
            qs.append(jnp.where(half == kside, x, 0.0))
            sinks.append(jnp.full((ROW_BLOCK, 1), sink_ref[h], f32))
        q4s.append(jnp.concatenate(qs, axis=0).astype(bf16))
        sinks4.append(jnp.concatenate(sinks, axis=0))
    nt = (((1,), (1,)), ((), ()))
    scs = [lax.dot_general(q4s[k], kpairs[k], nt, preferred_element_type=f32) + bias4 for k in kvs]
    mxs = [jnp.maximum(jnp.max(scs[k], axis=-1, keepdims=True), sinks4[k]) for k in kvs]
    ps = [jnp.exp(scs[k] - mxs[k]).astype(bf16) for k in kvs]
    ovs = [jnp.dot(ps[k], vpairs[k], preferred_element_type=f32) for k in kvs]
    dens = [pltpu.roll(ovs[k], HEAD_DIM, axis=1) + jnp.exp(sinks4[k] - mxs[k]) for k in kvs]
    os_ = [ovs[k] * (1.0 / dens[k]) for k in kvs]
    mss = [jnp.sum(jnp.where(half4 == k % 2, os_[k] * os_[k], 0.0), axis=-1, keepdims=True) * (1.0 / HEAD_DIM)
           for k in kvs]
    ons = [os_[k] * lax.rsqrt(mss[k] + EPS) for k in kvs]
    for kvh in kvs:
        for pair in range(2):
            parts = []
            for side in range(2):
                g = 2 * pair + side
                x = ons[kvh][ROW_BLOCK * g:ROW_BLOCK * (g + 1)]
                if side != kvh % 2:
                    x = pltpu.roll(x, HEAD_DIM, axis=1)
                parts.append(x)
            blk = 2 * kvh + pair
            out = jnp.where(half == 0, parts[0], parts[1]) * g_all[:, 128 * blk:128 * blk + 128]
            an_ref[:, 128 * blk:128 * blk + 128] = out.astype(an_ref.dtype)


def _cur_block(s):
    return jnp.where(s == N_ROW_BLOCKS - 2, META_BLOCK, jnp.where(s == N_ROW_BLOCKS - 1, SAMPLE_BLOCK, s))


def _prev_block(s):
    return jnp.where((s % BLOCKS_PER_SEQ == 0) | (s == N_ROW_BLOCKS - 2), META_BLOCK, s - 1)


def _cur_tab(s):
    return jnp.where(s == N_ROW_BLOCKS - 2, BLOCKS_PER_SEQ, s % BLOCKS_PER_SEQ)


def _prev_tab(s):
    return jnp.where((s % BLOCKS_PER_SEQ == 0) | (s == N_ROW_BLOCKS - 2), BLOCKS_PER_SEQ, s % BLOCKS_PER_SEQ - 1)


def _mixer(z, sinks, rope_tab, g_attn, state_l, conv_w_l, g_conv, bd):
    kcol, vcol = ATTN_DIM // KV_DIM, ATTN_DIM // KV_DIM + 1
    wide = 3 * CONV_DIM // 2
    cur = lambda s: (_cur_block(s), 0)
    return pl.pallas_call(
        _mixer_kernel,
        grid=(N_ROW_BLOCKS,),
        in_specs=[pl.BlockSpec(memory_space=pltpu.SMEM),
                  pl.BlockSpec((ROW_BLOCK, ATTN_DIM), cur),
                  pl.BlockSpec((ROW_BLOCK, KV_DIM), lambda s: (_cur_block(s), kcol)),
                  pl.BlockSpec((ROW_BLOCK, KV_DIM), lambda s: (_cur_block(s), vcol)),
                  pl.BlockSpec((ROW_BLOCK, KV_DIM), lambda s: (_prev_block(s), kcol)),
                  pl.BlockSpec((ROW_BLOCK, KV_DIM), lambda s: (_prev_block(s), vcol)),
                  pl.BlockSpec((ROW_BLOCK, 256), lambda s: (_cur_tab(s), 0)),
                  pl.BlockSpec((ROW_BLOCK, 256), lambda s: (_prev_tab(s), 0)),
                  pl.BlockSpec((1, ATTN_DIM), lambda s: (0, 0)),
                  pl.BlockSpec((ROW_BLOCK, wide), lambda s: (_cur_block(s), 1)),
                  pl.BlockSpec((ROW_BLOCK, wide), lambda s: (_cur_block(s), 2)),
                  pl.BlockSpec((8, wide), lambda s: (_conv_prev_block(_cur_block(s)), 1)),
                  pl.BlockSpec((8, wide), lambda s: (_conv_prev_block(_cur_block(s)), 2)),
                  pl.BlockSpec((DEC_BATCH, CONV_DIM), lambda s: (0, 0)),
                  pl.BlockSpec((DEC_BATCH, CONV_DIM), lambda s: (0, 1)),
                  pl.BlockSpec((CONV_WIDTH, CONV_DIM), lambda s: (0, 0)),
                  pl.BlockSpec((1, CONV_DIM), lambda s: (0, 0)),
                  pl.BlockSpec((SEG, SEG), lambda s: (0, 0))],
        out_specs=[pl.BlockSpec((ROW_BLOCK, ATTN_DIM), cur),
                   pl.BlockSpec((WINDOW, KV_DIM), lambda s: (s // BLOCKS_PER_SEQ, 0)),
                   pl.BlockSpec((WINDOW, KV_DIM), lambda s: (s // BLOCKS_PER_SEQ, 0)),
                   pl.BlockSpec((ROW_BLOCK, CONV_DIM), cur),
                   pl.BlockSpec((8, CONV_DIM), lambda s: (s // BLOCKS_PER_SEQ, 0)),
                   pl.BlockSpec((DEC_BATCH, 2 * CONV_DIM), lambda s: (0, 0))],
        out_shape=[jax.ShapeDtypeStruct((N_ROWS, ATTN_DIM), bf16),
                   jax.ShapeDtypeStruct(((BATCH + 1) * WINDOW, KV_DIM), f32),
                   jax.ShapeDtypeStruct(((BATCH + 1) * WINDOW, KV_DIM), f32),
                   jax.ShapeDtypeStruct((N_ROWS, CONV_DIM), bf16),
                   jax.ShapeDtypeStruct(((BATCH + 1) * 8, CONV_DIM), f32),
                   jax.ShapeDtypeStruct((DEC_BATCH, 2 * CONV_DIM), f32)],
        compiler_params=_cparams("arbitrary"),
    )(sinks, z, z, z, z, z, rope_tab, rope_tab, g_attn, z, z, z, z, state_l, state_l, conv_w_l, g_conv, bd)


def _attn_sample_kernel(q_ref, k_ref, v_ref, ck_ref, cv_ref, tab_ref, sink_ref, g_ref, bd_ref,
                        an_ref, nk_ref, nv_ref):
    bt = q_ref.shape[0]
    w = ck_ref.shape[1]
    tab = tab_ref[...]
    q = _rope(q_ref[...], jnp.tile(tab[0:1, :128], (1, 8)), jnp.tile(tab[0:1, 128:], (1, 8)))
    k = _rope(k_ref[...], jnp.tile(tab[0:1, :128], (1, 2)), jnp.tile(tab[0:1, 128:], (1, 2)))
    v = v_ref[...]
    ck = ck_ref[...]
    cv = cv_ref[...]
    bd = bd_ref[...]

    last = lax.broadcasted_iota(jnp.int32, (w, KV_DIM), 0) == w - 1
    for b in range(bt):
        nk_ref[b] = jnp.where(last, k[b:b + 1], pltpu.roll(ck[b], w - 1, axis=0))
        nv_ref[b] = jnp.where(last, v[b:b + 1], pltpu.roll(cv[b], w - 1, axis=0))

    seg = lax.broadcasted_iota(jnp.int32, (bt, SEG), 1) // HEAD_DIM
    scale = HEAD_DIM ** -0.5
    chunks = [jnp.zeros((bt, SEG), f32) for _ in range(N_KV_HEADS)]
    gs = range(GROUP)
    qgs = []
    for g in gs:
        qg = jnp.zeros((bt, SEG), f32)
        for kvh in range(N_KV_HEADS):
            x = q[:, SEG * kvh:SEG * (kvh + 1)]
            sh = (HEAD_DIM * (kvh - g)) % SEG
            if sh:
                x = pltpu.roll(x, sh, axis=1)
            qg = jnp.where(seg == kvh, x, qg)
        qgs.append(qg)
    scs = [(_segsum((ck * qgs[g][:, None, :]).reshape(bt * w, SEG), bd) * scale).reshape(bt, w, SEG) for g in gs]
    sns = [_segsum(qgs[g] * k, bd) * scale for g in gs]
    sinks = [sink_ref[g:g + 1, :] for g in gs]
    mxs = [jnp.maximum(jnp.maximum(jnp.max(scs[g], axis=1), sns[g]), sinks[g]) for g in gs]
    ps = [jnp.exp(scs[g] - mxs[g][:, None, :]) for g in gs]
    pns = [jnp.exp(sns[g] - mxs[g]) for g in gs]
    dens = [jnp.sum(ps[g], axis=1) + pns[g] + jnp.exp(sinks[g] - mxs[g]) for g in gs]
    os_ = [(jnp.sum(ps[g] * cv, axis=1) + pns[g] * v) / dens[g] for g in gs]
    mss = [_segsum(os_[g] * os_[g], bd) * (1.0 / HEAD_DIM) for g in gs]
    for g in gs:
        on = os_[g] * lax.rsqrt(mss[g] + EPS)
        for kvh in range(N_KV_HEADS):
            sh = (HEAD_DIM * (g - kvh)) % SEG
            x = pltpu.roll(on, sh, axis=1) if sh else on
            chunks[kvh] = jnp.where(seg == g, x, chunks[kvh])
    out = jnp.concatenate(chunks, axis=1) * g_ref[...]
    an_ref[...] = out.astype(an_ref.dtype)


def _attn_sample(z, cache_k, cache_v, layer, tab, sinks_exp, g_attn, bd, bt=16):
    w = cache_k.shape[2]
    row0 = SAMPLE_ROW0 // bt
    kcol, vcol = ATTN_DIM // KV_DIM, ATTN_DIM // KV_DIM + 1
    cache_in = pl.BlockSpec((None, bt, w, KV_DIM), lambda i: (layer, i, 0, 0))
    cache_spec = pl.BlockSpec((bt, w, KV_DIM), lambda i: (i, 0, 0))
    return pl.pallas_call(
        _attn_sample_kernel,
        grid=(DEC_BATCH // bt,),
        in_specs=[pl.BlockSpec((bt, ATTN_DIM), lambda i: (row0 + i, 0)),
                  pl.BlockSpec((bt, KV_DIM), lambda i: (row0 + i, kcol)),
                  pl.BlockSpec((bt, KV_DIM), lambda i: (row0 + i, vcol)),
                  cache_in, cache_in,
                  pl.BlockSpec((8, 256), lambda i: (0, 0)),
                  pl.BlockSpec((GROUP, SEG), lambda i: (0, 0)),
                  pl.BlockSpec((1, ATTN_DIM), lambda i: (0, 0)),
                  pl.BlockSpec((SEG, SEG), lambda i: (0, 0))],
        out_specs=[pl.BlockSpec((bt, ATTN_DIM), lambda i: (i, 0)), cache_spec, cache_spec],
        out_shape=[jax.ShapeDtypeStruct((DEC_BATCH, ATTN_DIM), bf16),
                   jax.ShapeDtypeStruct(cache_k.shape[1:], f32),
                   jax.ShapeDtypeStruct(cache_v.shape[1:], f32)],
        compiler_params=_cparams("parallel"),
    )(z, z, z, cache_k, cache_v, tab, sinks_exp, g_attn, bd)


def _conv_block(is_sample, z1_ref, z2_ref, p1_ref, p2_ref, s0_ref, s1_ref, w_ref, g_ref, bd_ref,
                c_ref, utail_ref, ns_ref):
    def split(b1, b2):
        gb = b1[:, :CONV_DIM]
        gc = jnp.concatenate([b1[:, CONV_DIM:], b2[:, :CONV_DIM // 2]], axis=1)
        return gb, gc * b2[:, CONV_DIM // 2:]

    gb, u = split(z1_ref[...], z2_ref[...])
    _, up = split(p1_ref[...], p2_ref[...])
    utail_ref[...] = u[ROW_BLOCK - 8:]
    ns_ref[:, :CONV_DIM] = s1_ref[...]
    ns_ref[:, CONV_DIM:] = u

    row = lax.broadcasted_iota(jnp.int32, u.shape, 0)
    u1 = jnp.where(row == 0, up[7:8], pltpu.roll(u, 1, axis=0))
    u2 = jnp.where(row == 0, up[6:7], jnp.where(row == 1, up[7:8], pltpu.roll(u, 2, axis=0)))
    u1 = jnp.where(is_sample, s1_ref[...], u1)
    u2 = jnp.where(is_sample, s0_ref[...], u2)
    w = w_ref[...]
    y = u2 * w[0:1] + u1 * w[1:2] + u * w[2:3]
    t = gb * y
    bd = bd_ref[...]
    g = g_ref[...]
    cols = [slice(SEG * c, SEG * (c + 1)) for c in range(CONV_DIM // SEG)]
    mss = [_segsum(t[:, cs] * t[:, cs], bd) * (1.0 / HEAD_DIM) for cs in cols]
    for cs, ms in zip(cols, mss):
        c_ref[:, cs] = (t[:, cs] * lax.rsqrt(ms + EPS) * g[:, cs]).astype(c_ref.dtype)


def _conv_prev_block(i):
    last8_of_meta = N_ROWS // 8 - 1
    return jnp.where(i % BLOCKS_PER_SEQ == 0, last8_of_meta, (ROW_BLOCK // 8) * i - 1)


def _outproj_kernel(a_ref, as_ref, c_ref, w_ref, h_ref, g_ref, ho_ref, no_ref, wbf_ref):
    @pl.when(pl.program_id(0) == 0)
    def _():
        wbf_ref[...] = w_ref[...].astype(bf16)

    tm = a_ref.shape[0]
    off = SAMPLE_ROW0 % tm
    a = a_ref[...]
    pieces = [jnp.zeros((off, ATTN_DIM), a.dtype)] if off else []
    pieces.append(as_ref[...])
    if tm - off - DEC_BATCH:
        pieces.append(jnp.zeros((tm - off - DEC_BATCH, ATTN_DIM), a.dtype))
    row = lax.broadcasted_iota(jnp.int32, a.shape, 0)
    is_decode = (pl.program_id(0) == SAMPLE_ROW0 // tm) & (row >= off) & (row < off + DEC_BATCH)
    a = jnp.where(is_decode, jnp.concatenate(pieces, axis=0), a)
    acc = jnp.dot(a, wbf_ref[:ATTN_DIM, :], preferred_element_type=f32)
    acc += jnp.dot(c_ref[...], wbf_ref[ATTN_DIM:, :], preferred_element_type=f32)
    hn = h_ref[...] + acc
    ho_ref[...] = hn
    no_ref[...] = _rms(hn, g_ref[...]).astype(no_ref.dtype)


def _out_proj(an, an_decode, c, w_out, layer, h, g_next, tm=384):
    assert SAMPLE_ROW0 % tm + DEC_BATCH <= tm
    row = lambda i: (i, 0)
    return pl.pallas_call(
        _outproj_kernel,
        grid=(N_ROWS // tm,),
        in_specs=[pl.BlockSpec((tm, ATTN_DIM), row),
                  pl.BlockSpec((DEC_BATCH, ATTN_DIM), lambda i: (0, 0)),
                  pl.BlockSpec((tm, CONV_DIM), row),
                  pl.BlockSpec((None, D_MODEL, D_MODEL), lambda i: (layer, 0, 0), pipeline_mode=pl.Buffered(1)),
                  pl.BlockSpec((tm, D_MODEL), row),
                  pl.BlockSpec((1, D_MODEL), lambda i: (0, 0))],
        out_specs=[pl.BlockSpec((tm, D_MODEL), row), pl.BlockSpec((tm, D_MODEL), row)],
        out_shape=[jax.ShapeDtypeStruct((N_ROWS, D_MODEL), f32),
                   jax.ShapeDtypeStruct((N_ROWS, D_MODEL), bf16)],
        scratch_shapes=[pltpu.VMEM((D_MODEL, D_MODEL), bf16)],
        compiler_params=_cparams("arbitrary"),
    )(an, an_decode, c, w_out, h, g_next)


def _router_kernel(h_ref, g_ref, r_ref, tri_ref, cmb_ref, dest_ref, carry_ref):
    i = pl.program_id(0)
    tm = h_ref.shape[0]

    @pl.when(i == 0)
    def _():
        carry_ref[...] = jnp.zeros_like(carry_ref)

    n = _rms(h_ref[...], g_ref[...])
    rt = r_ref[...]
    idx = lax.broadcasted_iota(jnp.int32, (tm, N_EXPERTS), 1)
    logits = jnp.zeros((tm, N_EXPERTS), f32)
    for e in range(N_EXPERTS):
        prod = n * rt[e:e + 1, :]
        part = prod[:, :128]
        for c in range(1, D_MODEL // 128):
            part = part + prod[:, 128 * c:128 * (c + 1)]
        logits = jnp.where(idx == e, jnp.sum(part, axis=-1, keepdims=True), logits)
    m1 = jnp.max(logits, axis=-1, keepdims=True)
    i1 = jnp.min(jnp.where(logits == m1, idx, N_EXPERTS), axis=-1, keepdims=True)
    rest = jnp.where(idx == i1, -jnp.inf, logits)
    m2 = jnp.max(rest, axis=-1, keepdims=True)
    i2 = jnp.min(jnp.where(rest == m2, idx, N_EXPERTS), axis=-1, keepdims=True)
    e2 = jnp.exp(m2 - m1)
    den = 1.0 + e2
    rowid = i * tm + lax.broadcasted_iota(jnp.int32, logits.shape, 0)
    real = (rowid < TAIL_ROW0) | (rowid >= N_ROWS - N_META)
    sel = ((idx == i1) | (idx == i2)) & real
    cmb_ref[...] = jnp.where(sel, jnp.where(idx == i1, 1.0 / den, e2 / den), 0.0)
    self = jnp.where(sel, 1.0, 0.0)
    incl = jnp.dot(tri_ref[...], self.astype(bf16), preferred_element_type=f32)
    carry = carry_ref[...]
    dest_ref[...] = jnp.where(sel, incl - self + carry, -1.0).astype(jnp.int32)
    carry_ref[...] = carry + incl[tm - 1:tm, :]


def _router(h, g, router, tm=768):
    tri = jnp.asarray(np.tril(np.ones((tm, tm), np.float32)), dtype=bf16)
    return pl.pallas_call(
        _router_kernel,
        grid=(N_ROWS // tm,),
        in_specs=[pl.BlockSpec((tm, D_MODEL), lambda i: (i, 0)),
                  pl.BlockSpec((1, D_MODEL), lambda i: (0, 0)),
                  pl.BlockSpec((N_EXPERTS, D_MODEL), lambda i: (0, 0)),
                  pl.BlockSpec((tm, tm), lambda i: (0, 0))],
        out_specs=[pl.BlockSpec((tm, N_EXPERTS), lambda i: (i, 0)),
                   pl.BlockSpec((tm, N_EXPERTS), lambda i: (i, 0))],
        out_shape=[jax.ShapeDtypeStruct((N_ROWS, N_EXPERTS), f32),
                   jax.ShapeDtypeStruct((N_ROWS, N_EXPERTS), jnp.int32)],
        scratch_shapes=[pltpu.VMEM((1, N_EXPERTS), f32)],
        compiler_params=_cparams("arbitrary"),
    )(h, g, router.T, tri)


MOE_CHUNK = 256
MOE_GATHER = 768
MOE_SUB_SIZES = (640, 704, 736, 768)
MOE_SUBS_PER_VISIT = 3
MOE_VISIT_ROWS = MOE_SUBS_PER_VISIT * MOE_SUB_SIZES[-1]
MOE_CAP = 4 * MOE_VISIT_ROWS
MOE_TF = 256
N_CHUNKS = N_ROWS // MOE_CHUNK
N_GATHER = N_ROWS // MOE_GATHER
N_FSTEPS = D_FF // MOE_TF
MOE_MAX_VISITS = (2 * N_ROWS) // MOE_VISIT_ROWS + N_EXPERTS


def _moe_kernel(vis_e, vis_k, vis_nsub, vis_cls, vis_ok, rlo, xn_ref, dest_ref, w1_ref, w3_ref, w2_ref, y_ref, acc_ref):
    v, p = pl.program_id(0), pl.program_id(1)
    ok = vis_ok[v] == 1
    e = vis_e[v]
    base = vis_k[v] * MOE_VISIT_ROWS

    @pl.when(ok & (p == 0))
    def _():
        y_ref[...] = jnp.zeros_like(y_ref)

    @pl.when(ok & (p < N_GATHER))
    def _():
        lo = jnp.maximum(rlo[e * (N_GATHER + 1) + p] - base, 0)
        hi = jnp.minimum(rlo[e * (N_GATHER + 1) + p + 1] - base, MOE_VISIT_ROWS)

        @pl.when(hi > lo)
        def _():
            dest = dest_ref[...] - base
            win = lax.broadcasted_iota(jnp.int32, (MOE_CHUNK, MOE_GATHER), 0)

            def tile_body(t, carry):
                off = pl.multiple_of(t * MOE_CHUNK, MOE_CHUNK)
                onehot = jnp.where(win == dest - off, 1.0, 0.0).astype(bf16)
                rows = jnp.dot(onehot, xn_ref[...], preferred_element_type=f32)
                y_ref[pl.ds(off, MOE_CHUNK), :] += rows.astype(bf16)
                return carry

            t_first = lo // MOE_CHUNK
            t_last = (hi - 1) // MOE_CHUNK

            @pl.when(t_last - t_first < 2)
            def _():
                off = pl.multiple_of(jnp.minimum(t_first, MOE_VISIT_ROWS // MOE_CHUNK - 2) * MOE_CHUNK, MOE_CHUNK)
                win2 = lax.broadcasted_iota(jnp.int32, (2 * MOE_CHUNK, MOE_GATHER), 0)
                onehot = jnp.where(win2 == dest - off, 1.0, 0.0).astype(bf16)
                rows = jnp.dot(onehot, xn_ref[...], preferred_element_type=f32)
                y_ref[pl.ds(off, 2 * MOE_CHUNK), :] += rows.astype(bf16)

            @pl.when(t_last - t_first >= 2)
            def _():
                lax.fori_loop(t_first, t_last + 1, tile_body, 0)

    @pl.when(ok & (p >= N_GATHER))
    def _():
        f = p - N_GATHER
        w1 = w1_ref[...].astype(bf16)
        w3 = w3_ref[...].astype(bf16)
        w2 = w2_ref[...].astype(bf16)
        for cls, sub in enumerate(MOE_SUB_SIZES):
            @pl.when(vis_cls[v] == cls)
            def _():
                def sub_body(r, carry):
                    rows = pl.ds(pl.multiple_of(r * sub, 16), sub)

                    @pl.when(f == 0)
                    def _():
                        acc_ref[rows, :] = jnp.zeros((sub, D_MODEL), f32)

                    x = y_ref[rows, :]
                    a = jnp.dot(x, w1, preferred_element_type=f32)
                    b = jnp.dot(x, w3, preferred_element_type=f32)
                    act = (a * jax.nn.sigmoid(a) * b).astype(bf16)
                    acc_ref[rows, :] += jnp.dot(act, w2, preferred_element_type=f32)

                    @pl.when(f == N_FSTEPS - 1)
                    def _():
                        y_ref[rows, :] = acc_ref[rows, :].astype(y_ref.dtype)

                    return carry

                lax.fori_loop(0, vis_nsub[v], sub_body, 0)


def _moe_experts(xn, dest_exp, tables, w1, w3, w2, layer_idx):
    def chunk(v, p, ve, vk, vn, vc, vok, rlo):
        return jnp.where(vok[v] == 1, jnp.minimum(p, N_GATHER - 1), N_GATHER - 1)

    def fstep(v, p, ve, vk, vn, vc, vok, rlo):
        return jnp.where(vok[v] == 1, jnp.maximum(p - N_GATHER, 0), N_FSTEPS - 1)

    grid_spec = pltpu.PrefetchScalarGridSpec(
        num_scalar_prefetch=6,
        grid=(MOE_MAX_VISITS, N_GATHER + N_FSTEPS),
        in_specs=[pl.BlockSpec((MOE_GATHER, D_MODEL), lambda v, p, *t: (chunk(v, p, *t), 0)),
                  pl.BlockSpec((None, None, 1, MOE_GATHER), lambda v, p, *t: (t[0][v], chunk(v, p, *t), 0, 0)),
                  pl.BlockSpec((None, None, D_MODEL, MOE_TF), lambda v, p, *t: (layer_idx, t[0][v], 0, fstep(v, p, *t))),
                  pl.BlockSpec((None, None, D_MODEL, MOE_TF), lambda v, p, *t: (layer_idx, t[0][v], 0, fstep(v, p, *t))),
                  pl.BlockSpec((None, None, MOE_TF, D_MODEL), lambda v, p, *t: (layer_idx, t[0][v], fstep(v, p, *t), 0))],
        out_specs=pl.BlockSpec((None, MOE_VISIT_ROWS, D_MODEL), lambda v, p, *t: (t[0][v], t[1][v], 0),
                               pipeline_mode=pl.Buffered(1)),
        scratch_shapes=[pltpu.VMEM((MOE_VISIT_ROWS, D_MODEL), f32)],
    )
    return pl.pallas_call(
        _moe_kernel,
        grid_spec=grid_spec,
        out_shape=jax.ShapeDtypeStruct((N_EXPERTS, MOE_CAP, D_MODEL), bf16),
        compiler_params=_cparams("arbitrary", "arbitrary"),
    )(*tables, xn, dest_exp, w1, w3, w2)


def _combine_kernel(t0, nwin, h_ref, dest_ref, cmb_ref, g_ref, *rest):
    ywin = rest[:2 * N_EXPERTS]
    yp_ref, ys_ref, acc_ref = rest[2 * N_EXPERTS:]
    c = pl.program_id(0)
    acc_ref[...] = h_ref[...]
    dest = dest_ref[...]
    cmb = cmb_ref[...]
    lane = lax.broadcasted_iota(jnp.int32, (MOE_CHUNK, MOE_CHUNK), 1)
    for e in range(N_EXPERTS):
        for w in range(2):
            @pl.when(nwin[e * N_CHUNKS + c] > w)
            def _():
                rel = dest[:, e:e + 1] - MOE_CHUNK * (t0[e * N_CHUNKS + c] + w)
                onehot = jnp.where(rel == lane, 1.0, 0.0).astype(bf16)
                got = jnp.dot(onehot, ywin[2 * e + w][...], preferred_element_type=f32)
                acc_ref[...] += cmb[:, e:e + 1] * got

    n = _rms(acc_ref[...], g_ref[...])
    n_prompt_chunks = N_PROMPT_ROWS // MOE_CHUNK

    @pl.when(c < n_prompt_chunks)
    def _():
        yp_ref[...] = n

    @pl.when(c == n_prompt_chunks)
    def _():
        ys_ref[...] = n[:DEC_BATCH]


def _moe_combine(y_sorted, win_tables, h, dest_tok, cmb, g_final):
    t0, t0c, t1c, nwin = win_tables
    row = lambda c, *t: (c, 0)
    n_prompt_chunks = N_PROMPT_ROWS // MOE_CHUNK
    wins = []
    for e in range(N_EXPERTS):
        wins.append(pl.BlockSpec((None, MOE_CHUNK, D_MODEL), lambda c, a, b, *t, e=e: (e, a[e * N_CHUNKS + c], 0)))
        wins.append(pl.BlockSpec((None, MOE_CHUNK, D_MODEL), lambda c, a, b, *t, e=e: (e, b[e * N_CHUNKS + c], 0)))
    grid_spec = pltpu.PrefetchScalarGridSpec(
        num_scalar_prefetch=4,
        grid=(N_CHUNKS,),
        in_specs=[pl.BlockSpec((MOE_CHUNK, D_MODEL), row),
                  pl.BlockSpec((MOE_CHUNK, N_EXPERTS), row),
                  pl.BlockSpec((MOE_CHUNK, N_EXPERTS), row),
                  pl.BlockSpec((1, D_MODEL), lambda c, *t: (0, 0))] + wins,
        out_specs=[pl.BlockSpec((MOE_CHUNK, D_MODEL), lambda c, *t: (jnp.minimum(c, n_prompt_chunks - 1), 0)),
                   pl.BlockSpec((DEC_BATCH, D_MODEL), lambda c, *t: (0, 0))],
        scratch_shapes=[pltpu.VMEM((MOE_CHUNK, D_MODEL), f32)],
    )

    def body(t0c_ref, t1c_ref, t0_ref, nwin_ref, *refs):
        _combine_kernel(t0_ref, nwin_ref, *refs)

    return pl.pallas_call(
        body,
        grid_spec=grid_spec,
        out_shape=[jax.ShapeDtypeStruct((N_PROMPT_ROWS, D_MODEL), f32),
                   jax.ShapeDtypeStruct((DEC_BATCH, D_MODEL), f32)],
        compiler_params=_cparams("arbitrary"),
    )(t0c, t1c, t0, nwin, h, dest_tok, cmb, g_final, *([y_sorted] * (2 * N_EXPERTS)))


def _routing_tables(dest_tok):
    i32 = jnp.int32
    cnt = (dest_tok >= 0).astype(i32).reshape(N_CHUNKS, MOE_CHUNK, N_EXPERTS).sum(axis=1)
    csum = jnp.cumsum(cnt, axis=0)
    rank_lo = jnp.concatenate([jnp.zeros((1, N_EXPERTS), i32), csum], axis=0).T
    n_e = csum[-1]
    nv = (n_e + MOE_VISIT_ROWS - 1) // MOE_VISIT_ROWS
    cum_nv = jnp.cumsum(nv)
    total = cum_nv[-1]
    v = jnp.arange(MOE_MAX_VISITS, dtype=i32)
    vv = jnp.minimum(v, jnp.maximum(total - 1, 0))
    vis_e = jnp.minimum(jnp.sum((vv[:, None] >= cum_nv[None, :]).astype(i32), axis=1), N_EXPERTS - 1)
    vis_k = vv - (cum_nv - nv)[vis_e]
    vis_ok = (v < total).astype(i32)
    rows = jnp.clip(n_e[vis_e] - vis_k * MOE_VISIT_ROWS, 0, MOE_VISIT_ROWS)
    sizes = jnp.asarray(MOE_SUB_SIZES, i32)
    n_sub = (rows[:, None] + sizes[None, :] - 1) // sizes[None, :]
    padded = jnp.where(n_sub <= MOE_SUBS_PER_VISIT, n_sub * sizes[None, :], 2 * MOE_VISIT_ROWS)
    vis_cls = jnp.argmin(padded, axis=1).astype(i32)
    vis_nsub = jnp.where(vis_ok == 1, jnp.take_along_axis(n_sub, vis_cls[:, None], axis=1)[:, 0], 0)
    t0 = rank_lo[:, :N_CHUNKS] // MOE_CHUNK
    tmax = jnp.maximum(nv * (MOE_VISIT_ROWS // MOE_CHUNK) - 1, 0)[:, None]
    moe_tables = (vis_e, vis_k, vis_nsub.astype(i32), vis_cls, vis_ok,
                  rank_lo[:, ::MOE_GATHER // MOE_CHUNK].reshape(-1))
    t_last = (rank_lo[:, 1:] - 1) // MOE_CHUNK
    nwin = jnp.where(cnt.T > 0, t_last - t0 + 1, 0)
    chunk_ids = jnp.arange(N_CHUNKS, dtype=i32)[None, :]
    last_two = lax.cummax(jnp.where(nwin == 2, chunk_ids, 0), axis=1)
    t1c = jnp.take_along_axis(jnp.minimum(t0 + 1, tmax), last_two, axis=1)
    win_tables = (t0.reshape(-1), jnp.minimum(t0, tmax).reshape(-1), t1c.reshape(-1), nwin.reshape(-1))
    return moe_tables, win_tables


def _ffn_kernel(x_ref, w1_ref, w3_ref, w2_ref, h_ref, g_ref, ho_ref, no_ref):
    f = pl.program_id(1)

    @pl.when(f == 0)
    def _():
        ho_ref[...] = h_ref[...]

    x = x_ref[...]
    a = jnp.dot(x, w1_ref[...].astype(bf16), preferred_element_type=f32)
    b = jnp.dot(x, w3_ref[...].astype(bf16), preferred_element_type=f32)
    act = (a * jax.nn.sigmoid(a) * b).astype(bf16)
    ho_ref[...] += jnp.dot(act, w2_ref[...].astype(bf16), preferred_element_type=f32)

    @pl.when(f == pl.num_programs(1) - 1)
    def _():
        no_ref[...] = _rms(ho_ref[...], g_ref[...]).astype(no_ref.dtype)


def _ffn(xn, w1, w3, w2, layer_idx, h, g_next, tm=768, tf=512):
    row = lambda i, f: (i, 0)
    once = pl.Buffered(1)
    return pl.pallas_call(
        _ffn_kernel,
        grid=(N_ROWS // tm, D_FF // tf),
        in_specs=[pl.BlockSpec((tm, D_MODEL), row, pipeline_mode=once),
                  pl.BlockSpec((None, D_MODEL, tf), lambda i, f: (layer_idx, 0, f)),
                  pl.BlockSpec((None, D_MODEL, tf), lambda i, f: (layer_idx, 0, f)),
                  pl.BlockSpec((None, tf, D_MODEL), lambda i, f: (layer_idx, f, 0)),
                  pl.BlockSpec((tm, D_MODEL), row, pipeline_mode=once),
                  pl.BlockSpec((1, D_MODEL), lambda i, f: (0, 0))],
        out_specs=[pl.BlockSpec((tm, D_MODEL), row, pipeline_mode=once),
                   pl.BlockSpec((tm, D_MODEL), row, pipeline_mode=once)],
        out_shape=[jax.ShapeDtypeStruct((N_ROWS, D_MODEL), f32),
                   jax.ShapeDtypeStruct((N_ROWS, D_MODEL), bf16)],
        compiler_params=_cparams("parallel", "arbitrary"),
    )(xn, w1, w3, w2, h, g_next)


def _rope_patterns(pos):
    half = ROT_DIM // 2
    inv_freq = jnp.power(ROPE_THETA, -jnp.arange(half, dtype=f32) * 2.0 / ROT_DIM)
    ang = pos.astype(f32)[:, None] * inv_freq[None, :]
    cos, sin = jnp.cos(ang), jnp.sin(ang)
    n = pos.shape[0]
    cos_h = jnp.concatenate([cos, cos, jnp.ones((n, HEAD_DIM - ROT_DIM), f32)], axis=1)
    sin_h = jnp.concatenate([-sin, sin, jnp.zeros((n, HEAD_DIM - ROT_DIM), f32)], axis=1)
    return jnp.concatenate([cos_h, cos_h, sin_h, sin_h], axis=1)


def _block_diag_ones():
    i = np.arange(SEG) // HEAD_DIM
    return jnp.asarray((i[:, None] == i[None, :]).astype(np.float32), dtype=bf16)


def kernel(x_prompt, x_sample, cache_k, cache_v, state_conv, meta_tokens, g_mix, w_in, conv_w, attn_sinks,
           g_attn_out, g_conv_out, w_out, g_ffn, dense_w1, dense_w3, dense_w2, moe_router, moe_w1, moe_w3,
           moe_w2, g_final):
    w_cache = cache_k.shape[2]
    tail = jnp.concatenate([x_sample.reshape(DEC_BATCH, D_MODEL), jnp.zeros((META_ROW_IN_BLOCK, D_MODEL), f32),
                            meta_tokens.astype(f32)], axis=0)

    prompt_pos = N_META + jnp.arange(SEQ, dtype=jnp.int32)
    tail_pos = jnp.maximum(jnp.arange(ROW_BLOCK, dtype=jnp.int32) - META_ROW_IN_BLOCK, 0)
    rope_tab = _rope_patterns(jnp.concatenate([prompt_pos, tail_pos]))
    sample_tab = _rope_patterns(jnp.full((8,), PAST_LEN, jnp.int32))
    bd = _block_diag_ones()
    cache_k = cache_k.reshape(DEPTH, DEC_BATCH, w_cache, KV_DIM)
    cache_v = cache_v.reshape(DEPTH, DEC_BATCH, w_cache, KV_DIM)

    h, xn = _norm_rows(x_prompt.reshape(N_PROMPT_ROWS, D_MODEL), tail, g_mix[0:1])
    nk_p, nv_p, nc_p, nk_s, nv_s, nc_s = [], [], [], [], [], []
    for l in range(DEPTH):
        z = _in_proj(xn, w_in, l)
        g_attn = g_attn_out[l:l + 1]
        state_l = state_conv[l].reshape(DEC_BATCH, (CONV_WIDTH - 1) * CONV_DIM)
        an, nk_prompt, nv_prompt, c, u_tail, conv_state = _mixer(z, attn_sinks[l], rope_tab, g_attn, state_l,
                                                                 conv_w[l], g_conv_out[l:l + 1], bd)
        sinks_exp = jnp.repeat(attn_sinks[l].reshape(N_KV_HEADS, GROUP).T, HEAD_DIM, axis=1)
        an_decode, nk, nv = _attn_sample(z, cache_k, cache_v, l, sample_tab, sinks_exp, g_attn, bd)
        h, xn = _out_proj(an, an_decode, c, w_out, l, h, g_ffn[l:l + 1])

        j = l // 2
        if l % 2 == 0:
            h, xn = _ffn(xn, dense_w1, dense_w3, dense_w2, j, h, g_mix[l + 1:l + 2])
        else:
            cmb, dest_tok = _router(h, g_ffn[l:l + 1], moe_router[j])
            moe_tables, win_tables = _routing_tables(dest_tok)
            dest_exp = dest_tok.T.reshape(N_EXPERTS, N_GATHER, 1, MOE_GATHER)
            y_sorted = _moe_experts(xn, dest_exp, moe_tables, moe_w1, moe_w3, moe_w2, j)
            y_prompt, y_sample = _moe_combine(y_sorted, win_tables, h, dest_tok, cmb, g_final[None])

        nk_p.append(nk_prompt[:BATCH * WINDOW].reshape(BATCH, WINDOW, N_KV_HEADS, HEAD_DIM))
        nv_p.append(nv_prompt[:BATCH * WINDOW].reshape(BATCH, WINDOW, N_KV_HEADS, HEAD_DIM))
        nc_p.append(u_tail[:BATCH * 8].reshape(BATCH, 8, CONV_DIM)[:, 8 - (CONV_WIDTH - 1):])
        nk_s.append(nk.reshape(DEC_BATCH, w_cache, N_KV_HEADS, HEAD_DIM))
        nv_s.append(nv.reshape(DEC_BATCH, w_cache, N_KV_HEADS, HEAD_DIM))
        nc_s.append(conv_state.reshape(DEC_BATCH, CONV_WIDTH - 1, CONV_DIM))

    return (y_prompt.reshape(BATCH, SEQ, D_MODEL), y_sample.reshape(DEC_BATCH, 1, D_MODEL),
            jnp.stack(nk_p), jnp.stack(nv_p), jnp.stack(nc_p), jnp.stack(nk_s), jnp.stack(nv_s), jnp.stack(nc_s))
```

```python
import functools

import jax
import jax.numpy as jnp
import numpy as np
from jax import lax
from jax.experimental import pallas as pl
from jax.experimental.pallas import tpu as pltpu

D_MODEL = 2048
BATCH = 4
SEQ = 2048
DEPTH = 2
DEC_BATCH = 128
PAST_LEN = 8192
HEAD_DIM = 64
ATTN_DIM = D_MODEL // 2
N_Q_HEADS = ATTN_DIM // HEAD_DIM
N_KV_HEADS = N_Q_HEADS // 4
GROUP = N_Q_HEADS // N_KV_HEADS
KV_DIM = N_KV_HEADS * HEAD_DIM
CONV_DIM = D_MODEL - ATTN_DIM
CONV_WIDTH = 3
IN_DIM = ATTN_DIM + 2 * KV_DIM + 3 * CONV_DIM
WINDOW = 128
ROPE_THETA = 500000.0
ROT_DIM = HEAD_DIM // 4
N_META = 16
D_FF = 7 * D_MODEL // 2
N_EXPERTS = 8
EPS = 1e-5

ROW_BLOCK = 128
N_PROMPT_ROWS = BATCH * SEQ
SAMPLE_ROW0 = N_PROMPT_ROWS
TAIL_ROW0 = SAMPLE_ROW0 + DEC_BATCH
N_ROWS = TAIL_ROW0 + ROW_BLOCK
N_ROW_BLOCKS = N_ROWS // ROW_BLOCK
BLOCKS_PER_SEQ = SEQ // ROW_BLOCK
SAMPLE_BLOCK = SAMPLE_ROW0 // ROW_BLOCK
META_BLOCK = TAIL_ROW0 // ROW_BLOCK
META_ROW_IN_BLOCK = ROW_BLOCK - N_META
SEG = 256

VMEM_LIMIT_V7X = 56 * 1024 * 1024

bf16 = jnp.bfloat16
f32 = jnp.float32


def _cparams(*sem):
    return pltpu.CompilerParams(dimension_semantics=sem, vmem_limit_bytes=VMEM_LIMIT_V7X)


def _rms(x, g):
    ms = jnp.mean(x * x, axis=-1, keepdims=True)
    return (x * lax.rsqrt(ms + EPS)) * g


def _segsum(x, bd):
    x1 = x.astype(bf16)
    r1 = x - x1.astype(f32)
    x2 = r1.astype(bf16)
    x3 = (r1 - x2.astype(f32)).astype(bf16)
    d = functools.partial(jnp.dot, preferred_element_type=f32)
    return d(x1, bd) + d(x2, bd) + d(x3, bd)


def _rope(x, cos_t, sin_t):
    n = x.shape[-1]
    lane = lax.broadcasted_iota(jnp.int32, x.shape, x.ndim - 1) % HEAD_DIM
    nxt = pltpu.roll(x, n - ROT_DIM // 2, axis=x.ndim - 1)
    prv = pltpu.roll(x, ROT_DIM // 2, axis=x.ndim - 1)
    return x * cos_t + jnp.where(lane < ROT_DIM // 2, nxt, prv) * sin_t


def _norm_kernel(xp_ref, tail_ref, g_ref, h_ref, o_ref):
    is_tail = pl.program_id(0) == pl.num_programs(0) - 1
    x = jnp.where(is_tail, tail_ref[...], xp_ref[...])
    h_ref[...] = x
    o_ref[...] = _rms(x, g_ref[...]).astype(o_ref.dtype)


def _norm_rows(x_prompt_rows, tail_rows, g):
    tm = tail_rows.shape[0]
    n_prompt_tiles = N_PROMPT_ROWS // tm
    row = lambda i: (i, 0)
    return pl.pallas_call(
        _norm_kernel,
        grid=(N_ROWS // tm,),
        in_specs=[pl.BlockSpec((tm, D_MODEL), lambda i: (jnp.minimum(i, n_prompt_tiles - 1), 0)),
                  pl.BlockSpec((tm, D_MODEL), lambda i: (0, 0)),
                  pl.BlockSpec((1, D_MODEL), lambda i: (0, 0))],
        out_specs=[pl.BlockSpec((tm, D_MODEL), row), pl.BlockSpec((tm, D_MODEL), row)],
        out_shape=[jax.ShapeDtypeStruct((N_ROWS, D_MODEL), f32),
                   jax.ShapeDtypeStruct((N_ROWS, D_MODEL), bf16)],
        compiler_params=_cparams("parallel"),
    )(x_prompt_rows, tail_rows, g)


def _inproj_kernel(x_ref, w_ref, o_ref, wbf_ref):
    @pl.when(pl.program_id(1) == 0)
    def _():
        wbf_ref[...] = w_ref[...].astype(bf16)

    o_ref[...] = jnp.dot(x_ref[...], wbf_ref[...], preferred_element_type=f32)


def _in_proj(xn, w_in, layer, tm=768, tn=1536):
    return pl.pallas_call(
        _inproj_kernel,
        grid=(IN_DIM // tn, N_ROWS // tm),
        in_specs=[pl.BlockSpec((tm, D_MODEL), lambda j, i: (i, 0)),
                  pl.BlockSpec((None, D_MODEL, tn), lambda j, i: (layer, 0, j))],
        out_specs=pl.BlockSpec((tm, tn), lambda j, i: (i, j)),
        out_shape=jax.ShapeDtypeStruct((N_ROWS, IN_DIM), f32),
        scratch_shapes=[pltpu.VMEM((D_MODEL, tn), bf16)],
        compiler_params=_cparams("parallel", "arbitrary"),
    )(xn, w_in)


def _mixer_kernel(sink_ref, q_ref, kc_ref, vc_ref, kp_ref, vp_ref, tc_ref, tp_ref, g_ref,
                  z1_ref, z2_ref, p1_ref, p2_ref, s0_ref, s1_ref, cw_ref, gc_ref, bd_ref,
                  an_ref, nk_ref, nv_ref, c_ref, utail_ref, ns_ref):
    s = pl.program_id(0)
    _conv_block(s == N_ROW_BLOCKS - 1, z1_ref, z2_ref, p1_ref, p2_ref, s0_ref, s1_ref, cw_ref, gc_ref, bd_ref,
                c_ref, utail_ref, ns_ref)
    is_meta = s == N_ROW_BLOCKS - 2
    j = s % BLOCKS_PER_SEQ
    tc = tc_ref[...]
    tp = tp_ref[...]
    q = _rope(q_ref[...] * (HEAD_DIM ** -0.5), jnp.tile(tc[:, :128], (1, 8)), jnp.tile(tc[:, 128:], (1, 8)))
    kc = _rope(kc_ref[...], jnp.tile(tc[:, :128], (1, 2)), jnp.tile(tc[:, 128:], (1, 2)))
    kp = _rope(kp_ref[...], jnp.tile(tp[:, :128], (1, 2)), jnp.tile(tp[:, 128:], (1, 2)))
    nk_ref[...] = kc
    nv_ref[...] = vc_ref[...]

    kall = jnp.concatenate([kp, kc], axis=0).astype(bf16)
    vall = jnp.concatenate([vp_ref[...], vc_ref[...]], axis=0)

    row = lax.broadcasted_iota(jnp.int32, (ROW_BLOCK, 2 * ROW_BLOCK), 0)
    col = lax.broadcasted_iota(jnp.int32, (ROW_BLOCK, 2 * ROW_BLOCK), 1)
    lo_prev = jnp.where(is_meta, ROW_BLOCK, jnp.where(j == 0, META_ROW_IN_BLOCK, 0))
    lo_cur = jnp.where(is_meta, META_ROW_IN_BLOCK, 0)
    ccur = col - ROW_BLOCK
    valid = (((col < ROW_BLOCK) & (col >= row) & (col >= lo_prev))
             | ((ccur >= 0) & (ccur <= row) & (ccur >= lo_cur)))
    bias = jnp.where(valid, 0.0, -jnp.inf)
    bias4 = jnp.concatenate([bias] * GROUP, axis=0)
    half = lax.broadcasted_iota(jnp.int32, (ROW_BLOCK, 128), 1) // HEAD_DIM
    half4 = lax.broadcasted_iota(jnp.int32, (GROUP * ROW_BLOCK, 128), 1) // HEAD_DIM
    halfkv = lax.broadcasted_iota(jnp.int32, (2 * ROW_BLOCK, 128), 1) // HEAD_DIM
    g_all = g_ref[...]

    kvs = range(N_KV_HEADS)
    q4s, sinks4, kpairs, vpairs = [], [], [], []
    for kvh in kvs:
        kside = kvh % 2
        kpairs.append(kall[:, 128 * (kvh // 2):128 * (kvh // 2) + 128])
        vpairs.append(jnp.where(halfkv == kside, vall[:, 128 * (kvh // 2):128 * (kvh // 2) + 128], 1.0).astype(bf16))
        qs, sinks = [], []
        for g in range(GROUP):
            h = GROUP * kvh + g
            x = q[:, 128 * (h // 2):128 * (h // 2) + 128]
            if h % 2 != kside:
                x = pltpu.roll(x, HEAD_DIM, axis=1)
            qs.append(jnp.where(half == kside, x, 0.0))
            sinks.append(jnp.full((ROW_BLOCK, 1), sink_ref[h], f32))
        q4s.append(jnp.concatenate(qs, axis=0).astype(bf16))
        sinks4.append(jnp.concatenate(sinks, axis=0))
    nt = (((1,), (1,)), ((), ()))
    scs = [lax.dot_general(q4s[k], kpairs[k], nt, preferred_element_type=f32) + bias4 for k in kvs]
    mxs = [jnp.maximum(jnp.max(scs[k], axis=-1, keepdims=True), sinks4[k]) for k in kvs]
    ps = [jnp.exp(scs[k] - mxs[k]).astype(bf16) for k in kvs]
    ovs = [jnp.dot(ps[k], vpairs[k], preferred_element_type=f32) for k in kvs]
    dens = [pltpu.roll(ovs[k], HEAD_DIM, axis=1) + jnp.exp(sinks4[k] - mxs[k]) for k in kvs]
    os_ = [ovs[k] * (1.0 / dens[k]) for k in kvs]
    mss = [jnp.sum(jnp.where(half4 == k % 2, os_[k] * os_[k], 0.0), axis=-1, keepdims=True) * (1.0 / HEAD_DIM)
           for k in kvs]
    ons = [os_[k] * lax.rsqrt(mss[k] + EPS) for k in kvs]
    for kvh in kvs:
        for pair in range(2):
            parts = []
            for side in range(2):
                g = 2 * pair + side
                x = ons[kvh][ROW_BLOCK * g:ROW_BLOCK * (g + 1)]
                if side != kvh % 2:
                    x = pltpu.roll(x, HEAD_DIM, axis=1)
                parts.append(x)
            blk = 2 * kvh + pair
            out = jnp.where(half == 0, parts[0], parts[1]) * g_all[:, 128 * blk:128 * blk + 128]
            an_ref[:, 128 * blk:128 * blk + 128] = out.astype(an_ref.dtype)


def _cur_block(s):
    return jnp.where(s == N_ROW_BLOCKS - 2, META_BLOCK, jnp.where(s == N_ROW_BLOCKS - 1, SAMPLE_BLOCK, s))


def _prev_block(s):
    return jnp.where((s % BLOCKS_PER_SEQ == 0) | (s == N_ROW_BLOCKS - 2), META_BLOCK, s - 1)


def _cur_tab(s):
    return jnp.where(s == N_ROW_BLOCKS - 2, BLOCKS_PER_SEQ, s % BLOCKS_PER_SEQ)


def _prev_tab(s):
    return jnp.where((s % BLOCKS_PER_SEQ == 0) | (s == N_ROW_BLOCKS - 2), BLOCKS_PER_SEQ, s % BLOCKS_PER_SEQ - 1)


def _mixer(z, sinks, rope_tab, g_attn, state_l, conv_w_l, g_conv, bd):
    kcol, vcol = ATTN_DIM // KV_DIM, ATTN_DIM // KV_DIM + 1
    wide = 3 * CONV_DIM // 2
    cur = lambda s: (_cur_block(s), 0)
    return pl.pallas_call(
        _mixer_kernel,
        grid=(N_ROW_BLOCKS,),
        in_specs=[pl.BlockSpec(memory_space=pltpu.SMEM),
                  pl.BlockSpec((ROW_BLOCK, ATTN_DIM), cur),
                  pl.BlockSpec((ROW_BLOCK, KV_DIM), lambda s: (_cur_block(s), kcol)),
                  pl.BlockSpec((ROW_BLOCK, KV_DIM), lambda s: (_cur_block(s), vcol)),
                  pl.BlockSpec((ROW_BLOCK, KV_DIM), lambda s: (_prev_block(s), kcol)),
                  pl.BlockSpec((ROW_BLOCK, KV_DIM), lambda s: (_prev_block(s), vcol)),
                  pl.BlockSpec((ROW_BLOCK, 256), lambda s: (_cur_tab(s), 0)),
                  pl.BlockSpec((ROW_BLOCK, 256), lambda s: (_prev_tab(s), 0)),
                  pl.BlockSpec((1, ATTN_DIM), lambda s: (0, 0)),
                  pl.BlockSpec((ROW_BLOCK, wide), lambda s: (_cur_block(s), 1)),
                  pl.BlockSpec((ROW_BLOCK, wide), lambda s: (_cur_block(s), 2)),
                  pl.BlockSpec((8, wide), lambda s: (_conv_prev_block(_cur_block(s)), 1)),
                  pl.BlockSpec((8, wide), lambda s: (_conv_prev_block(_cur_block(s)), 2)),
                  pl.BlockSpec((DEC_BATCH, CONV_DIM), lambda s: (0, 0)),
                  pl.BlockSpec((DEC_BATCH, CONV_DIM), lambda s: (0, 1)),
                  pl.BlockSpec((CONV_WIDTH, CONV_DIM), lambda s: (0, 0)),
                  pl.BlockSpec((1, CONV_DIM), lambda s: (0, 0)),
                  pl.BlockSpec((SEG, SEG), lambda s: (0, 0))],
        out_specs=[pl.BlockSpec((ROW_BLOCK, ATTN_DIM), cur),
                   pl.BlockSpec((WINDOW, KV_DIM), lambda s: (s // BLOCKS_PER_SEQ, 0)),
                   pl.BlockSpec((WINDOW, KV_DIM), lambda s: (s // BLOCKS_PER_SEQ, 0)),
                   pl.BlockSpec((ROW_BLOCK, CONV_DIM), cur),
                   pl.BlockSpec((8, CONV_DIM), lambda s: (s // BLOCKS_PER_SEQ, 0)),
                   pl.BlockSpec((DEC_BATCH, 2 * CONV_DIM), lambda s: (0, 0))],
        out_shape=[jax.ShapeDtypeStruct((N_ROWS, ATTN_DIM), bf16),
                   jax.ShapeDtypeStruct(((BATCH + 1) * WINDOW, KV_DIM), f32),
                   jax.ShapeDtypeStruct(((BATCH + 1) * WINDOW, KV_DIM), f32),
                   jax.ShapeDtypeStruct((N_ROWS, CONV_DIM), bf16),
                   jax.ShapeDtypeStruct(((BATCH + 1) * 8, CONV_DIM), f32),
                   jax.ShapeDtypeStruct((DEC_BATCH, 2 * CONV_DIM), f32)],
        compiler_params=_cparams("arbitrary"),
    )(sinks, z, z, z, z, z, rope_tab, rope_tab, g_attn, z, z, z, z, state_l, state_l, conv_w_l, g_conv, bd)


def _attn_sample_kernel(q_ref, k_ref, v_ref, ck_ref, cv_ref, tab_ref, sink_ref, g_ref, bd_ref,
                        an_ref, nk_ref, nv_ref):
    bt = q_ref.shape[0]
    w = ck_ref.shape[1]
    tab = tab_ref[...]
    q = _rope(q_ref[...], jnp.tile(tab[0:1, :128], (1, 8)), jnp.tile(tab[0:1, 128:], (1, 8)))
    k = _rope(k_ref[...], jnp.tile(tab[0:1, :128], (1, 2)), jnp.tile(tab[0:1, 128:], (1, 2)))
    v = v_ref[...]
    ck = ck_ref[...]
    cv = cv_ref[...]
    bd = bd_ref[...]

    last = lax.broadcasted_iota(jnp.int32, (w, KV_DIM), 0) == w - 1
    for b in range(bt):
        nk_ref[b] = jnp.where(last, k[b:b + 1], pltpu.roll(ck[b], w - 1, axis=0))
        nv_ref[b] = jnp.where(last, v[b:b + 1], pltpu.roll(cv[b], w - 1, axis=0))

    seg = lax.broadcasted_iota(jnp.int32, (bt, SEG), 1) // HEAD_DIM
    scale = HEAD_DIM ** -0.5
    chunks = [jnp.zeros((bt, SEG), f32) for _ in range(N_KV_HEADS)]
    gs = range(GROUP)
    qgs = []
    for g in gs:
        qg = jnp.zeros((bt, SEG), f32)
        for kvh in range(N_KV_HEADS):
            x = q[:, SEG * kvh:SEG * (kvh + 1)]
            sh = (HEAD_DIM * (kvh - g)) % SEG
            if sh:
                x = pltpu.roll(x, sh, axis=1)
            qg = jnp.where(seg == kvh, x, qg)
        qgs.append(qg)
    scs = [(_segsum((ck * qgs[g][:, None, :]).reshape(bt * w, SEG), bd) * scale).reshape(bt, w, SEG) for g in gs]
    sns = [_segsum(qgs[g] * k, bd) * scale for g in gs]
    sinks = [sink_ref[g:g + 1, :] for g in gs]
    mxs = [jnp.maximum(jnp.maximum(jnp.max(scs[g], axis=1), sns[g]), sinks[g]) for g in gs]
    ps = [jnp.exp(scs[g] - mxs[g][:, None, :]) for g in gs]
    pns = [jnp.exp(sns[g] - mxs[g]) for g in gs]
    dens = [jnp.sum(ps[g], axis=1) + pns[g] + jnp.exp(sinks[g] - mxs[g]) for g in gs]
    os_ = [(jnp.sum(ps[g] * cv, axis=1) + pns[g] * v) / dens[g] for g in gs]
    mss = [_segsum(os_[g] * os_[g], bd) * (1.0 / HEAD_DIM) for g in gs]
    for g in gs:
        on = os_[g] * lax.rsqrt(mss[g] + EPS)
        for kvh in range(N_KV_HEADS):
            sh = (HEAD_DIM * (g - kvh)) % SEG
            x = pltpu.roll(on, sh, axis=1) if sh else on
            chunks[kvh] = jnp.where(seg == g, x, chunks[kvh])
    out = jnp.concatenate(chunks, axis=1) * g_ref[...]
    an_ref[...] = out.astype(an_ref.dtype)


def _attn_sample(z, cache_k, cache_v, layer, tab, sinks_exp, g_attn, bd, bt=16):
    w = cache_k.shape[2]
    row0 = SAMPLE_ROW0 // bt
    kcol, vcol = ATTN_DIM // KV_DIM, ATTN_DIM // KV_DIM + 1
    cache_in = pl.BlockSpec((None, bt, w, KV_DIM), lambda i: (layer, i, 0, 0))
    cache_spec = pl.BlockSpec((bt, w, KV_DIM), lambda i: (i, 0, 0))
    return pl.pallas_call(
        _attn_sample_kernel,
        grid=(DEC_BATCH // bt,),
        in_specs=[pl.BlockSpec((bt, ATTN_DIM), lambda i: (row0 + i, 0)),
                  pl.BlockSpec((bt, KV_DIM), lambda i: (row0 + i, kcol)),
                  pl.BlockSpec((bt, KV_DIM), lambda i: (row0 + i, vcol)),
                  cache_in, cache_in,
                  pl.BlockSpec((8, 256), lambda i: (0, 0)),
                  pl.BlockSpec((GROUP, SEG), lambda i: (0, 0)),
                  pl.BlockSpec((1, ATTN_DIM), lambda i: (0, 0)),
                  pl.BlockSpec((SEG, SEG), lambda i: (0, 0))],
        out_specs=[pl.BlockSpec((bt, ATTN_DIM), lambda i: (i, 0)), cache_spec, cache_spec],
        out_shape=[jax.ShapeDtypeStruct((DEC_BATCH, ATTN_DIM), bf16),
                   jax.ShapeDtypeStruct(cache_k.shape[1:], f32),
                   jax.ShapeDtypeStruct(cache_v.shape[1:], f32)],
        compiler_params=_cparams("parallel"),
    )(z, z, z, cache_k, cache_v, tab, sinks_exp, g_attn, bd)


def _conv_block(is_sample, z1_ref, z2_ref, p1_ref, p2_ref, s0_ref, s1_ref, w_ref, g_ref, bd_ref,
                c_ref, utail_ref, ns_ref):
    def split(b1, b2):
        gb = b1[:, :CONV_DIM]
        gc = jnp.concatenate([b1[:, CONV_DIM:], b2[:, :CONV_DIM // 2]], axis=1)
        return gb, gc * b2[:, CONV_DIM // 2:]

    gb, u = split(z1_ref[...], z2_ref[...])
    _, up = split(p1_ref[...], p2_ref[...])
    utail_ref[...] = u[ROW_BLOCK - 8:]
    ns_ref[:, :CONV_DIM] = s1_ref[...]
    ns_ref[:, CONV_DIM:] = u

    row = lax.broadcasted_iota(jnp.int32, u.shape, 0)
    u1 = jnp.where(row == 0, up[7:8], pltpu.roll(u, 1, axis=0))
    u2 = jnp.where(row == 0, up[6:7], jnp.where(row == 1, up[7:8], pltpu.roll(u, 2, axis=0)))
    u1 = jnp.where(is_sample, s1_ref[...], u1)
    u2 = jnp.where(is_sample, s0_ref[...], u2)
    w = w_ref[...]
    y = u2 * w[0:1] + u1 * w[1:2] + u * w[2:3]
    t = gb * y
    bd = bd_ref[...]
    g = g_ref[...]
    cols = [slice(SEG * c, SEG * (c + 1)) for c in range(CONV_DIM // SEG)]
    mss = [_segsum(t[:, cs] * t[:, cs], bd) * (1.0 / HEAD_DIM) for cs in cols]
    for cs, ms in zip(cols, mss):
        c_ref[:, cs] = (t[:, cs] * lax.rsqrt(ms + EPS) * g[:, cs]).astype(c_ref.dtype)


def _conv_prev_block(i):
    last8_of_meta = N_ROWS // 8 - 1
    return jnp.where(i % BLOCKS_PER_SEQ == 0, last8_of_meta, (ROW_BLOCK // 8) * i - 1)


def _outproj_kernel(*refs, routed):
    if routed:
        (a_ref, as_ref, c_ref, w_ref, h_ref, g_ref, r_ref, tri_ref,
         ho_ref, no_ref, cmb_ref, dest_ref, wbf_ref, carry_ref) = refs
    else:
        a_ref, as_ref, c_ref, w_ref, h_ref, g_ref, ho_ref, no_ref, wbf_ref = refs
    i = pl.program_id(0)

    @pl.when(i == 0)
    def _():
        wbf_ref[...] = w_ref[...].astype(bf16)
        if routed:
            carry_ref[...] = jnp.zeros_like(carry_ref)

    tm = a_ref.shape[0]
    off = SAMPLE_ROW0 % tm
    a = a_ref[...]
    pieces = [jnp.zeros((off, ATTN_DIM), a.dtype)] if off else []
    pieces.append(as_ref[...])
    if tm - off - DEC_BATCH:
        pieces.append(jnp.zeros((tm - off - DEC_BATCH, ATTN_DIM), a.dtype))
    row = lax.broadcasted_iota(jnp.int32, a.shape, 0)
    is_decode = (i == SAMPLE_ROW0 // tm) & (row >= off) & (row < off + DEC_BATCH)
    a = jnp.where(is_decode, jnp.concatenate(pieces, axis=0), a)
    acc = jnp.dot(a, wbf_ref[:ATTN_DIM, :], preferred_element_type=f32)
    acc += jnp.dot(c_ref[...], wbf_ref[ATTN_DIM:, :], preferred_element_type=f32)
    hn = h_ref[...] + acc
    ho_ref[...] = hn
    n = _rms(hn, g_ref[...])
    no_ref[...] = n.astype(no_ref.dtype)
    if routed:
        _route_rows(n, i * tm, r_ref, tri_ref, cmb_ref, dest_ref, carry_ref)


def _route_rows(n, row0, r_ref, tri_ref, cmb_ref, dest_ref, carry_ref):
    tm = n.shape[0]
    rt = r_ref[...]
    idx = lax.broadcasted_iota(jnp.int32, (tm, N_EXPERTS), 1)
    logits = jnp.zeros((tm, N_EXPERTS), f32)
    for e in range(N_EXPERTS):
        prod = n * rt[e:e + 1, :]
        part = prod[:, :128]
        for c in range(1, D_MODEL // 128):
            part = part + prod[:, 128 * c:128 * (c + 1)]
        logits = jnp.where(idx == e, jnp.sum(part, axis=-1, keepdims=True), logits)
    m1 = jnp.max(logits, axis=-1, keepdims=True)
    i1 = jnp.min(jnp.where(logits == m1, idx, N_EXPERTS), axis=-1, keepdims=True)
    rest = jnp.where(idx == i1, -jnp.inf, logits)
    m2 = jnp.max(rest, axis=-1, keepdims=True)
    i2 = jnp.min(jnp.where(rest == m2, idx, N_EXPERTS), axis=-1, keepdims=True)
    e2 = jnp.exp(m2 - m1)
    den = 1.0 + e2
    rowid = row0 + lax.broadcasted_iota(jnp.int32, logits.shape, 0)
    real = (rowid < TAIL_ROW0) | (rowid >= N_ROWS - N_META)
    sel = ((idx == i1) | (idx == i2)) & real
    cmb_ref[...] = jnp.where(sel, jnp.where(idx == i1, 1.0 / den, e2 / den), 0.0)
    self = jnp.where(sel, 1.0, 0.0)
    incl = jnp.dot(tri_ref[...], self.astype(bf16), preferred_element_type=f32)
    carry = carry_ref[...]
    dest_ref[...] = jnp.where(sel, incl - self + carry, -1.0).astype(jnp.int32)
    carry_ref[...] = carry + incl[tm - 1:tm, :]


def _out_proj(an, an_decode, c, w_out, layer, h, g_next, router=None, tm=384):
    assert SAMPLE_ROW0 % tm + DEC_BATCH <= tm
    routed = router is not None
    row = lambda i: (i, 0)
    const = lambda i: (0, 0)
    in_specs = [pl.BlockSpec((tm, ATTN_DIM), row),
                pl.BlockSpec((DEC_BATCH, ATTN_DIM), const),
                pl.BlockSpec((tm, CONV_DIM), row),
                pl.BlockSpec((None, D_MODEL, D_MODEL), lambda i: (layer, 0, 0), pipeline_mode=pl.Buffered(1)),
                pl.BlockSpec((tm, D_MODEL), row),
                pl.BlockSpec((1, D_MODEL), const)]
    out_specs = [pl.BlockSpec((tm, D_MODEL), row), pl.BlockSpec((tm, D_MODEL), row)]
    out_shape = [jax.ShapeDtypeStruct((N_ROWS, D_MODEL), f32), jax.ShapeDtypeStruct((N_ROWS, D_MODEL), bf16)]
    scratch = [pltpu.VMEM((D_MODEL, D_MODEL), bf16)]
    args = [an, an_decode, c, w_out, h, g_next]
    if routed:
        tri = jnp.asarray(np.tril(np.ones((tm, tm), np.float32)), dtype=bf16)
        in_specs += [pl.BlockSpec((N_EXPERTS, D_MODEL), const), pl.BlockSpec((tm, tm), const)]
        out_specs += [pl.BlockSpec((tm, N_EXPERTS), row), pl.BlockSpec((tm, N_EXPERTS), row)]
        out_shape += [jax.ShapeDtypeStruct((N_ROWS, N_EXPERTS), f32),
                      jax.ShapeDtypeStruct((N_ROWS, N_EXPERTS), jnp.int32)]
        scratch.append(pltpu.VMEM((1, N_EXPERTS), f32))
        args += [router.T, tri]
    return pl.pallas_call(
        functools.partial(_outproj_kernel, routed=routed),
        grid=(N_ROWS // tm,),
        in_specs=in_specs,
        out_specs=out_specs,
        out_shape=out_shape,
        scratch_shapes=scratch,
        compiler_params=_cparams("arbitrary"),
    )(*args)


MOE_CHUNK = 256
MOE_GATHER = 768
MOE_SUB_SIZES = (640, 704, 736, 768)
MOE_SUBS_PER_VISIT = 3
MOE_VISIT_ROWS = MOE_SUBS_PER_VISIT * MOE_SUB_SIZES[-1]
MOE_CAP = 4 * MOE_VISIT_ROWS
MOE_TF = 256
N_CHUNKS = N_ROWS // MOE_CHUNK
N_GATHER = N_ROWS // MOE_GATHER
N_FSTEPS = D_FF // MOE_TF
MOE_MAX_VISITS = (2 * N_ROWS) // MOE_VISIT_ROWS + N_EXPERTS


def _moe_kernel(vis_e, vis_k, vis_nsub, vis_cls, vis_ok, rlo, xn_ref, dest_ref, w1_ref, w3_ref, w2_ref, y_ref, acc_ref):
    v, p = pl.program_id(0), pl.program_id(1)
    ok = vis_ok[v] == 1
    e = vis_e[v]
    base = vis_k[v] * MOE_VISIT_ROWS

    @pl.when(ok & (p == 0))
    def _():
        y_ref[...] = jnp.zeros_like(y_ref)

    @pl.when(ok & (p < N_GATHER))
    def _():
        lo = jnp.maximum(rlo[e * (N_GATHER + 1) + p] - base, 0)
        hi = jnp.minimum(rlo[e * (N_GATHER + 1) + p + 1] - base, MOE_VISIT_ROWS)

        @pl.when(hi > lo)
        def _():
            dest = dest_ref[...] - base
            win = lax.broadcasted_iota(jnp.int32, (MOE_CHUNK, MOE_GATHER), 0)

            def tile_body(t, carry):
                off = pl.multiple_of(t * MOE_CHUNK, MOE_CHUNK)
                onehot = jnp.where(win == dest - off, 1.0, 0.0).astype(bf16)
                rows = jnp.dot(onehot, xn_ref[...], preferred_element_type=f32)
                y_ref[pl.ds(off, MOE_CHUNK), :] += rows.astype(bf16)
                return carry

            t_first = lo // MOE_CHUNK
            t_last = (hi - 1) // MOE_CHUNK

            @pl.when(t_last - t_first < 2)
            def _():
                off = pl.multiple_of(jnp.minimum(t_first, MOE_VISIT_ROWS // MOE_CHUNK - 2) * MOE_CHUNK, MOE_CHUNK)
                win2 = lax.broadcasted_iota(jnp.int32, (2 * MOE_CHUNK, MOE_GATHER), 0)
                onehot = jnp.where(win2 == dest - off, 1.0, 0.0).astype(bf16)
                rows = jnp.dot(onehot, xn_ref[...], preferred_element_type=f32)
                y_ref[pl.ds(off, 2 * MOE_CHUNK), :] += rows.astype(bf16)

            @pl.when(t_last - t_first >= 2)
            def _():
                lax.fori_loop(t_first, t_last + 1, tile_body, 0)

    @pl.when(ok & (p >= N_GATHER))
    def _():
        f = p - N_GATHER
        w1 = w1_ref[...].astype(bf16)
        w3 = w3_ref[...].astype(bf16)
        w2 = w2_ref[...].astype(bf16)
        for cls, sub in enumerate(MOE_SUB_SIZES):
            @pl.when(vis_cls[v] == cls)
            def _():
                def sub_body(r, carry):
                    rows = pl.ds(pl.multiple_of(r * sub, 16), sub)

                    @pl.when(f == 0)
                    def _():
                        acc_ref[rows, :] = jnp.zeros((sub, D_MODEL), f32)

                    x = y_ref[rows, :]
                    a = jnp.dot(x, w1, preferred_element_type=f32)
                    b = jnp.dot(x, w3, preferred_element_type=f32)
                    act = (a * jax.nn.sigmoid(a) * b).astype(bf16)
                    acc_ref[rows, :] += jnp.dot(act, w2, preferred_element_type=f32)

                    @pl.when(f == N_FSTEPS - 1)
                    def _():
                        y_ref[rows, :] = acc_ref[rows, :].astype(y_ref.dtype)

                    return carry

                lax.fori_loop(0, vis_nsub[v], sub_body, 0)


def _moe_experts(xn, dest_exp, tables, w1, w3, w2, layer_idx):
    def chunk(v, p, ve, vk, vn, vc, vok, rlo):
        return jnp.where(vok[v] == 1, jnp.minimum(p, N_GATHER - 1), N_GATHER - 1)

    def fstep(v, p, ve, vk, vn, vc, vok, rlo):
        return jnp.where(vok[v] == 1, jnp.maximum(p - N_GATHER, 0), N_FSTEPS - 1)

    grid_spec = pltpu.PrefetchScalarGridSpec(
        num_scalar_prefetch=6,
        grid=(MOE_MAX_VISITS, N_GATHER + N_FSTEPS),
        in_specs=[pl.BlockSpec((MOE_GATHER, D_MODEL), lambda v, p, *t: (chunk(v, p, *t), 0)),
                  pl.BlockSpec((None, None, 1, MOE_GATHER), lambda v, p, *t: (t[0][v], chunk(v, p, *t), 0, 0)),
                  pl.BlockSpec((None, None, D_MODEL, MOE_TF), lambda v, p, *t: (layer_idx, t[0][v], 0, fstep(v, p, *t))),
                  pl.BlockSpec((None, None, D_MODEL, MOE_TF), lambda v, p, *t: (layer_idx, t[0][v], 0, fstep(v, p, *t))),
                  pl.BlockSpec((None, None, MOE_TF, D_MODEL), lambda v, p, *t: (layer_idx, t[0][v], fstep(v, p, *t), 0))],
        out_specs=pl.BlockSpec((None, MOE_VISIT_ROWS, D_MODEL), lambda v, p, *t: (t[0][v], t[1][v], 0),
                               pipeline_mode=pl.Buffered(1)),
        scratch_shapes=[pltpu.VMEM((MOE_VISIT_ROWS, D_MODEL), f32)],
    )
    return pl.pallas_call(
        _moe_kernel,
        grid_spec=grid_spec,
        out_shape=jax.ShapeDtypeStruct((N_EXPERTS, MOE_CAP, D_MODEL), bf16),
        compiler_params=_cparams("arbitrary", "arbitrary"),
    )(*tables, xn, dest_exp, w1, w3, w2)


def _combine_kernel(t0, nwin, h_ref, dest_ref, cmb_ref, g_ref, *rest):
    ywin = rest[:2 * N_EXPERTS]
    yp_ref, ys_ref, acc_ref = rest[2 * N_EXPERTS:]
    c = pl.program_id(0)
    acc_ref[...] = h_ref[...]
    dest = dest_ref[...]
    cmb = cmb_ref[...]
    lane = lax.broadcasted_iota(jnp.int32, (MOE_CHUNK, MOE_CHUNK), 1)
    for e in range(N_EXPERTS):
        for w in range(2):
            @pl.when(nwin[e * N_CHUNKS + c] > w)
            def _():
                rel = dest[:, e:e + 1] - MOE_CHUNK * (t0[e * N_CHUNKS + c] + w)
                onehot = jnp.where(rel == lane, 1.0, 0.0).astype(bf16)
                got = jnp.dot(onehot, ywin[2 * e + w][...], preferred_element_type=f32)
                acc_ref[...] += cmb[:, e:e + 1] * got

    n = _rms(acc_ref[...], g_ref[...])
    n_prompt_chunks = N_PROMPT_ROWS // MOE_CHUNK

    @pl.when(c < n_prompt_chunks)
    def _():
        yp_ref[...] = n

    @pl.when(c == n_prompt_chunks)
    def _():
        ys_ref[...] = n[:DEC_BATCH]


def _moe_combine(y_sorted, win_tables, h, dest_tok, cmb, g_final):
    t0, t0c, t1c, nwin = win_tables
    row = lambda c, *t: (c, 0)
    n_prompt_chunks = N_PROMPT_ROWS // MOE_CHUNK
    wins = []
    for e in range(N_EXPERTS):
        wins.append(pl.BlockSpec((None, MOE_CHUNK, D_MODEL), lambda c, a, b, *t, e=e: (e, a[e * N_CHUNKS + c], 0)))
        wins.append(pl.BlockSpec((None, MOE_CHUNK, D_MODEL), lambda c, a, b, *t, e=e: (e, b[e * N_CHUNKS + c], 0)))
    grid_spec = pltpu.PrefetchScalarGridSpec(
        num_scalar_prefetch=4,
        grid=(N_CHUNKS,),
        in_specs=[pl.BlockSpec((MOE_CHUNK, D_MODEL), row),
                  pl.BlockSpec((MOE_CHUNK, N_EXPERTS), row),
                  pl.BlockSpec((MOE_CHUNK, N_EXPERTS), row),
                  pl.BlockSpec((1, D_MODEL), lambda c, *t: (0, 0))] + wins,
        out_specs=[pl.BlockSpec((MOE_CHUNK, D_MODEL), lambda c, *t: (jnp.minimum(c, n_prompt_chunks - 1), 0)),
                   pl.BlockSpec((DEC_BATCH, D_MODEL), lambda c, *t: (0, 0))],
        scratch_shapes=[pltpu.VMEM((MOE_CHUNK, D_MODEL), f32)],
    )

    def body(t0c_ref, t1c_ref, t0_ref, nwin_ref, *refs):
        _combine_kernel(t0_ref, nwin_ref, *refs)

    return pl.pallas_call(
        body,
        grid_spec=grid_spec,
        out_shape=[jax.ShapeDtypeStruct((N_PROMPT_ROWS, D_MODEL), f32),
                   jax.ShapeDtypeStruct((DEC_BATCH, D_MODEL), f32)],
        compiler_params=_cparams("arbitrary"),
    )(t0c, t1c, t0, nwin, h, dest_tok, cmb, g_final, *([y_sorted] * (2 * N_EXPERTS)))


def _routing_tables(dest_tok):
    i32 = jnp.int32
    cnt = (dest_tok >= 0).astype(i32).reshape(N_CHUNKS, MOE_CHUNK, N_EXPERTS).sum(axis=1)
    csum = jnp.cumsum(cnt, axis=0)
    rank_lo = jnp.concatenate([jnp.zeros((1, N_EXPERTS), i32), csum], axis=0).T
    n_e = csum[-1]
    nv = (n_e + MOE_VISIT_ROWS - 1) // MOE_VISIT_ROWS
    cum_nv = jnp.cumsum(nv)
    total = cum_nv[-1]
    v = jnp.arange(MOE_MAX_VISITS, dtype=i32)
    vv = jnp.minimum(v, jnp.maximum(total - 1, 0))
    vis_e = jnp.minimum(jnp.sum((vv[:, None] >= cum_nv[None, :]).astype(i32), axis=1), N_EXPERTS - 1)
    vis_k = vv - (cum_nv - nv)[vis_e]
    vis_ok = (v < total).astype(i32)
    rows = jnp.clip(n_e[vis_e] - vis_k * MOE_VISIT_ROWS, 0, MOE_VISIT_ROWS)
    sizes = jnp.asarray(MOE_SUB_SIZES, i32)
    n_sub = (rows[:, None] + sizes[None, :] - 1) // sizes[None, :]
    padded = jnp.where(n_sub <= MOE_SUBS_PER_VISIT, n_sub * sizes[None, :], 2 * MOE_VISIT_ROWS)
    vis_cls = jnp.argmin(padded, axis=1).astype(i32)
    vis_nsub = jnp.where(vis_ok == 1, jnp.take_along_axis(n_sub, vis_cls[:, None], axis=1)[:, 0], 0)
    t0 = rank_lo[:, :N_CHUNKS] // MOE_CHUNK
    tmax = jnp.maximum(nv * (MOE_VISIT_ROWS // MOE_CHUNK) - 1, 0)[:, None]
    moe_tables = (vis_e, vis_k, vis_nsub.astype(i32), vis_cls, vis_ok,
                  rank_lo[:, ::MOE_GATHER // MOE_CHUNK].reshape(-1))
    t_last = (rank_lo[:, 1:] - 1) // MOE_CHUNK
    nwin = jnp.where(cnt.T > 0, t_last - t0 + 1, 0)
    chunk_ids = jnp.arange(N_CHUNKS, dtype=i32)[None, :]
    last_two = lax.cummax(jnp.where(nwin == 2, chunk_ids, 0), axis=1)
    t1c = jnp.take_along_axis(jnp.minimum(t0 + 1, tmax), last_two, axis=1)
    win_tables = (t0.reshape(-1), jnp.minimum(t0, tmax).reshape(-1), t1c.reshape(-1), nwin.reshape(-1))
    return moe_tables, win_tables


def _ffn_kernel(x_ref, w1_ref, w3_ref, w2_ref, h_ref, g_ref, ho_ref, no_ref):
    f = pl.program_id(1)

    @pl.when(f == 0)
    def _():
        ho_ref[...] = h_ref[...]

    x = x_ref[...]
    a = jnp.dot(x, w1_ref[...].astype(bf16), preferred_element_type=f32)
    b = jnp.dot(x, w3_ref[...].astype(bf16), preferred_element_type=f32)
    act = (a * jax.nn.sigmoid(a) * b).astype(bf16)
    ho_ref[...] += jnp.dot(act, w2_ref[...].astype(bf16), preferred_element_type=f32)

    @pl.when(f == pl.num_programs(1) - 1)
    def _():
        no_ref[...] = _rms(ho_ref[...], g_ref[...]).astype(no_ref.dtype)


def _ffn(xn, w1, w3, w2, layer_idx, h, g_next, tm=768, tf=512):
    row = lambda i, f: (i, 0)
    once = pl.Buffered(1)
    return pl.pallas_call(
        _ffn_kernel,
        grid=(N_ROWS // tm, D_FF // tf),
        in_specs=[pl.BlockSpec((tm, D_MODEL), row),
                  pl.BlockSpec((None, D_MODEL, tf), lambda i, f: (layer_idx, 0, f)),
                  pl.BlockSpec((None, D_MODEL, tf), lambda i, f: (layer_idx, 0, f)),
                  pl.BlockSpec((None, tf, D_MODEL), lambda i, f: (layer_idx, f, 0)),
                  pl.BlockSpec((tm, D_MODEL), row, pipeline_mode=once),
                  pl.BlockSpec((1, D_MODEL), lambda i, f: (0, 0))],
        out_specs=[pl.BlockSpec((tm, D_MODEL), row, pipeline_mode=once),
                   pl.BlockSpec((tm, D_MODEL), row, pipeline_mode=once)],
        out_shape=[jax.ShapeDtypeStruct((N_ROWS, D_MODEL), f32),
                   jax.ShapeDtypeStruct((N_ROWS, D_MODEL), bf16)],
        compiler_params=_cparams("parallel", "arbitrary"),
    )(xn, w1, w3, w2, h, g_next)


def _rope_patterns(pos):
    half = ROT_DIM // 2
    inv_freq = jnp.power(ROPE_THETA, -jnp.arange(half, dtype=f32) * 2.0 / ROT_DIM)
    ang = pos.astype(f32)[:, None] * inv_freq[None, :]
    cos, sin = jnp.cos(ang), jnp.sin(ang)
    n = pos.shape[0]
    cos_h = jnp.concatenate([cos, cos, jnp.ones((n, HEAD_DIM - ROT_DIM), f32)], axis=1)
    sin_h = jnp.concatenate([-sin, sin, jnp.zeros((n, HEAD_DIM - ROT_DIM), f32)], axis=1)
    return jnp.concatenate([cos_h, cos_h, sin_h, sin_h], axis=1)


def _block_diag_ones():
    i = np.arange(SEG) // HEAD_DIM
    return jnp.asarray((i[:, None] == i[None, :]).astype(np.float32), dtype=bf16)


def kernel(x_prompt, x_sample, cache_k, cache_v, state_conv, meta_tokens, g_mix, w_in, conv_w, attn_sinks,
           g_attn_out, g_conv_out, w_out, g_ffn, dense_w1, dense_w3, dense_w2, moe_router, moe_w1, moe_w3,
           moe_w2, g_final):
    w_cache = cache_k.shape[2]
    tail = jnp.concatenate([x_sample.reshape(DEC_BATCH, D_MODEL), jnp.zeros((META_ROW_IN_BLOCK, D_MODEL), f32),
                            meta_tokens.astype(f32)], axis=0)

    prompt_pos = N_META + jnp.arange(SEQ, dtype=jnp.int32)
    tail_pos = jnp.maximum(jnp.arange(ROW_BLOCK, dtype=jnp.int32) - META_ROW_IN_BLOCK, 0)
    rope_tab = _rope_patterns(jnp.concatenate([prompt_pos, tail_pos]))
    sample_tab = _rope_patterns(jnp.full((8,), PAST_LEN, jnp.int32))
    bd = _block_diag_ones()
    cache_k = cache_k.reshape(DEPTH, DEC_BATCH, w_cache, KV_DIM)
    cache_v = cache_v.reshape(DEPTH, DEC_BATCH, w_cache, KV_DIM)

    h, xn = _norm_rows(x_prompt.reshape(N_PROMPT_ROWS, D_MODEL), tail, g_mix[0:1])
    nk_p, nv_p, nc_p, nk_s, nv_s, nc_s = [], [], [], [], [], []
    for l in range(DEPTH):
        z = _in_proj(xn, w_in, l)
        g_attn = g_attn_out[l:l + 1]
        state_l = state_conv[l].reshape(DEC_BATCH, (CONV_WIDTH - 1) * CONV_DIM)
        an, nk_prompt, nv_prompt, c, u_tail, conv_state = _mixer(z, attn_sinks[l], rope_tab, g_attn, state_l,
                                                                 conv_w[l], g_conv_out[l:l + 1], bd)
        sinks_exp = jnp.repeat(attn_sinks[l].reshape(N_KV_HEADS, GROUP).T, HEAD_DIM, axis=1)
        an_decode, nk, nv = _attn_sample(z, cache_k, cache_v, l, sample_tab, sinks_exp, g_attn, bd)
        j = l // 2
        if l % 2 == 0:
            h, xn = _out_proj(an, an_decode, c, w_out, l, h, g_ffn[l:l + 1])
            h, xn = _ffn(xn, dense_w1, dense_w3, dense_w2, j, h, g_mix[l + 1:l + 2])
        else:
            h, xn, cmb, dest_tok = _out_proj(an, an_decode, c, w_out, l, h, g_ffn[l:l + 1], router=moe_router[j])
            moe_tables, win_tables = _routing_tables(dest_tok)
            dest_exp = dest_tok.T.reshape(N_EXPERTS, N_GATHER, 1, MOE_GATHER)
            y_sorted = _moe_experts(xn, dest_exp, moe_tables, moe_w1, moe_w3, moe_w2, j)
            y_prompt, y_sample = _moe_combine(y_sorted, win_tables, h, dest_tok, cmb, g_final[None])

        nk_p.append(nk_prompt[:BATCH * WINDOW].reshape(BATCH, WINDOW, N_KV_HEADS, HEAD_DIM))
        nv_p.append(nv_prompt[:BATCH * WINDOW].reshape(BATCH, WINDOW, N_KV_HEADS, HEAD_DIM))
        nc_p.append(u_tail[:BATCH * 8].reshape(BATCH, 8, CONV_DIM)[:, 8 - (CONV_WIDTH - 1):])
        nk_s.append(nk.reshape(DEC_BATCH, w_cache, N_KV_HEADS, HEAD_DIM))
        nv_s.append(nv.reshape(DEC_BATCH, w_cache, N_KV_HEADS, HEAD_DIM))
        nc_s.append(conv_state.reshape(DEC_BATCH, CONV_WIDTH - 1, CONV_DIM))

    return (y_prompt.reshape(BATCH, SEQ, D_MODEL), y_sample.reshape(DEC_BATCH, 1, D_MODEL),
            jnp.stack(nk_p), jnp.stack(nv_p), jnp.stack(nc_p), jnp.stack(nk_s), jnp.stack(nv_s), jnp.stack(nc_s))
```

```python
import functools

import jax
import jax.numpy as jnp
import numpy as np
from jax import lax
from jax.experimental import pallas as pl
from jax.experimental.pallas import tpu as pltpu

D_MODEL = 2048
BATCH = 4
SEQ = 2048
DEPTH = 2
DEC_BATCH = 128
PAST_LEN = 8192
HEAD_DIM = 64
ATTN_DIM = D_MODEL // 2
N_Q_HEADS = ATTN_DIM // HEAD_DIM
N_KV_HEADS = N_Q_HEADS // 4
GROUP = N_Q_HEADS // N_KV_HEADS
KV_DIM = N_KV_HEADS * HEAD_DIM
CONV_DIM = D_MODEL - ATTN_DIM
CONV_WIDTH = 3
IN_DIM = ATTN_DIM + 2 * KV_DIM + 3 * CONV_DIM
WINDOW = 128
ROPE_THETA = 500000.0
ROT_DIM = HEAD_DIM // 4
N_META = 16
D_FF = 7 * D_MODEL // 2
N_EXPERTS = 8
EPS = 1e-5

ROW_BLOCK = 128
N_PROMPT_ROWS = BATCH * SEQ
SAMPLE_ROW0 = N_PROMPT_ROWS
TAIL_ROW0 = SAMPLE_ROW0 + DEC_BATCH
N_ROWS = TAIL_ROW0 + ROW_BLOCK
N_ROW_BLOCKS = N_ROWS // ROW_BLOCK
BLOCKS_PER_SEQ = SEQ // ROW_BLOCK
SAMPLE_BLOCK = SAMPLE_ROW0 // ROW_BLOCK
META_BLOCK = TAIL_ROW0 // ROW_BLOCK
META_ROW_IN_BLOCK = ROW_BLOCK - N_META
SEG = 256

VMEM_LIMIT_V7X = 56 * 1024 * 1024

bf16 = jnp.bfloat16
f32 = jnp.float32


def _cparams(*sem):
    return pltpu.CompilerParams(dimension_semantics=sem, vmem_limit_bytes=VMEM_LIMIT_V7X)


def _rms(x, g):
    ms = jnp.mean(x * x, axis=-1, keepdims=True)
    return (x * lax.rsqrt(ms + EPS)) * g


def _segsum(x, bd):
    x1 = x.astype(bf16)
    r1 = x - x1.astype(f32)
    x2 = r1.astype(bf16)
    x3 = (r1 - x2.astype(f32)).astype(bf16)
    d = functools.partial(jnp.dot, preferred_element_type=f32)
    return d(x1, bd) + d(x2, bd) + d(x3, bd)


def _rope(x, cos_t, sin_t):
    n = x.shape[-1]
    lane = lax.broadcasted_iota(jnp.int32, x.shape, x.ndim - 1) % HEAD_DIM
    nxt = pltpu.roll(x, n - ROT_DIM // 2, axis=x.ndim - 1)
    prv = pltpu.roll(x, ROT_DIM // 2, axis=x.ndim - 1)
    return x * cos_t + jnp.where(lane < ROT_DIM // 2, nxt, prv) * sin_t


def _norm_kernel(xp_ref, tail_ref, g_ref, h_ref, o_ref):
    is_tail = pl.program_id(0) == pl.num_programs(0) - 1
    x = jnp.where(is_tail, tail_ref[...], xp_ref[...])
    h_ref[...] = x
    o_ref[...] = _rms(x, g_ref[...]).astype(o_ref.dtype)


def _norm_rows(x_prompt_rows, tail_rows, g):
    tm = tail_rows.shape[0]
    n_prompt_tiles = N_PROMPT_ROWS // tm
    row = lambda i: (i, 0)
    return pl.pallas_call(
        _norm_kernel,
        grid=(N_ROWS // tm,),
        in_specs=[pl.BlockSpec((tm, D_MODEL), lambda i: (jnp.minimum(i, n_prompt_tiles - 1), 0)),
                  pl.BlockSpec((tm, D_MODEL), lambda i: (0, 0)),
                  pl.BlockSpec((1, D_MODEL), lambda i: (0, 0))],
        out_specs=[pl.BlockSpec((tm, D_MODEL), row), pl.BlockSpec((tm, D_MODEL), row)],
        out_shape=[jax.ShapeDtypeStruct((N_ROWS, D_MODEL), f32),
                   jax.ShapeDtypeStruct((N_ROWS, D_MODEL), bf16)],
        compiler_params=_cparams("parallel"),
    )(x_prompt_rows, tail_rows, g)


def _inproj_kernel(x_ref, w_ref, o_ref, wbf_ref):
    @pl.when(pl.program_id(1) == 0)
    def _():
        wbf_ref[...] = w_ref[...].astype(bf16)

    o_ref[...] = jnp.dot(x_ref[...], wbf_ref[...], preferred_element_type=f32)


def _in_proj(xn, w_in, layer, tm=768, tn=1536):
    return pl.pallas_call(
        _inproj_kernel,
        grid=(IN_DIM // tn, N_ROWS // tm),
        in_specs=[pl.BlockSpec((tm, D_MODEL), lambda j, i: (i, 0)),
                  pl.BlockSpec((None, D_MODEL, tn), lambda j, i: (layer, 0, j))],
        out_specs=pl.BlockSpec((tm, tn), lambda j, i: (i, j)),
        out_shape=jax.ShapeDtypeStruct((N_ROWS, IN_DIM), f32),
        scratch_shapes=[pltpu.VMEM((D_MODEL, tn), bf16)],
        compiler_params=_cparams("parallel", "arbitrary"),
    )(xn, w_in)


def _mixer_kernel(sink_ref, q_ref, kc_ref, vc_ref, kp_ref, vp_ref, tc_ref, tp_ref, g_ref,
                  z1_ref, z2_ref, p1_ref, p2_ref, s0_ref, s1_ref, cw_ref, gc_ref, bd_ref,
                  an_ref, nk_ref, nv_ref, c_ref, utail_ref, ns_ref):
    s = pl.program_id(0)
    _conv_block(s == N_ROW_BLOCKS - 1, z1_ref, z2_ref, p1_ref, p2_ref, s0_ref, s1_ref, cw_ref, gc_ref, bd_ref,
                c_ref, utail_ref, ns_ref)
    is_meta = s == N_ROW_BLOCKS - 2
    j = s % BLOCKS_PER_SEQ
    tc = tc_ref[...]
    tp = tp_ref[...]
    q = _rope(q_ref[...] * (HEAD_DIM ** -0.5), jnp.tile(tc[:, :128], (1, 8)), jnp.tile(tc[:, 128:], (1, 8)))
    kc = _rope(kc_ref[...], jnp.tile(tc[:, :128], (1, 2)), jnp.tile(tc[:, 128:], (1, 2)))
    kp = _rope(kp_ref[...], jnp.tile(tp[:, :128], (1, 2)), jnp.tile(tp[:, 128:], (1, 2)))
    nk_ref[...] = kc
    nv_ref[...] = vc_ref[...]

    kall = jnp.concatenate([kp, kc], axis=0).astype(bf16)
    vall = jnp.concatenate([vp_ref[...], vc_ref[...]], axis=0)

    row = lax.broadcasted_iota(jnp.int32, (ROW_BLOCK, 2 * ROW_BLOCK), 0)
    col = lax.broadcasted_iota(jnp.int32, (ROW_BLOCK, 2 * ROW_BLOCK), 1)
    lo_prev = jnp.where(is_meta, ROW_BLOCK, jnp.where(j == 0, META_ROW_IN_BLOCK, 0))
    lo_cur = jnp.where(is_meta, META_ROW_IN_BLOCK, 0)
    ccur = col - ROW_BLOCK
    valid = (((col < ROW_BLOCK) & (col >= row) & (col >= lo_prev))
             | ((ccur >= 0) & (ccur <= row) & (ccur >= lo_cur)))
    bias = jnp.where(valid, 0.0, -jnp.inf)
    bias4 = jnp.concatenate([bias] * GROUP, axis=0)
    half = lax.broadcasted_iota(jnp.int32, (ROW_BLOCK, 128), 1) // HEAD_DIM
    half4 = lax.broadcasted_iota(jnp.int32, (GROUP * ROW_BLOCK, 128), 1) // HEAD_DIM
    halfkv = lax.broadcasted_iota(jnp.int32, (2 * ROW_BLOCK, 128), 1) // HEAD_DIM
    g_all = g_ref[...]

    kvs = range(N_KV_HEADS)
    q4s, sinks4, kpairs, vpairs = [], [], [], []
    for kvh in kvs:
        kside = kvh % 2
        kpairs.append(kall[:, 128 * (kvh // 2):128 * (kvh // 2) + 128])
        vpairs.append(jnp.where(halfkv == kside, vall[:, 128 * (kvh // 2):128 * (kvh // 2) + 128], 1.0).astype(bf16))
        qs, sinks = [], []
        for g in range(GROUP):
            h = GROUP * kvh + g
            x = q[:, 128 * (h // 2):128 * (h // 2) + 128]
            if h % 2 != kside:
                x = pltpu.roll(x, HEAD_DIM, axis=1)
            qs.append(jnp.where(half == kside, x, 0.0))
            sinks.append(jnp.full((ROW_BLOCK, 1), sink_ref[h], f32))
        q4s.append(jnp.concatenate(qs, axis=0).astype(bf16))
        sinks4.append(jnp.concatenate(sinks, axis=0))
    nt = (((1,), (1,)), ((), ()))
    scs = [lax.dot_general(q4s[k], kpairs[k], nt, preferred_element_type=f32) + bias4 for k in kvs]
    mxs = [jnp.maximum(jnp.max(scs[k], axis=-1, keepdims=True), sinks4[k]) for k in kvs]
    ps = [jnp.exp(scs[k] - mxs[k]).astype(bf16) for k in kvs]
    ovs = [jnp.dot(ps[k], vpairs[k], preferred_element_type=f32) for k in kvs]
    dens = [pltpu.roll(ovs[k], HEAD_DIM, axis=1) + jnp.exp(sinks4[k] - mxs[k]) for k in kvs]
    os_ = [ovs[k] * (1.0 / dens[k]) for k in kvs]
    mss = [jnp.sum(jnp.where(half4 == k % 2, os_[k] * os_[k], 0.0), axis=-1, keepdims=True) * (1.0 / HEAD_DIM)
           for k in kvs]
    ons = [os_[k] * lax.rsqrt(mss[k] + EPS) for k in kvs]
    for kvh in kvs:
        for pair in range(2):
            parts = []
            for side in range(2):
                g = 2 * pair + side
                x = ons[kvh][ROW_BLOCK * g:ROW_BLOCK * (g + 1)]
                if side != kvh % 2:
                    x = pltpu.roll(x, HEAD_DIM, axis=1)
                parts.append(x)
            blk = 2 * kvh + pair
            out = jnp.where(half == 0, parts[0], parts[1]) * g_all[:, 128 * blk:128 * blk + 128]
            an_ref[:, 128 * blk:128 * blk + 128] = out.astype(an_ref.dtype)


def _cur_block(s):
    return jnp.where(s == N_ROW_BLOCKS - 2, META_BLOCK, jnp.where(s == N_ROW_BLOCKS - 1, SAMPLE_BLOCK, s))


def _prev_block(s):
    return jnp.where((s % BLOCKS_PER_SEQ == 0) | (s == N_ROW_BLOCKS - 2), META_BLOCK, s - 1)


def _cur_tab(s):
    return jnp.where(s == N_ROW_BLOCKS - 2, BLOCKS_PER_SEQ, s % BLOCKS_PER_SEQ)


def _prev_tab(s):
    return jnp.where((s % BLOCKS_PER_SEQ == 0) | (s == N_ROW_BLOCKS - 2), BLOCKS_PER_SEQ, s % BLOCKS_PER_SEQ - 1)


def _mixer(z, sinks, rope_tab, g_attn, state_l, conv_w_l, g_conv, bd):
    kcol, vcol = ATTN_DIM // KV_DIM, ATTN_DIM // KV_DIM + 1
    wide = 3 * CONV_DIM // 2
    cur = lambda s: (_cur_block(s), 0)
    return pl.pallas_call(
        _mixer_kernel,
        grid=(N_ROW_BLOCKS,),
        in_specs=[pl.BlockSpec(memory_space=pltpu.SMEM),
                  pl.BlockSpec((ROW_BLOCK, ATTN_DIM), cur),
                  pl.BlockSpec((ROW_BLOCK, KV_DIM), lambda s: (_cur_block(s), kcol)),
                  pl.BlockSpec((ROW_BLOCK, KV_DIM), lambda s: (_cur_block(s), vcol)),
                  pl.BlockSpec((ROW_BLOCK, KV_DIM), lambda s: (_prev_block(s), kcol)),
                  pl.BlockSpec((ROW_BLOCK, KV_DIM), lambda s: (_prev_block(s), vcol)),
                  pl.BlockSpec((ROW_BLOCK, 256), lambda s: (_cur_tab(s), 0)),
                  pl.BlockSpec((ROW_BLOCK, 256), lambda s: (_prev_tab(s), 0)),
                  pl.BlockSpec((1, ATTN_DIM), lambda s: (0, 0)),
                  pl.BlockSpec((ROW_BLOCK, wide), lambda s: (_cur_block(s), 1)),
                  pl.BlockSpec((ROW_BLOCK, wide), lambda s: (_cur_block(s), 2)),
                  pl.BlockSpec((8, wide), lambda s: (_conv_prev_block(_cur_block(s)), 1)),
                  pl.BlockSpec((8, wide), lambda s: (_conv_prev_block(_cur_block(s)), 2)),
                  pl.BlockSpec((DEC_BATCH, CONV_DIM), lambda s: (0, 0)),
                  pl.BlockSpec((DEC_BATCH, CONV_DIM), lambda s: (0, 1)),
                  pl.BlockSpec((CONV_WIDTH, CONV_DIM), lambda s: (0, 0)),
                  pl.BlockSpec((1, CONV_DIM), lambda s: (0, 0)),
                  pl.BlockSpec((SEG, SEG), lambda s: (0, 0))],
        out_specs=[pl.BlockSpec((ROW_BLOCK, ATTN_DIM), cur),
                   pl.BlockSpec((WINDOW, KV_DIM), lambda s: (s // BLOCKS_PER_SEQ, 0)),
                   pl.BlockSpec((WINDOW, KV_DIM), lambda s: (s // BLOCKS_PER_SEQ, 0)),
                   pl.BlockSpec((ROW_BLOCK, CONV_DIM), cur),
                   pl.BlockSpec((8, CONV_DIM), lambda s: (s // BLOCKS_PER_SEQ, 0)),
                   pl.BlockSpec((DEC_BATCH, 2 * CONV_DIM), lambda s: (0, 0))],
        out_shape=[jax.ShapeDtypeStruct((N_ROWS, ATTN_DIM), bf16),
                   jax.ShapeDtypeStruct(((BATCH + 1) * WINDOW, KV_DIM), f32),
                   jax.ShapeDtypeStruct(((BATCH + 1) * WINDOW, KV_DIM), f32),
                   jax.ShapeDtypeStruct((N_ROWS, CONV_DIM), bf16),
                   jax.ShapeDtypeStruct(((BATCH + 1) * 8, CONV_DIM), f32),
                   jax.ShapeDtypeStruct((DEC_BATCH, 2 * CONV_DIM), f32)],
        compiler_params=_cparams("arbitrary"),
    )(sinks, z, z, z, z, z, rope_tab, rope_tab, g_attn, z, z, z, z, state_l, state_l, conv_w_l, g_conv, bd)


def _attn_sample_kernel(q_ref, k_ref, v_ref, ck_ref, cv_ref, tab_ref, sink_ref, g_ref, bd_ref,
                        an_ref, nk_ref, nv_ref):
    bt = q_ref.shape[0]
    w = ck_ref.shape[1]
    tab = tab_ref[...]
    q = _rope(q_ref[...], jnp.tile(tab[0:1, :128], (1, 8)), jnp.tile(tab[0:1, 128:], (1, 8)))
    k = _rope(k_ref[...], jnp.tile(tab[0:1, :128], (1, 2)), jnp.tile(tab[0:1, 128:], (1, 2)))
    v = v_ref[...]
    ck = ck_ref[...]
    cv = cv_ref[...]
    bd = bd_ref[...]

    last = lax.broadcasted_iota(jnp.int32, (w, KV_DIM), 0) == w - 1
    for b in range(bt):
        nk_ref[b] = jnp.where(last, k[b:b + 1], pltpu.roll(ck[b], w - 1, axis=0))
        nv_ref[b] = jnp.where(last, v[b:b + 1], pltpu.roll(cv[b], w - 1, axis=0))

    seg = lax.broadcasted_iota(jnp.int32, (bt, SEG), 1) // HEAD_DIM
    scale = HEAD_DIM ** -0.5
    chunks = [jnp.zeros((bt, SEG), f32) for _ in range(N_KV_HEADS)]
    gs = range(GROUP)
    qgs = []
    for g in gs:
        qg = jnp.zeros((bt, SEG), f32)
        for kvh in range(N_KV_HEADS):
            x = q[:, SEG * kvh:SEG * (kvh + 1)]
            sh = (HEAD_DIM * (kvh - g)) % SEG
            if sh:
                x = pltpu.roll(x, sh, axis=1)
            qg = jnp.where(seg == kvh, x, qg)
        qgs.append(qg)
    scs = [(_segsum((ck * qgs[g][:, None, :]).reshape(bt * w, SEG), bd) * scale).reshape(bt, w, SEG) for g in gs]
    sns = [_segsum(qgs[g] * k, bd) * scale for g in gs]
    sinks = [sink_ref[g:g + 1, :] for g in gs]
    mxs = [jnp.maximum(jnp.maximum(jnp.max(scs[g], axis=1), sns[g]), sinks[g]) for g in gs]
    ps = [jnp.exp(scs[g] - mxs[g][:, None, :]) for g in gs]
    pns = [jnp.exp(sns[g] - mxs[g]) for g in gs]
    dens = [jnp.sum(ps[g], axis=1) + pns[g] + jnp.exp(sinks[g] - mxs[g]) for g in gs]
    os_ = [(jnp.sum(ps[g] * cv, axis=1) + pns[g] * v) / dens[g] for g in gs]
    mss = [_segsum(os_[g] * os_[g], bd) * (1.0 / HEAD_DIM) for g in gs]
    for g in gs:
        on = os_[g] * lax.rsqrt(mss[g] + EPS)
        for kvh in range(N_KV_HEADS):
            sh = (HEAD_DIM * (g - kvh)) % SEG
            x = pltpu.roll(on, sh, axis=1) if sh else on
            chunks[kvh] = jnp.where(seg == g, x, chunks[kvh])
    out = jnp.concatenate(chunks, axis=1) * g_ref[...]
    an_ref[...] = out.astype(an_ref.dtype)


def _attn_sample(z, cache_k, cache_v, layer, tab, sinks_exp, g_attn, bd, bt=16):
    w = cache_k.shape[2]
    row0 = SAMPLE_ROW0 // bt
    kcol, vcol = ATTN_DIM // KV_DIM, ATTN_DIM // KV_DIM + 1
    cache_in = pl.BlockSpec((None, bt, w, KV_DIM), lambda i: (layer, i, 0, 0))
    cache_spec = pl.BlockSpec((bt, w, KV_DIM), lambda i: (i, 0, 0))
    return pl.pallas_call(
        _attn_sample_kernel,
        grid=(DEC_BATCH // bt,),
        in_specs=[pl.BlockSpec((bt, ATTN_DIM), lambda i: (row0 + i, 0)),
                  pl.BlockSpec((bt, KV_DIM), lambda i: (row0 + i, kcol)),
                  pl.BlockSpec((bt, KV_DIM), lambda i: (row0 + i, vcol)),
                  cache_in, cache_in,
                  pl.BlockSpec((8, 256), lambda i: (0, 0)),
                  pl.BlockSpec((GROUP, SEG), lambda i: (0, 0)),
                  pl.BlockSpec((1, ATTN_DIM), lambda i: (0, 0)),
                  pl.BlockSpec((SEG, SEG), lambda i: (0, 0))],
        out_specs=[pl.BlockSpec((bt, ATTN_DIM), lambda i: (i, 0)), cache_spec, cache_spec],
        out_shape=[jax.ShapeDtypeStruct((DEC_BATCH, ATTN_DIM), bf16),
                   jax.ShapeDtypeStruct(cache_k.shape[1:], f32),
                   jax.ShapeDtypeStruct(cache_v.shape[1:], f32)],
        compiler_params=_cparams("parallel"),
    )(z, z, z, cache_k, cache_v, tab, sinks_exp, g_attn, bd)


def _conv_block(is_sample, z1_ref, z2_ref, p1_ref, p2_ref, s0_ref, s1_ref, w_ref, g_ref, bd_ref,
                c_ref, utail_ref, ns_ref):
    def split(b1, b2):
        gb = b1[:, :CONV_DIM]
        gc = jnp.concatenate([b1[:, CONV_DIM:], b2[:, :CONV_DIM // 2]], axis=1)
        return gb, gc * b2[:, CONV_DIM // 2:]

    gb, u = split(z1_ref[...], z2_ref[...])
    _, up = split(p1_ref[...], p2_ref[...])
    utail_ref[...] = u[ROW_BLOCK - 8:]
    ns_ref[:, :CONV_DIM] = s1_ref[...]
    ns_ref[:, CONV_DIM:] = u

    row = lax.broadcasted_iota(jnp.int32, u.shape, 0)
    u1 = jnp.where(row == 0, up[7:8], pltpu.roll(u, 1, axis=0))
    u2 = jnp.where(row == 0, up[6:7], jnp.where(row == 1, up[7:8], pltpu.roll(u, 2, axis=0)))
    u1 = jnp.where(is_sample, s1_ref[...], u1)
    u2 = jnp.where(is_sample, s0_ref[...], u2)
    w = w_ref[...]
    y = u2 * w[0:1] + u1 * w[1:2] + u * w[2:3]
    t = gb * y
    bd = bd_ref[...]
    g = g_ref[...]
    cols = [slice(SEG * c, SEG * (c + 1)) for c in range(CONV_DIM // SEG)]
    mss = [_segsum(t[:, cs] * t[:, cs], bd) * (1.0 / HEAD_DIM) for cs in cols]
    for cs, ms in zip(cols, mss):
        c_ref[:, cs] = (t[:, cs] * lax.rsqrt(ms + EPS) * g[:, cs]).astype(c_ref.dtype)


def _conv_prev_block(i):
    last8_of_meta = N_ROWS // 8 - 1
    return jnp.where(i % BLOCKS_PER_SEQ == 0, last8_of_meta, (ROW_BLOCK // 8) * i - 1)


def _outproj_kernel(*refs, routed):
    if routed:
        (a_ref, as_ref, c_ref, w_ref, h_ref, g_ref, r_ref, tri_ref,
         ho_ref, no_ref, cmb_ref, dest_ref, wbf_ref, carry_ref) = refs
    else:
        a_ref, as_ref, c_ref, w_ref, h_ref, g_ref, ho_ref, no_ref, wbf_ref = refs
    i = pl.program_id(0)

    @pl.when(i == 0)
    def _():
        wbf_ref[...] = w_ref[...].astype(bf16)
        if routed:
            carry_ref[...] = jnp.zeros_like(carry_ref)

    tm = a_ref.shape[0]
    off = SAMPLE_ROW0 % tm
    a = a_ref[...]
    pieces = [jnp.zeros((off, ATTN_DIM), a.dtype)] if off else []
    pieces.append(as_ref[...])
    if tm - off - DEC_BATCH:
        pieces.append(jnp.zeros((tm - off - DEC_BATCH, ATTN_DIM), a.dtype))
    row = lax.broadcasted_iota(jnp.int32, a.shape, 0)
    is_decode = (i == SAMPLE_ROW0 // tm) & (row >= off) & (row < off + DEC_BATCH)
    a = jnp.where(is_decode, jnp.concatenate(pieces, axis=0), a)
    acc = jnp.dot(a, wbf_ref[:ATTN_DIM, :], preferred_element_type=f32)
    acc += jnp.dot(c_ref[...], wbf_ref[ATTN_DIM:, :], preferred_element_type=f32)
    hn = h_ref[...] + acc
    ho_ref[...] = hn
    n = _rms(hn, g_ref[...])
    no_ref[...] = n.astype(no_ref.dtype)
    if routed:
        _route_rows(n, i * tm, r_ref, tri_ref, cmb_ref, dest_ref, carry_ref)


def _route_rows(n, row0, r_ref, tri_ref, cmb_ref, dest_ref, carry_ref):
    tm = n.shape[0]
    rt = r_ref[...]
    idx = lax.broadcasted_iota(jnp.int32, (tm, N_EXPERTS), 1)
    logits = jnp.zeros((tm, N_EXPERTS), f32)
    for e in range(N_EXPERTS):
        prod = n * rt[e:e + 1, :]
        part = prod[:, :128]
        for c in range(1, D_MODEL // 128):
            part = part + prod[:, 128 * c:128 * (c + 1)]
        logits = jnp.where(idx == e, jnp.sum(part, axis=-1, keepdims=True), logits)
    m1 = jnp.max(logits, axis=-1, keepdims=True)
    i1 = jnp.min(jnp.where(logits == m1, idx, N_EXPERTS), axis=-1, keepdims=True)
    rest = jnp.where(idx == i1, -jnp.inf, logits)
    m2 = jnp.max(rest, axis=-1, keepdims=True)
    i2 = jnp.min(jnp.where(rest == m2, idx, N_EXPERTS), axis=-1, keepdims=True)
    e2 = jnp.exp(m2 - m1)
    den = 1.0 + e2
    rowid = row0 + lax.broadcasted_iota(jnp.int32, logits.shape, 0)
    real = (rowid < TAIL_ROW0) | (rowid >= N_ROWS - N_META)
    sel = ((idx == i1) | (idx == i2)) & real
    cmb_ref[...] = jnp.where(sel, jnp.where(idx == i1, 1.0 / den, e2 / den), 0.0)
    self = jnp.where(sel, 1.0, 0.0)
    incl = jnp.dot(tri_ref[...], self.astype(bf16), preferred_element_type=f32)
    carry = carry_ref[...]
    dest_ref[...] = jnp.where(sel, incl - self + carry, -1.0).astype(jnp.int32)
    carry_ref[...] = carry + incl[tm - 1:tm, :]


def _out_proj(an, an_decode, c, w_out, layer, h, g_next, router=None, tm=384):
    assert SAMPLE_ROW0 % tm + DEC_BATCH <= tm
    routed = router is not None
    row = lambda i: (i, 0)
    const = lambda i: (0, 0)
    in_specs = [pl.BlockSpec((tm, ATTN_DIM), row),
                pl.BlockSpec((DEC_BATCH, ATTN_DIM), const),
                pl.BlockSpec((tm, CONV_DIM), row),
                pl.BlockSpec((None, D_MODEL, D_MODEL), lambda i: (layer, 0, 0), pipeline_mode=pl.Buffered(1)),
                pl.BlockSpec((tm, D_MODEL), row),
                pl.BlockSpec((1, D_MODEL), const)]
    out_specs = [pl.BlockSpec((tm, D_MODEL), row), pl.BlockSpec((tm, D_MODEL), row)]
    out_shape = [jax.ShapeDtypeStruct((N_ROWS, D_MODEL), f32), jax.ShapeDtypeStruct((N_ROWS, D_MODEL), bf16)]
    scratch = [pltpu.VMEM((D_MODEL, D_MODEL), bf16)]
    args = [an, an_decode, c, w_out, h, g_next]
    if routed:
        tri = jnp.asarray(np.tril(np.ones((tm, tm), np.float32)), dtype=bf16)
        in_specs += [pl.BlockSpec((N_EXPERTS, D_MODEL), const), pl.BlockSpec((tm, tm), const)]
        out_specs += [pl.BlockSpec((tm, N_EXPERTS), row), pl.BlockSpec((tm, N_EXPERTS), row)]
        out_shape += [jax.ShapeDtypeStruct((N_ROWS, N_EXPERTS), f32),
                      jax.ShapeDtypeStruct((N_ROWS, N_EXPERTS), jnp.int32)]
        scratch.append(pltpu.VMEM((1, N_EXPERTS), f32))
        args += [router.T, tri]
    return pl.pallas_call(
        functools.partial(_outproj_kernel, routed=routed),
        grid=(N_ROWS // tm,),
        in_specs=in_specs,
        out_specs=out_specs,
        out_shape=out_shape,
        scratch_shapes=scratch,
        compiler_params=_cparams("arbitrary"),
    )(*args)


MOE_CHUNK = 256
MOE_GATHER = 768
MOE_SUB_SIZES = (640, 704, 736, 768)
MOE_SUBS_PER_VISIT = 3
MOE_VISIT_ROWS = MOE_SUBS_PER_VISIT * MOE_SUB_SIZES[-1]
MOE_CAP = 4 * MOE_VISIT_ROWS
MOE_TF = 256
N_CHUNKS = N_ROWS // MOE_CHUNK
N_GATHER = N_ROWS // MOE_GATHER
N_FSTEPS = D_FF // MOE_TF
MOE_MAX_VISITS = (2 * N_ROWS) // MOE_VISIT_ROWS + N_EXPERTS


def _moe_kernel(vis_e, vis_k, vis_nsub, vis_cls, vis_ok, rlo, xn_ref, dest_ref, w1_ref, w3_ref, w2_ref, y_ref, acc_ref):
    v, p = pl.program_id(0), pl.program_id(1)
    ok = vis_ok[v] == 1
    e = vis_e[v]
    base = vis_k[v] * MOE_VISIT_ROWS

    @pl.when(ok & (p == 0))
    def _():
        y_ref[...] = jnp.zeros_like(y_ref)

    @pl.when(ok & (p < N_GATHER))
    def _():
        lo = jnp.maximum(rlo[e * (N_GATHER + 1) + p] - base, 0)
        hi = jnp.minimum(rlo[e * (N_GATHER + 1) + p + 1] - base, MOE_VISIT_ROWS)

        @pl.when(hi > lo)
        def _():
            dest = dest_ref[...] - base
            win = lax.broadcasted_iota(jnp.int32, (MOE_CHUNK, MOE_GATHER), 0)

            def tile_body(t, carry):
                off = pl.multiple_of(t * MOE_CHUNK, MOE_CHUNK)
                onehot = jnp.where(win == dest - off, 1.0, 0.0).astype(bf16)
                rows = jnp.dot(onehot, xn_ref[...], preferred_element_type=f32)
                y_ref[pl.ds(off, MOE_CHUNK), :] += rows.astype(bf16)
                return carry

            t_first = lo // MOE_CHUNK
            t_last = (hi - 1) // MOE_CHUNK

            @pl.when(t_last - t_first < 2)
            def _():
                off = pl.multiple_of(jnp.minimum(t_first, MOE_VISIT_ROWS // MOE_CHUNK - 2) * MOE_CHUNK, MOE_CHUNK)
                win2 = lax.broadcasted_iota(jnp.int32, (2 * MOE_CHUNK, MOE_GATHER), 0)
                onehot = jnp.where(win2 == dest - off, 1.0, 0.0).astype(bf16)
                rows = jnp.dot(onehot, xn_ref[...], preferred_element_type=f32)
                y_ref[pl.ds(off, 2 * MOE_CHUNK), :] += rows.astype(bf16)

            @pl.when(t_last - t_first >= 2)
            def _():
                lax.fori_loop(t_first, t_last + 1, tile_body, 0)

    @pl.when(ok & (p >= N_GATHER))
    def _():
        f = p - N_GATHER
        w1 = w1_ref[...].astype(bf16)
        w3 = w3_ref[...].astype(bf16)
        w2 = w2_ref[...].astype(bf16)
        for cls, sub in enumerate(MOE_SUB_SIZES):
            @pl.when(vis_cls[v] == cls)
            def _():
                def sub_body(r, carry):
                    rows = pl.ds(pl.multiple_of(r * sub, 16), sub)

                    @pl.when(f == 0)
                    def _():
                        acc_ref[rows, :] = jnp.zeros((sub, D_MODEL), f32)

                    x = y_ref[rows, :]
                    a = jnp.dot(x, w1, preferred_element_type=f32)
                    b = jnp.dot(x, w3, preferred_element_type=f32)
                    act = (a * jax.nn.sigmoid(a) * b).astype(bf16)
                    acc_ref[rows, :] += jnp.dot(act, w2, preferred_element_type=f32)

                    @pl.when(f == N_FSTEPS - 1)
                    def _():
                        y_ref[rows, :] = acc_ref[rows, :].astype(y_ref.dtype)

                    return carry

                lax.fori_loop(0, vis_nsub[v], sub_body, 0)


def _moe_experts(xn, dest_exp, tables, w1, w3, w2, layer_idx):
    def chunk(v, p, ve, vk, vn, vc, vok, rlo):
        return jnp.where(vok[v] == 1, jnp.minimum(p, N_GATHER - 1), N_GATHER - 1)

    def fstep(v, p, ve, vk, vn, vc, vok, rlo):
        return jnp.where(vok[v] == 1, jnp.maximum(p - N_GATHER, 0), N_FSTEPS - 1)

    grid_spec = pltpu.PrefetchScalarGridSpec(
        num_scalar_prefetch=6,
        grid=(MOE_MAX_VISITS, N_GATHER + N_FSTEPS),
        in_specs=[pl.BlockSpec((MOE_GATHER, D_MODEL), lambda v, p, *t: (chunk(v, p, *t), 0)),
                  pl.BlockSpec((None, None, 1, MOE_GATHER), lambda v, p, *t: (t[0][v], chunk(v, p, *t), 0, 0)),
                  pl.BlockSpec((None, None, D_MODEL, MOE_TF), lambda v, p, *t: (layer_idx, t[0][v], 0, fstep(v, p, *t))),
                  pl.BlockSpec((None, None, D_MODEL, MOE_TF), lambda v, p, *t: (layer_idx, t[0][v], 0, fstep(v, p, *t))),
                  pl.BlockSpec((None, None, MOE_TF, D_MODEL), lambda v, p, *t: (layer_idx, t[0][v], fstep(v, p, *t), 0))],
        out_specs=pl.BlockSpec((None, MOE_VISIT_ROWS, D_MODEL), lambda v, p, *t: (t[0][v], t[1][v], 0),
                               pipeline_mode=pl.Buffered(1)),
        scratch_shapes=[pltpu.VMEM((MOE_VISIT_ROWS, D_MODEL), f32)],
    )
    return pl.pallas_call(
        _moe_kernel,
        grid_spec=grid_spec,
        out_shape=jax.ShapeDtypeStruct((N_EXPERTS, MOE_CAP, D_MODEL), bf16),
        compiler_params=_cparams("arbitrary", "arbitrary"),
    )(*tables, xn, dest_exp, w1, w3, w2)


def _combine_kernel(t0, nwin, h_ref, dest_ref, cmb_ref, g_ref, *rest):
    ywin = rest[:2 * N_EXPERTS]
    yp_ref, ys_ref, acc_ref = rest[2 * N_EXPERTS:]
    c = pl.program_id(0)
    dest = dest_ref[...]
    cmb = cmb_ref[...]
    lane = lax.broadcasted_iota(jnp.int32, (MOE_CHUNK, MOE_CHUNK), 1)
    rels = [dest[:, e:e + 1] - MOE_CHUNK * t0[e * N_CHUNKS + c] for e in range(N_EXPERTS)]
    acc = h_ref[...]
    for group in (range(0, N_EXPERTS // 2), range(N_EXPERTS // 2, N_EXPERTS)):
        onehots = [jnp.where(rels[e] == lane, 1.0, 0.0).astype(bf16) for e in group]
        gots = [jnp.dot(oh, ywin[2 * e][...], preferred_element_type=f32) for oh, e in zip(onehots, group)]
        for got, e in zip(gots, group):
            acc = acc + cmb[:, e:e + 1] * got
    acc_ref[...] = acc
    for e in range(N_EXPERTS):
        @pl.when(nwin[e * N_CHUNKS + c] > 1)
        def _():
            onehot = jnp.where(rels[e] - MOE_CHUNK == lane, 1.0, 0.0).astype(bf16)
            got = jnp.dot(onehot, ywin[2 * e + 1][...], preferred_element_type=f32)
            acc_ref[...] += cmb[:, e:e + 1] * got

    n = _rms(acc_ref[...], g_ref[...])
    n_prompt_chunks = N_PROMPT_ROWS // MOE_CHUNK

    @pl.when(c < n_prompt_chunks)
    def _():
        yp_ref[...] = n

    @pl.when(c == n_prompt_chunks)
    def _():
        ys_ref[...] = n[:DEC_BATCH]


def _moe_combine(y_sorted, win_tables, h, dest_tok, cmb, g_final):
    t0, t0c, t1c, nwin, safe_e = win_tables
    row = lambda c, *t: (c, 0)
    n_prompt_chunks = N_PROMPT_ROWS // MOE_CHUNK
    wins = []
    for e in range(N_EXPERTS):
        wins.append(pl.BlockSpec((None, MOE_CHUNK, D_MODEL),
                                 lambda c, a, b, *t, e=e: (t[2][e], a[e * N_CHUNKS + c], 0)))
        wins.append(pl.BlockSpec((None, MOE_CHUNK, D_MODEL),
                                 lambda c, a, b, *t, e=e: (t[2][e], b[e * N_CHUNKS + c], 0), pipeline_mode=pl.Buffered(1)))
    grid_spec = pltpu.PrefetchScalarGridSpec(
        num_scalar_prefetch=5,
        grid=(N_CHUNKS,),
        in_specs=[pl.BlockSpec((MOE_CHUNK, D_MODEL), row),
                  pl.BlockSpec((MOE_CHUNK, N_EXPERTS), row),
                  pl.BlockSpec((MOE_CHUNK, N_EXPERTS), row),
                  pl.BlockSpec((1, D_MODEL), lambda c, *t: (0, 0))] + wins,
        out_specs=[pl.BlockSpec((MOE_CHUNK, D_MODEL), lambda c, *t: (jnp.minimum(c, n_prompt_chunks - 1), 0)),
                   pl.BlockSpec((DEC_BATCH, D_MODEL), lambda c, *t: (0, 0))],
        scratch_shapes=[pltpu.VMEM((MOE_CHUNK, D_MODEL), f32)],
    )

    def body(t0c_ref, t1c_ref, t0_ref, nwin_ref, safe_ref, *refs):
        _combine_kernel(t0_ref, nwin_ref, *refs)

    return pl.pallas_call(
        body,
        grid_spec=grid_spec,
        out_shape=[jax.ShapeDtypeStruct((N_PROMPT_ROWS, D_MODEL), f32),
                   jax.ShapeDtypeStruct((DEC_BATCH, D_MODEL), f32)],
        compiler_params=_cparams("arbitrary"),
    )(t0c, t1c, t0, nwin, safe_e, h, dest_tok, cmb, g_final, *([y_sorted] * (2 * N_EXPERTS)))


def _routing_tables(dest_tok):
    i32 = jnp.int32
    cnt = (dest_tok >= 0).astype(i32).reshape(N_CHUNKS, MOE_CHUNK, N_EXPERTS).sum(axis=1)
    csum = jnp.cumsum(cnt, axis=0)
    rank_lo = jnp.concatenate([jnp.zeros((1, N_EXPERTS), i32), csum], axis=0).T
    n_e = csum[-1]
    nv = (n_e + MOE_VISIT_ROWS - 1) // MOE_VISIT_ROWS
    cum_nv = jnp.cumsum(nv)
    total = cum_nv[-1]
    v = jnp.arange(MOE_MAX_VISITS, dtype=i32)
    vv = jnp.minimum(v, jnp.maximum(total - 1, 0))
    vis_e = jnp.minimum(jnp.sum((vv[:, None] >= cum_nv[None, :]).astype(i32), axis=1), N_EXPERTS - 1)
    vis_k = vv - (cum_nv - nv)[vis_e]
    vis_ok = (v < total).astype(i32)
    rows = jnp.clip(n_e[vis_e] - vis_k * MOE_VISIT_ROWS, 0, MOE_VISIT_ROWS)
    sizes = jnp.asarray(MOE_SUB_SIZES, i32)
    n_sub = (rows[:, None] + sizes[None, :] - 1) // sizes[None, :]
    padded = jnp.where(n_sub <= MOE_SUBS_PER_VISIT, n_sub * sizes[None, :], 2 * MOE_VISIT_ROWS)
    vis_cls = jnp.argmin(padded, axis=1).astype(i32)
    vis_nsub = jnp.where(vis_ok == 1, jnp.take_along_axis(n_sub, vis_cls[:, None], axis=1)[:, 0], 0)
    t0 = rank_lo[:, :N_CHUNKS] // MOE_CHUNK
    tmax = jnp.maximum(nv * (MOE_VISIT_ROWS // MOE_CHUNK) - 1, 0)[:, None]
    moe_tables = (vis_e, vis_k, vis_nsub.astype(i32), vis_cls, vis_ok,
                  rank_lo[:, ::MOE_GATHER // MOE_CHUNK].reshape(-1))
    t_last = (rank_lo[:, 1:] - 1) // MOE_CHUNK
    nwin = jnp.where(cnt.T > 0, t_last - t0 + 1, 0)
    chunk_ids = jnp.arange(N_CHUNKS, dtype=i32)[None, :]
    last_two = lax.cummax(jnp.where(nwin == 2, chunk_ids, 0), axis=1)
    t1c = jnp.take_along_axis(jnp.minimum(t0 + 1, tmax), last_two, axis=1)
    experts = jnp.arange(N_EXPERTS, dtype=i32)
    safe_e = jnp.where(n_e > 0, experts, jnp.argmax(n_e).astype(i32))
    t0c = jnp.where((n_e > 0)[:, None], jnp.minimum(t0, tmax), 0)
    t1c = jnp.where((n_e > 0)[:, None], t1c, 0)
    win_tables = (t0.reshape(-1), t0c.reshape(-1), t1c.reshape(-1), nwin.reshape(-1), safe_e)
    return moe_tables, win_tables


def _ffn_kernel(x_ref, w1_ref, w3_ref, w2_ref, h_ref, g_ref, ho_ref, no_ref):
    f = pl.program_id(1)

    @pl.when(f == 0)
    def _():
        ho_ref[...] = h_ref[...]

    x = x_ref[...]
    a = jnp.dot(x, w1_ref[...].astype(bf16), preferred_element_type=f32)
    b = jnp.dot(x, w3_ref[...].astype(bf16), preferred_element_type=f32)
    act = (a * jax.nn.sigmoid(a) * b).astype(bf16)
    ho_ref[...] += jnp.dot(act, w2_ref[...].astype(bf16), preferred_element_type=f32)

    @pl.when(f == pl.num_programs(1) - 1)
    def _():
        no_ref[...] = _rms(ho_ref[...], g_ref[...]).astype(no_ref.dtype)


def _ffn(xn, w1, w3, w2, layer_idx, h, g_next, tm=768, tf=512):
    row = lambda i, f: (i, 0)
    once = pl.Buffered(1)
    return pl.pallas_call(
        _ffn_kernel,
        grid=(N_ROWS // tm, D_FF // tf),
        in_specs=[pl.BlockSpec((tm, D_MODEL), row),
                  pl.BlockSpec((None, D_MODEL, tf), lambda i, f: (layer_idx, 0, f)),
                  pl.BlockSpec((None, D_MODEL, tf), lambda i, f: (layer_idx, 0, f)),
                  pl.BlockSpec((None, tf, D_MODEL), lambda i, f: (layer_idx, f, 0)),
                  pl.BlockSpec((tm, D_MODEL), row, pipeline_mode=once),
                  pl.BlockSpec((1, D_MODEL), lambda i, f: (0, 0))],
        out_specs=[pl.BlockSpec((tm, D_MODEL), row, pipeline_mode=once),
                   pl.BlockSpec((tm, D_MODEL), row, pipeline_mode=once)],
        out_shape=[jax.ShapeDtypeStruct((N_ROWS, D_MODEL), f32),
                   jax.ShapeDtypeStruct((N_ROWS, D_MODEL), bf16)],
        compiler_params=_cparams("parallel", "arbitrary"),
    )(xn, w1, w3, w2, h, g_next)


def _rope_patterns(pos):
    half = ROT_DIM // 2
    inv_freq = jnp.power(ROPE_THETA, -jnp.arange(half, dtype=f32) * 2.0 / ROT_DIM)
    ang = pos.astype(f32)[:, None] * inv_freq[None, :]
    cos, sin = jnp.cos(ang), jnp.sin(ang)
    n = pos.shape[0]
    cos_h = jnp.concatenate([cos, cos, jnp.ones((n, HEAD_DIM - ROT_DIM), f32)], axis=1)
    sin_h = jnp.concatenate([-sin, sin, jnp.zeros((n, HEAD_DIM - ROT_DIM), f32)], axis=1)
    return jnp.concatenate([cos_h, cos_h, sin_h, sin_h], axis=1)


def _block_diag_ones():
    i = np.arange(SEG) // HEAD_DIM
    return jnp.asarray((i[:, None] == i[None, :]).astype(np.float32), dtype=bf16)


def kernel(x_prompt, x_sample, cache_k, cache_v, state_conv, meta_tokens, g_mix, w_in, conv_w, attn_sinks,
           g_attn_out, g_conv_out, w_out, g_ffn, dense_w1, dense_w3, dense_w2, moe_router, moe_w1, moe_w3,
           moe_w2, g_final):
    w_cache = cache_k.shape[2]
    tail = jnp.concatenate([x_sample.reshape(DEC_BATCH, D_MODEL), jnp.zeros((META_ROW_IN_BLOCK, D_MODEL), f32),
                            meta_tokens.astype(f32)], axis=0)

    prompt_pos = N_META + jnp.arange(SEQ, dtype=jnp.int32)
    tail_pos = jnp.maximum(jnp.arange(ROW_BLOCK, dtype=jnp.int32) - META_ROW_IN_BLOCK, 0)
    rope_tab = _rope_patterns(jnp.concatenate([prompt_pos, tail_pos]))
    sample_tab = _rope_patterns(jnp.full((8,), PAST_LEN, jnp.int32))
    bd = _block_diag_ones()
    cache_k = cache_k.reshape(DEPTH, DEC_BATCH, w_cache, KV_DIM)
    cache_v = cache_v.reshape(DEPTH, DEC_BATCH, w_cache, KV_DIM)

    h, xn = _norm_rows(x_prompt.reshape(N_PROMPT_ROWS, D_MODEL), tail, g_mix[0:1])
    nk_p, nv_p, nc_p, nk_s, nv_s, nc_s = [], [], [], [], [], []
    for l in range(DEPTH):
        z = _in_proj(xn, w_in, l)
        g_attn = g_attn_out[l:l + 1]
        state_l = state_conv[l].reshape(DEC_BATCH, (CONV_WIDTH - 1) * CONV_DIM)
        an, nk_prompt, nv_prompt, c, u_tail, conv_state = _mixer(z, attn_sinks[l], rope_tab, g_attn, state_l,
                                                                 conv_w[l], g_conv_out[l:l + 1], bd)
        sinks_exp = jnp.repeat(attn_sinks[l].reshape(N_KV_HEADS, GROUP).T, HEAD_DIM, axis=1)
        an_decode, nk, nv = _attn_sample(z, cache_k, cache_v, l, sample_tab, sinks_exp, g_attn, bd)
        j = l // 2
        if l % 2 == 0:
            h, xn = _out_proj(an, an_decode, c, w_out, l, h, g_ffn[l:l + 1])
            h, xn = _ffn(xn, dense_w1, dense_w3, dense_w2, j, h, g_mix[l + 1:l + 2])
        else:
            h, xn, cmb, dest_tok = _out_proj(an, an_decode, c, w_out, l, h, g_ffn[l:l + 1], router=moe_router[j])
            moe_tables, win_tables = _routing_tables(dest_tok)
            dest_exp = dest_tok.T.reshape(N_EXPERTS, N_GATHER, 1, MOE_GATHER)
            y_sorted = _moe_experts(xn, dest_exp, moe_tables, moe_w1, moe_w3, moe_w2, j)
            y_prompt, y_sample = _moe_combine(y_sorted, win_tables, h, dest_tok, cmb, g_final[None])

        nk_p.append(nk_prompt[:BATCH * WINDOW].reshape(BATCH, WINDOW, N_KV_HEADS, HEAD_DIM))
        nv_p.append(nv_prompt[:BATCH * WINDOW].reshape(BATCH, WINDOW, N_KV_HEADS, HEAD_DIM))
        nc_p.append(u_tail[:BATCH * 8].reshape(BATCH, 8, CONV_DIM)[:, 8 - (CONV_WIDTH - 1):])
        nk_s.append(nk.reshape(DEC_BATCH, w_cache, N_KV_HEADS, HEAD_DIM))
        nv_s.append(nv.reshape(DEC_BATCH, w_cache, N_KV_HEADS, HEAD_DIM))
        nc_s.append(conv_state.reshape(DEC_BATCH, CONV_WIDTH - 1, CONV_DIM))

    return (y_prompt.reshape(BATCH, SEQ, D_MODEL), y_sample.reshape(DEC_BATCH, 1, D_MODEL),
            jnp.stack(nk_p), jnp.stack(nv_p), jnp.stack(nc_p), jnp.stack(nk_s), jnp.stack(nv_s), jnp.stack(nc_s))
```

```python
import functools

import jax
import jax.numpy as jnp
import numpy as np
from jax import lax
from jax.experimental import pallas as pl
from jax.experimental.pallas import tpu as pltpu

D_MODEL = 2048
BATCH = 4
SEQ = 2048
DEPTH = 2
DEC_BATCH = 128
PAST_LEN = 8192
HEAD_DIM = 64
ATTN_DIM = D_MODEL // 2
N_Q_HEADS = ATTN_DIM // HEAD_DIM
N_KV_HEADS = N_Q_HEADS // 4
GROUP = N_Q_HEADS // N_KV_HEADS
KV_DIM = N_KV_HEADS * HEAD_DIM
CONV_DIM = D_MODEL - ATTN_DIM
CONV_WIDTH = 3
IN_DIM = ATTN_DIM + 2 * KV_DIM + 3 * CONV_DIM
WINDOW = 128
ROPE_THETA = 500000.0
ROT_DIM = HEAD_DIM // 4
N_META = 16
D_FF = 7 * D_MODEL // 2
N_EXPERTS = 8
EPS = 1e-5

ROW_BLOCK = 128
N_PROMPT_ROWS = BATCH * SEQ
SAMPLE_ROW0 = N_PROMPT_ROWS
TAIL_ROW0 = SAMPLE_ROW0 + DEC_BATCH
N_ROWS = TAIL_ROW0 + ROW_BLOCK
N_ROW_BLOCKS = N_ROWS // ROW_BLOCK
BLOCKS_PER_SEQ = SEQ // ROW_BLOCK
SAMPLE_BLOCK = SAMPLE_ROW0 // ROW_BLOCK
META_BLOCK = TAIL_ROW0 // ROW_BLOCK
META_ROW_IN_BLOCK = ROW_BLOCK - N_META
SEG = 256

VMEM_LIMIT_V7X = 56 * 1024 * 1024

bf16 = jnp.bfloat16
f32 = jnp.float32


def _cparams(*sem):
    return pltpu.CompilerParams(dimension_semantics=sem, vmem_limit_bytes=VMEM_LIMIT_V7X)


def _rms(x, g):
    ms = jnp.mean(x * x, axis=-1, keepdims=True)
    return (x * lax.rsqrt(ms + EPS)) * g


def _segsum(x, bd):
    x1 = x.astype(bf16)
    r1 = x - x1.astype(f32)
    x2 = r1.astype(bf16)
    x3 = (r1 - x2.astype(f32)).astype(bf16)
    d = functools.partial(jnp.dot, preferred_element_type=f32)
    return d(x1, bd) + d(x2, bd) + d(x3, bd)


def _rope(x, cos_t, sin_t):
    n = x.shape[-1]
    lane = lax.broadcasted_iota(jnp.int32, x.shape, x.ndim - 1) % HEAD_DIM
    nxt = pltpu.roll(x, n - ROT_DIM // 2, axis=x.ndim - 1)
    prv = pltpu.roll(x, ROT_DIM // 2, axis=x.ndim - 1)
    return x * cos_t + jnp.where(lane < ROT_DIM // 2, nxt, prv) * sin_t


def _norm_kernel(xp_ref, tail_ref, g_ref, h_ref, o_ref):
    is_tail = pl.program_id(0) == pl.num_programs(0) - 1

    def emit(src_ref):
        x = src_ref[...]
        h_ref[...] = x
        o_ref[...] = _rms(x, g_ref[...]).astype(o_ref.dtype)

    pl.when(is_tail)(lambda: emit(tail_ref))
    pl.when(jnp.logical_not(is_tail))(lambda: emit(xp_ref))


def _norm_rows(x_prompt_rows, tail_rows, g):
    tm = tail_rows.shape[0]
    n_prompt_tiles = N_PROMPT_ROWS // tm
    row = lambda i: (i, 0)
    return pl.pallas_call(
        _norm_kernel,
        grid=(N_ROWS // tm,),
        in_specs=[pl.BlockSpec((tm, D_MODEL), lambda i: (jnp.minimum(i, n_prompt_tiles - 1), 0)),
                  pl.BlockSpec((tm, D_MODEL), lambda i: (0, 0)),
                  pl.BlockSpec((1, D_MODEL), lambda i: (0, 0))],
        out_specs=[pl.BlockSpec((tm, D_MODEL), row), pl.BlockSpec((tm, D_MODEL), row)],
        out_shape=[jax.ShapeDtypeStruct((N_ROWS, D_MODEL), f32),
                   jax.ShapeDtypeStruct((N_ROWS, D_MODEL), bf16)],
        compiler_params=_cparams("parallel"),
    )(x_prompt_rows, tail_rows, g)


def _inproj_kernel(x_ref, w_ref, o_ref, wbf_ref):
    @pl.when(pl.program_id(1) == 0)
    def _():
        wbf_ref[...] = w_ref[...].astype(bf16)

    o_ref[...] = jnp.dot(x_ref[...], wbf_ref[...], preferred_element_type=f32)


def _in_proj(xn, w_in, layer, tm=768, tn=1536):
    return pl.pallas_call(
        _inproj_kernel,
        grid=(IN_DIM // tn, N_ROWS // tm),
        in_specs=[pl.BlockSpec((tm, D_MODEL), lambda j, i: (i, 0)),
                  pl.BlockSpec((None, D_MODEL, tn), lambda j, i: (layer, 0, j))],
        out_specs=pl.BlockSpec((tm, tn), lambda j, i: (i, j)),
        out_shape=jax.ShapeDtypeStruct((N_ROWS, IN_DIM), f32),
        scratch_shapes=[pltpu.VMEM((D_MODEL, tn), bf16)],
        compiler_params=_cparams("parallel", "arbitrary"),
    )(xn, w_in)


def _mixer_kernel(sink_ref, q_ref, kc_ref, vc_ref, kp_ref, vp_ref, tc_ref, tp_ref, g_ref,
                  z1_ref, z2_ref, p1_ref, p2_ref, s0_ref, s1_ref, cw_ref, gc_ref, bd_ref,
                  an_ref, nk_ref, nv_ref, c_ref, utail_ref, ns_ref):
    s = pl.program_id(0)
    _conv_block(s == N_ROW_BLOCKS - 1, z1_ref, z2_ref, p1_ref, p2_ref, s0_ref, s1_ref, cw_ref, gc_ref, bd_ref,
                c_ref, utail_ref, ns_ref)
    is_meta = s == N_ROW_BLOCKS - 2
    j = s % BLOCKS_PER_SEQ
    tc = tc_ref[...]
    tp = tp_ref[...]
    q = _rope(q_ref[...] * (HEAD_DIM ** -0.5), jnp.tile(tc[:, :128], (1, 8)), jnp.tile(tc[:, 128:], (1, 8)))
    kc = _rope(kc_ref[...], jnp.tile(tc[:, :128], (1, 2)), jnp.tile(tc[:, 128:], (1, 2)))
    kp = _rope(kp_ref[...], jnp.tile(tp[:, :128], (1, 2)), jnp.tile(tp[:, 128:], (1, 2)))
    nk_ref[...] = kc
    nv_ref[...] = vc_ref[...]

    kall = jnp.concatenate([kp, kc], axis=0).astype(bf16)
    vall = jnp.concatenate([vp_ref[...], vc_ref[...]], axis=0)

    row = lax.broadcasted_iota(jnp.int32, (ROW_BLOCK, 2 * ROW_BLOCK), 0)
    col = lax.broadcasted_iota(jnp.int32, (ROW_BLOCK, 2 * ROW_BLOCK), 1)
    lo_prev = jnp.where(is_meta, ROW_BLOCK, jnp.where(j == 0, META_ROW_IN_BLOCK, 0))
    lo_cur = jnp.where(is_meta, META_ROW_IN_BLOCK, 0)
    ccur = col - ROW_BLOCK
    valid = (((col < ROW_BLOCK) & (col >= row) & (col >= lo_prev))
             | ((ccur >= 0) & (ccur <= row) & (ccur >= lo_cur)))
    bias = jnp.where(valid, 0.0, -jnp.inf)
    bias4 = jnp.concatenate([bias] * GROUP, axis=0)
    half = lax.broadcasted_iota(jnp.int32, (ROW_BLOCK, 128), 1) // HEAD_DIM
    half4 = lax.broadcasted_iota(jnp.int32, (GROUP * ROW_BLOCK, 128), 1) // HEAD_DIM
    halfkv = lax.broadcasted_iota(jnp.int32, (2 * ROW_BLOCK, 128), 1) // HEAD_DIM
    g_all = g_ref[...]

    kvs = range(N_KV_HEADS)
    q4s, sinks4, kpairs, vpairs = [], [], [], []
    for kvh in kvs:
        kside = kvh % 2
        kpairs.append(kall[:, 128 * (kvh // 2):128 * (kvh // 2) + 128])
        vpairs.append(jnp.where(halfkv == kside, vall[:, 128 * (kvh // 2):128 * (kvh // 2) + 128], 1.0).astype(bf16))
        qs, sinks = [], []
        for g in range(GROUP):
            h = GROUP * kvh + g
            x = q[:, 128 * (h // 2):128 * (h // 2) + 128]
            if h % 2 != kside:
                x = pltpu.roll(x, HEAD_DIM, axis=1)
            qs.append(jnp.where(half == kside, x, 0.0))
            sinks.append(jnp.full((ROW_BLOCK, 1), sink_ref[h], f32))
        q4s.append(jnp.concatenate(qs, axis=0).astype(bf16))
        sinks4.append(jnp.concatenate(sinks, axis=0))
    nt = (((1,), (1,)), ((), ()))
    scs = [lax.dot_general(q4s[k], kpairs[k], nt, preferred_element_type=f32) + bias4 for k in kvs]
    mxs = [jnp.maximum(jnp.max(scs[k], axis=-1, keepdims=True), sinks4[k]) for k in kvs]
    ps = [jnp.exp(scs[k] - mxs[k]).astype(bf16) for k in kvs]
    ovs = [jnp.dot(ps[k], vpairs[k], preferred_element_type=f32) for k in kvs]
    dens = [pltpu.roll(ovs[k], HEAD_DIM, axis=1) + jnp.exp(sinks4[k] - mxs[k]) for k in kvs]
    os_ = [ovs[k] * (1.0 / dens[k]) for k in kvs]
    mss = [jnp.sum(jnp.where(half4 == k % 2, os_[k] * os_[k], 0.0), axis=-1, keepdims=True) * (1.0 / HEAD_DIM)
           for k in kvs]
    ons = [os_[k] * lax.rsqrt(mss[k] + EPS) for k in kvs]
    for kvh in kvs:
        for pair in range(2):
            parts = []
            for side in range(2):
                g = 2 * pair + side
                x = ons[kvh][ROW_BLOCK * g:ROW_BLOCK * (g + 1)]
                if side != kvh % 2:
                    x = pltpu.roll(x, HEAD_DIM, axis=1)
                parts.append(x)
            blk = 2 * kvh + pair
            out = jnp.where(half == 0, parts[0], parts[1]) * g_all[:, 128 * blk:128 * blk + 128]
            an_ref[:, 128 * blk:128 * blk + 128] = out.astype(an_ref.dtype)


def _cur_block(s):
    return jnp.where(s == N_ROW_BLOCKS - 2, META_BLOCK, jnp.where(s == N_ROW_BLOCKS - 1, SAMPLE_BLOCK, s))


def _prev_block(s):
    return jnp.where((s % BLOCKS_PER_SEQ == 0) | (s == N_ROW_BLOCKS - 2), META_BLOCK, s - 1)


def _cur_tab(s):
    return jnp.where(s == N_ROW_BLOCKS - 2, BLOCKS_PER_SEQ, s % BLOCKS_PER_SEQ)


def _prev_tab(s):
    return jnp.where((s % BLOCKS_PER_SEQ == 0) | (s == N_ROW_BLOCKS - 2), BLOCKS_PER_SEQ, s % BLOCKS_PER_SEQ - 1)


def _mixer(z, sinks, rope_tab, g_attn, state_l, conv_w_l, g_conv, bd):
    kcol, vcol = ATTN_DIM // KV_DIM, ATTN_DIM // KV_DIM + 1
    wide = 3 * CONV_DIM // 2
    cur = lambda s: (_cur_block(s), 0)
    return pl.pallas_call(
        _mixer_kernel,
        grid=(N_ROW_BLOCKS,),
        in_specs=[pl.BlockSpec(memory_space=pltpu.SMEM),
                  pl.BlockSpec((ROW_BLOCK, ATTN_DIM), cur),
                  pl.BlockSpec((ROW_BLOCK, KV_DIM), lambda s: (_cur_block(s), kcol)),
                  pl.BlockSpec((ROW_BLOCK, KV_DIM), lambda s: (_cur_block(s), vcol)),
                  pl.BlockSpec((ROW_BLOCK, KV_DIM), lambda s: (_prev_block(s), kcol)),
                  pl.BlockSpec((ROW_BLOCK, KV_DIM), lambda s: (_prev_block(s), vcol)),
                  pl.BlockSpec((ROW_BLOCK, 256), lambda s: (_cur_tab(s), 0)),
                  pl.BlockSpec((ROW_BLOCK, 256), lambda s: (_prev_tab(s), 0)),
                  pl.BlockSpec((1, ATTN_DIM), lambda s: (0, 0)),
                  pl.BlockSpec((ROW_BLOCK, wide), lambda s: (_cur_block(s), 1)),
                  pl.BlockSpec((ROW_BLOCK, wide), lambda s: (_cur_block(s), 2)),
                  pl.BlockSpec((8, wide), lambda s: (_conv_prev_block(_cur_block(s)), 1)),
                  pl.BlockSpec((8, wide), lambda s: (_conv_prev_block(_cur_block(s)), 2)),
                  pl.BlockSpec((DEC_BATCH, CONV_DIM), lambda s: (0, 0)),
                  pl.BlockSpec((DEC_BATCH, CONV_DIM), lambda s: (0, 1)),
                  pl.BlockSpec((CONV_WIDTH, CONV_DIM), lambda s: (0, 0)),
                  pl.BlockSpec((1, CONV_DIM), lambda s: (0, 0)),
                  pl.BlockSpec((SEG, SEG), lambda s: (0, 0))],
        out_specs=[pl.BlockSpec((ROW_BLOCK, ATTN_DIM), cur),
                   pl.BlockSpec((WINDOW, KV_DIM), lambda s: (s // BLOCKS_PER_SEQ, 0)),
                   pl.BlockSpec((WINDOW, KV_DIM), lambda s: (s // BLOCKS_PER_SEQ, 0)),
                   pl.BlockSpec((ROW_BLOCK, CONV_DIM), cur),
                   pl.BlockSpec((8, CONV_DIM), lambda s: (s // BLOCKS_PER_SEQ, 0)),
                   pl.BlockSpec((DEC_BATCH, 2 * CONV_DIM), lambda s: (0, 0))],
        out_shape=[jax.ShapeDtypeStruct((N_ROWS, ATTN_DIM), bf16),
                   jax.ShapeDtypeStruct(((BATCH + 1) * WINDOW, KV_DIM), f32),
                   jax.ShapeDtypeStruct(((BATCH + 1) * WINDOW, KV_DIM), f32),
                   jax.ShapeDtypeStruct((N_ROWS, CONV_DIM), bf16),
                   jax.ShapeDtypeStruct(((BATCH + 1) * 8, CONV_DIM), f32),
                   jax.ShapeDtypeStruct((DEC_BATCH, 2 * CONV_DIM), f32)],
        compiler_params=_cparams("arbitrary"),
    )(sinks, z, z, z, z, z, rope_tab, rope_tab, g_attn, z, z, z, z, state_l, state_l, conv_w_l, g_conv, bd)


def _attn_sample_kernel(q_ref, k_ref, v_ref, ck_ref, cv_ref, tab_ref, sink_ref, g_ref, bd_ref,
                        an_ref, nk_ref, nv_ref):
    bt = q_ref.shape[0]
    w = ck_ref.shape[1]
    tab = tab_ref[...]
    q = _rope(q_ref[...], jnp.tile(tab[0:1, :128], (1, 8)), jnp.tile(tab[0:1, 128:], (1, 8)))
    k = _rope(k_ref[...], jnp.tile(tab[0:1, :128], (1, 2)), jnp.tile(tab[0:1, 128:], (1, 2)))
    v = v_ref[...]
    ck = ck_ref[...]
    cv = cv_ref[...]
    bd = bd_ref[...]

    last = lax.broadcasted_iota(jnp.int32, (w, KV_DIM), 0) == w - 1
    for b in range(bt):
        nk_ref[b] = jnp.where(last, k[b:b + 1], pltpu.roll(ck[b], w - 1, axis=0))
        nv_ref[b] = jnp.where(last, v[b:b + 1], pltpu.roll(cv[b], w - 1, axis=0))

    seg = lax.broadcasted_iota(jnp.int32, (bt, SEG), 1) // HEAD_DIM
    scale = HEAD_DIM ** -0.5
    chunks = [jnp.zeros((bt, SEG), f32) for _ in range(N_KV_HEADS)]
    gs = range(GROUP)
    qgs = []
    for g in gs:
        qg = jnp.zeros((bt, SEG), f32)
        for kvh in range(N_KV_HEADS):
            x = q[:, SEG * kvh:SEG * (kvh + 1)]
            sh = (HEAD_DIM * (kvh - g)) % SEG
            if sh:
                x = pltpu.roll(x, sh, axis=1)
            qg = jnp.where(seg == kvh, x, qg)
        qgs.append(qg)
    scs = [(_segsum((ck * qgs[g][:, None, :]).reshape(bt * w, SEG), bd) * scale).reshape(bt, w, SEG) for g in gs]
    sns = [_segsum(qgs[g] * k, bd) * scale for g in gs]
    sinks = [sink_ref[g:g + 1, :] for g in gs]
    mxs = [jnp.maximum(jnp.maximum(jnp.max(scs[g], axis=1), sns[g]), sinks[g]) for g in gs]
    ps = [jnp.exp(scs[g] - mxs[g][:, None, :]) for g in gs]
    pns = [jnp.exp(sns[g] - mxs[g]) for g in gs]
    dens = [jnp.sum(ps[g], axis=1) + pns[g] + jnp.exp(sinks[g] - mxs[g]) for g in gs]
    os_ = [(jnp.sum(ps[g] * cv, axis=1) + pns[g] * v) / dens[g] for g in gs]
    mss = [_segsum(os_[g] * os_[g], bd) * (1.0 / HEAD_DIM) for g in gs]
    for g in gs:
        on = os_[g] * lax.rsqrt(mss[g] + EPS)
        for kvh in range(N_KV_HEADS):
            sh = (HEAD_DIM * (g - kvh)) % SEG
            x = pltpu.roll(on, sh, axis=1) if sh else on
            chunks[kvh] = jnp.where(seg == g, x, chunks[kvh])
    out = jnp.concatenate(chunks, axis=1) * g_ref[...]
    an_ref[...] = out.astype(an_ref.dtype)


def _attn_sample(z, cache_k, cache_v, layer, tab, sinks_exp, g_attn, bd, bt=16):
    w = cache_k.shape[2]
    row0 = SAMPLE_ROW0 // bt
    kcol, vcol = ATTN_DIM // KV_DIM, ATTN_DIM // KV_DIM + 1
    cache_in = pl.BlockSpec((None, bt, w, KV_DIM), lambda i: (layer, i, 0, 0))
    cache_spec = pl.BlockSpec((bt, w, KV_DIM), lambda i: (i, 0, 0))
    return pl.pallas_call(
        _attn_sample_kernel,
        grid=(DEC_BATCH // bt,),
        in_specs=[pl.BlockSpec((bt, ATTN_DIM), lambda i: (row0 + i, 0)),
                  pl.BlockSpec((bt, KV_DIM), lambda i: (row0 + i, kcol)),
                  pl.BlockSpec((bt, KV_DIM), lambda i: (row0 + i, vcol)),
                  cache_in, cache_in,
                  pl.BlockSpec((8, 256), lambda i: (0, 0)),
                  pl.BlockSpec((GROUP, SEG), lambda i: (0, 0)),
                  pl.BlockSpec((1, ATTN_DIM), lambda i: (0, 0)),
                  pl.BlockSpec((SEG, SEG), lambda i: (0, 0))],
        out_specs=[pl.BlockSpec((bt, ATTN_DIM), lambda i: (i, 0)), cache_spec, cache_spec],
        out_shape=[jax.ShapeDtypeStruct((DEC_BATCH, ATTN_DIM), bf16),
                   jax.ShapeDtypeStruct(cache_k.shape[1:], f32),
                   jax.ShapeDtypeStruct(cache_v.shape[1:], f32)],
        compiler_params=_cparams("parallel"),
    )(z, z, z, cache_k, cache_v, tab, sinks_exp, g_attn, bd)


def _conv_block(is_sample, z1_ref, z2_ref, p1_ref, p2_ref, s0_ref, s1_ref, w_ref, g_ref, bd_ref,
                c_ref, utail_ref, ns_ref):
    def split(b1, b2):
        gb = b1[:, :CONV_DIM]
        gc = jnp.concatenate([b1[:, CONV_DIM:], b2[:, :CONV_DIM // 2]], axis=1)
        return gb, gc * b2[:, CONV_DIM // 2:]

    gb, u = split(z1_ref[...], z2_ref[...])
    _, up = split(p1_ref[...], p2_ref[...])
    utail_ref[...] = u[ROW_BLOCK - 8:]
    ns_ref[:, :CONV_DIM] = s1_ref[...]
    ns_ref[:, CONV_DIM:] = u

    row = lax.broadcasted_iota(jnp.int32, u.shape, 0)
    u1 = jnp.where(row == 0, up[7:8], pltpu.roll(u, 1, axis=0))
    u2 = jnp.where(row == 0, up[6:7], jnp.where(row == 1, up[7:8], pltpu.roll(u, 2, axis=0)))
    u1 = jnp.where(is_sample, s1_ref[...], u1)
    u2 = jnp.where(is_sample, s0_ref[...], u2)
    w = w_ref[...]
    y = u2 * w[0:1] + u1 * w[1:2] + u * w[2:3]
    t = gb * y
    bd = bd_ref[...]
    g = g_ref[...]
    cols = [slice(SEG * c, SEG * (c + 1)) for c in range(CONV_DIM // SEG)]
    mss = [_segsum(t[:, cs] * t[:, cs], bd) * (1.0 / HEAD_DIM) for cs in cols]
    for cs, ms in zip(cols, mss):
        c_ref[:, cs] = (t[:, cs] * lax.rsqrt(ms + EPS) * g[:, cs]).astype(c_ref.dtype)


def _conv_prev_block(i):
    last8_of_meta = N_ROWS // 8 - 1
    return jnp.where(i % BLOCKS_PER_SEQ == 0, last8_of_meta, (ROW_BLOCK // 8) * i - 1)


def _outproj_kernel(*refs, routed):
    if routed:
        (a_ref, as_ref, c_ref, w_ref, h_ref, g_ref, r_ref, tri_ref,
         ho_ref, no_ref, cmb_ref, dest_ref, wbf_ref, carry_ref) = refs
    else:
        a_ref, as_ref, c_ref, w_ref, h_ref, g_ref, ho_ref, no_ref, wbf_ref = refs
    i = pl.program_id(0)

    @pl.when(i == 0)
    def _():
        wbf_ref[...] = w_ref[...].astype(bf16)
        if routed:
            carry_ref[...] = jnp.zeros_like(carry_ref)

    tm = a_ref.shape[0]
    off = SAMPLE_ROW0 % tm
    a = a_ref[...]
    pieces = [jnp.zeros((off, ATTN_DIM), a.dtype)] if off else []
    pieces.append(as_ref[...])
    if tm - off - DEC_BATCH:
        pieces.append(jnp.zeros((tm - off - DEC_BATCH, ATTN_DIM), a.dtype))
    row = lax.broadcasted_iota(jnp.int32, a.shape, 0)
    is_decode = (i == SAMPLE_ROW0 // tm) & (row >= off) & (row < off + DEC_BATCH)
    a = jnp.where(is_decode, jnp.concatenate(pieces, axis=0), a)
    acc = jnp.dot(a, wbf_ref[:ATTN_DIM, :], preferred_element_type=f32)
    acc += jnp.dot(c_ref[...], wbf_ref[ATTN_DIM:, :], preferred_element_type=f32)
    hn = h_ref[...] + acc
    ho_ref[...] = hn
    n = _rms(hn, g_ref[...])
    no_ref[...] = n.astype(no_ref.dtype)
    if routed:
        _route_rows(n, i * tm, r_ref, tri_ref, cmb_ref, dest_ref, carry_ref)


def _route_rows(n, row0, r_ref, tri_ref, cmb_ref, dest_ref, carry_ref):
    tm = n.shape[0]
    rt = r_ref[...]
    idx = lax.broadcasted_iota(jnp.int32, (tm, N_EXPERTS), 1)
    logits = jnp.zeros((tm, N_EXPERTS), f32)
    for e in range(N_EXPERTS):
        prod = n * rt[e:e + 1, :]
        part = prod[:, :128]
        for c in range(1, D_MODEL // 128):
            part = part + prod[:, 128 * c:128 * (c + 1)]
        logits = jnp.where(idx == e, jnp.sum(part, axis=-1, keepdims=True), logits)
    m1 = jnp.max(logits, axis=-1, keepdims=True)
    i1 = jnp.min(jnp.where(logits == m1, idx, N_EXPERTS), axis=-1, keepdims=True)
    rest = jnp.where(idx == i1, -jnp.inf, logits)
    m2 = jnp.max(rest, axis=-1, keepdims=True)
    i2 = jnp.min(jnp.where(rest == m2, idx, N_EXPERTS), axis=-1, keepdims=True)
    e2 = jnp.exp(m2 - m1)
    den = 1.0 + e2
    rowid = row0 + lax.broadcasted_iota(jnp.int32, logits.shape, 0)
    real = (rowid < TAIL_ROW0) | (rowid >= N_ROWS - N_META)
    sel = ((idx == i1) | (idx == i2)) & real
    cmb_ref[...] = jnp.where(sel, jnp.where(idx == i1, 1.0 / den, e2 / den), 0.0)
    self = jnp.where(sel, 1.0, 0.0)
    incl = jnp.dot(tri_ref[...], self.astype(bf16), preferred_element_type=f32)
    carry = carry_ref[...]
    dest_ref[...] = jnp.where(sel, incl - self + carry, -1.0).astype(jnp.int32)
    carry_ref[...] = carry + incl[tm - 1:tm, :]


def _out_proj(an, an_decode, c, w_out, layer, h, g_next, router=None, tm=384):
    assert SAMPLE_ROW0 % tm + DEC_BATCH <= tm
    routed = router is not None
    row = lambda i: (i, 0)
    const = lambda i: (0, 0)
    in_specs = [pl.BlockSpec((tm, ATTN_DIM), row),
                pl.BlockSpec((DEC_BATCH, ATTN_DIM), const),
                pl.BlockSpec((tm, CONV_DIM), row),
                pl.BlockSpec((None, D_MODEL, D_MODEL), lambda i: (layer, 0, 0), pipeline_mode=pl.Buffered(1)),
                pl.BlockSpec((tm, D_MODEL), row),
                pl.BlockSpec((1, D_MODEL), const)]
    out_specs = [pl.BlockSpec((tm, D_MODEL), row), pl.BlockSpec((tm, D_MODEL), row)]
    out_shape = [jax.ShapeDtypeStruct((N_ROWS, D_MODEL), f32), jax.ShapeDtypeStruct((N_ROWS, D_MODEL), bf16)]
    scratch = [pltpu.VMEM((D_MODEL, D_MODEL), bf16)]
    args = [an, an_decode, c, w_out, h, g_next]
    if routed:
        tri = jnp.asarray(np.tril(np.ones((tm, tm), np.float32)), dtype=bf16)
        in_specs += [pl.BlockSpec((N_EXPERTS, D_MODEL), const), pl.BlockSpec((tm, tm), const)]
        out_specs += [pl.BlockSpec((tm, N_EXPERTS), row), pl.BlockSpec((tm, N_EXPERTS), row)]
        out_shape += [jax.ShapeDtypeStruct((N_ROWS, N_EXPERTS), f32),
                      jax.ShapeDtypeStruct((N_ROWS, N_EXPERTS), jnp.int32)]
        scratch.append(pltpu.VMEM((1, N_EXPERTS), f32))
        args += [router.T, tri]
    return pl.pallas_call(
        functools.partial(_outproj_kernel, routed=routed),
        grid=(N_ROWS // tm,),
        in_specs=in_specs,
        out_specs=out_specs,
        out_shape=out_shape,
        scratch_shapes=scratch,
        compiler_params=_cparams("arbitrary"),
    )(*args)


MOE_CHUNK = 256
MOE_GATHER = 768
MOE_SUB_SIZES = (640, 704, 736, 768)
MOE_SUBS_PER_VISIT = 3
MOE_VISIT_ROWS = MOE_SUBS_PER_VISIT * MOE_SUB_SIZES[-1]
MOE_CAP = 4 * MOE_VISIT_ROWS
MOE_TF = 256
N_CHUNKS = N_ROWS // MOE_CHUNK
N_GATHER = N_ROWS // MOE_GATHER
N_FSTEPS = D_FF // MOE_TF
MOE_MAX_VISITS = (2 * N_ROWS) // MOE_VISIT_ROWS + N_EXPERTS


def _moe_kernel(vis_e, vis_k, vis_nsub, vis_cls, vis_ok, rlo, xn_ref, dest_ref, w1_ref, w3_ref, w2_ref, y_ref, acc_ref):
    v, p = pl.program_id(0), pl.program_id(1)
    ok = vis_ok[v] == 1
    e = vis_e[v]
    base = vis_k[v] * MOE_VISIT_ROWS

    @pl.when(ok & (p == 0))
    def _():
        y_ref[...] = jnp.zeros_like(y_ref)

    @pl.when(ok & (p < N_GATHER))
    def _():
        lo = jnp.maximum(rlo[e * (N_GATHER + 1) + p] - base, 0)
        hi = jnp.minimum(rlo[e * (N_GATHER + 1) + p + 1] - base, MOE_VISIT_ROWS)

        @pl.when(hi > lo)
        def _():
            dest = dest_ref[...] - base
            win = lax.broadcasted_iota(jnp.int32, (MOE_CHUNK, MOE_GATHER), 0)

            def tile_body(t, carry):
                off = pl.multiple_of(t * MOE_CHUNK, MOE_CHUNK)
                onehot = jnp.where(win == dest - off, 1.0, 0.0).astype(bf16)
                rows = jnp.dot(onehot, xn_ref[...], preferred_element_type=f32)
                y_ref[pl.ds(off, MOE_CHUNK), :] += rows.astype(bf16)
                return carry

            t_first = lo // MOE_CHUNK
            t_last = (hi - 1) // MOE_CHUNK

            @pl.when(t_last - t_first < 2)
            def _():
                off = pl.multiple_of(jnp.minimum(t_first, MOE_VISIT_ROWS // MOE_CHUNK - 2) * MOE_CHUNK, MOE_CHUNK)
                win2 = lax.broadcasted_iota(jnp.int32, (2 * MOE_CHUNK, MOE_GATHER), 0)
                onehot = jnp.where(win2 == dest - off, 1.0, 0.0).astype(bf16)
                rows = jnp.dot(onehot, xn_ref[...], preferred_element_type=f32)
                y_ref[pl.ds(off, 2 * MOE_CHUNK), :] += rows.astype(bf16)

            @pl.when(t_last - t_first >= 2)
            def _():
                lax.fori_loop(t_first, t_last + 1, tile_body, 0)

    @pl.when(ok & (p >= N_GATHER))
    def _():
        f = p - N_GATHER
        w1 = w1_ref[...].astype(bf16)
        w3 = w3_ref[...].astype(bf16)
        w2 = w2_ref[...].astype(bf16)
        for cls, sub in enumerate(MOE_SUB_SIZES):
            @pl.when(vis_cls[v] == cls)
            def _():
                def sub_body(r, carry):
                    rows = pl.ds(pl.multiple_of(r * sub, 16), sub)

                    @pl.when(f == 0)
                    def _():
                        acc_ref[rows, :] = jnp.zeros((sub, D_MODEL), f32)

                    x = y_ref[rows, :]
                    a = jnp.dot(x, w1, preferred_element_type=f32)
                    b = jnp.dot(x, w3, preferred_element_type=f32)
                    act = (a * jax.nn.sigmoid(a) * b).astype(bf16)
                    acc_ref[rows, :] += jnp.dot(act, w2, preferred_element_type=f32)

                    @pl.when(f == N_FSTEPS - 1)
                    def _():
                        y_ref[rows, :] = acc_ref[rows, :].astype(y_ref.dtype)

                    return carry

                lax.fori_loop(0, vis_nsub[v], sub_body, 0)


def _moe_experts(xn, dest_exp, tables, w1, w3, w2, layer_idx):
    def chunk(v, p, ve, vk, vn, vc, vok, rlo):
        return jnp.where(vok[v] == 1, jnp.minimum(p, N_GATHER - 1), N_GATHER - 1)

    def fstep(v, p, ve, vk, vn, vc, vok, rlo):
        return jnp.where(vok[v] == 1, jnp.maximum(p - N_GATHER, 0), N_FSTEPS - 1)

    grid_spec = pltpu.PrefetchScalarGridSpec(
        num_scalar_prefetch=6,
        grid=(MOE_MAX_VISITS, N_GATHER + N_FSTEPS),
        in_specs=[pl.BlockSpec((MOE_GATHER, D_MODEL), lambda v, p, *t: (chunk(v, p, *t), 0)),
                  pl.BlockSpec((None, None, 1, MOE_GATHER), lambda v, p, *t: (t[0][v], chunk(v, p, *t), 0, 0)),
                  pl.BlockSpec((None, None, D_MODEL, MOE_TF), lambda v, p, *t: (layer_idx, t[0][v], 0, fstep(v, p, *t))),
                  pl.BlockSpec((None, None, D_MODEL, MOE_TF), lambda v, p, *t: (layer_idx, t[0][v], 0, fstep(v, p, *t))),
                  pl.BlockSpec((None, None, MOE_TF, D_MODEL), lambda v, p, *t: (layer_idx, t[0][v], fstep(v, p, *t), 0))],
        out_specs=pl.BlockSpec((None, MOE_VISIT_ROWS, D_MODEL), lambda v, p, *t: (t[0][v], t[1][v], 0),
                               pipeline_mode=pl.Buffered(1)),
        scratch_shapes=[pltpu.VMEM((MOE_VISIT_ROWS, D_MODEL), f32)],
    )
    return pl.pallas_call(
        _moe_kernel,
        grid_spec=grid_spec,
        out_shape=jax.ShapeDtypeStruct((N_EXPERTS, MOE_CAP, D_MODEL), bf16),
        compiler_params=_cparams("arbitrary", "arbitrary"),
    )(*tables, xn, dest_exp, w1, w3, w2)


def _combine_kernel(t0, nwin, h_ref, dest_ref, cmb_ref, g_ref, *rest):
    ywin = rest[:2 * N_EXPERTS]
    yp_ref, ys_ref, acc_ref = rest[2 * N_EXPERTS:]
    c = pl.program_id(0)
    acc_ref[...] = h_ref[...]
    dest = dest_ref[...]
    cmb = cmb_ref[...]
    lane = lax.broadcasted_iota(jnp.int32, (MOE_CHUNK, MOE_CHUNK), 1)
    for e in range(N_EXPERTS):
        for w in range(2):
            @pl.when(nwin[e * N_CHUNKS + c] > w)
            def _():
                rel = dest[:, e:e + 1] - MOE_CHUNK * (t0[e * N_CHUNKS + c] + w)
                onehot = jnp.where(rel == lane, 1.0, 0.0).astype(bf16)
                got = jnp.dot(onehot, ywin[2 * e + w][...], preferred_element_type=f32)
                acc_ref[...] += cmb[:, e:e + 1] * got

    n = _rms(acc_ref[...], g_ref[...])
    n_prompt_chunks = N_PROMPT_ROWS // MOE_CHUNK

    @pl.when(c < n_prompt_chunks)
    def _():
        yp_ref[...] = n

    @pl.when(c == n_prompt_chunks)
    def _():
        ys_ref[...] = n[:DEC_BATCH]


def _moe_combine(y_sorted, win_tables, h, dest_tok, cmb, g_final):
    t0, t0c, t1c, nwin = win_tables
    row = lambda c, *t: (c, 0)
    n_prompt_chunks = N_PROMPT_ROWS // MOE_CHUNK
    wins = []
    for e in range(N_EXPERTS):
        wins.append(pl.BlockSpec((None, MOE_CHUNK, D_MODEL), lambda c, a, b, *t, e=e: (e, a[e * N_CHUNKS + c], 0)))
        wins.append(pl.BlockSpec((None, MOE_CHUNK, D_MODEL), lambda c, a, b, *t, e=e: (e, b[e * N_CHUNKS + c], 0)))
    grid_spec = pltpu.PrefetchScalarGridSpec(
        num_scalar_prefetch=4,
        grid=(N_CHUNKS,),
        in_specs=[pl.BlockSpec((MOE_CHUNK, D_MODEL), row),
                  pl.BlockSpec((MOE_CHUNK, N_EXPERTS), row),
                  pl.BlockSpec((MOE_CHUNK, N_EXPERTS), row),
                  pl.BlockSpec((1, D_MODEL), lambda c, *t: (0, 0))] + wins,
        out_specs=[pl.BlockSpec((MOE_CHUNK, D_MODEL), lambda c, *t: (jnp.minimum(c, n_prompt_chunks - 1), 0)),
                   pl.BlockSpec((DEC_BATCH, D_MODEL), lambda c, *t: (0, 0))],
        scratch_shapes=[pltpu.VMEM((MOE_CHUNK, D_MODEL), f32)],
    )

    def body(t0c_ref, t1c_ref, t0_ref, nwin_ref, *refs):
        _combine_kernel(t0_ref, nwin_ref, *refs)

    return pl.pallas_call(
        body,
        grid_spec=grid_spec,
        out_shape=[jax.ShapeDtypeStruct((N_PROMPT_ROWS, D_MODEL), f32),
                   jax.ShapeDtypeStruct((DEC_BATCH, D_MODEL), f32)],
        compiler_params=_cparams("arbitrary"),
    )(t0c, t1c, t0, nwin, h, dest_tok, cmb, g_final, *([y_sorted] * (2 * N_EXPERTS)))


def _routing_tables(dest_tok):
    i32 = jnp.int32
    cnt = (dest_tok >= 0).astype(i32).reshape(N_CHUNKS, MOE_CHUNK, N_EXPERTS).sum(axis=1)
    csum = jnp.cumsum(cnt, axis=0)
    rank_lo = jnp.concatenate([jnp.zeros((1, N_EXPERTS), i32), csum], axis=0).T
    n_e = csum[-1]
    nv = (n_e + MOE_VISIT_ROWS - 1) // MOE_VISIT_ROWS
    cum_nv = jnp.cumsum(nv)
    total = cum_nv[-1]
    v = jnp.arange(MOE_MAX_VISITS, dtype=i32)
    vv = jnp.minimum(v, jnp.maximum(total - 1, 0))
    vis_e = jnp.minimum(jnp.sum((vv[:, None] >= cum_nv[None, :]).astype(i32), axis=1), N_EXPERTS - 1)
    vis_k = vv - (cum_nv - nv)[vis_e]
    vis_ok = (v < total).astype(i32)
    rows = jnp.clip(n_e[vis_e] - vis_k * MOE_VISIT_ROWS, 0, MOE_VISIT_ROWS)
    sizes = jnp.asarray(MOE_SUB_SIZES, i32)
    n_sub = (rows[:, None] + sizes[None, :] - 1) // sizes[None, :]
    padded = jnp.where(n_sub <= MOE_SUBS_PER_VISIT, n_sub * sizes[None, :], 2 * MOE_VISIT_ROWS)
    vis_cls = jnp.argmin(padded, axis=1).astype(i32)
    vis_nsub = jnp.where(vis_ok == 1, jnp.take_along_axis(n_sub, vis_cls[:, None], axis=1)[:, 0], 0)
    t0 = rank_lo[:, :N_CHUNKS] // MOE_CHUNK
    tmax = jnp.maximum(nv * (MOE_VISIT_ROWS // MOE_CHUNK) - 1, 0)[:, None]
    moe_tables = (vis_e, vis_k, vis_nsub.astype(i32), vis_cls, vis_ok,
                  rank_lo[:, ::MOE_GATHER // MOE_CHUNK].reshape(-1))
    t_last = (rank_lo[:, 1:] - 1) // MOE_CHUNK
    nwin = jnp.where(cnt.T > 0, t_last - t0 + 1, 0)
    chunk_ids = jnp.arange(N_CHUNKS, dtype=i32)[None, :]
    last_two = lax.cummax(jnp.where(nwin == 2, chunk_ids, 0), axis=1)
    t1c = jnp.take_along_axis(jnp.minimum(t0 + 1, tmax), last_two, axis=1)
    win_tables = (t0.reshape(-1), jnp.minimum(t0, tmax).reshape(-1), t1c.reshape(-1), nwin.reshape(-1))
    return moe_tables, win_tables


def _ffn_kernel(x_ref, w1_ref, w3_ref, w2_ref, h_ref, g_ref, ho_ref, no_ref):
    f = pl.program_id(1)

    @pl.when(f == 0)
    def _():
        ho_ref[...] = h_ref[...]

    x = x_ref[...]
    a = jnp.dot(x, w1_ref[...].astype(bf16), preferred_element_type=f32)
    b = jnp.dot(x, w3_ref[...].astype(bf16), preferred_element_type=f32)
    act = (a * jax.nn.sigmoid(a) * b).astype(bf16)
    ho_ref[...] += jnp.dot(act, w2_ref[...].astype(bf16), preferred_element_type=f32)

    @pl.when(f == pl.num_programs(1) - 1)
    def _():
        no_ref[...] = _rms(ho_ref[...], g_ref[...]).astype(no_ref.dtype)


def _ffn(xn, w1, w3, w2, layer_idx, h, g_next, tm=768, tf=512):
    row = lambda i, f: (i, 0)
    once = pl.Buffered(1)
    return pl.pallas_call(
        _ffn_kernel,
        grid=(N_ROWS // tm, D_FF // tf),
        in_specs=[pl.BlockSpec((tm, D_MODEL), row),
                  pl.BlockSpec((None, D_MODEL, tf), lambda i, f: (layer_idx, 0, f)),
                  pl.BlockSpec((None, D_MODEL, tf), lambda i, f: (layer_idx, 0, f)),
                  pl.BlockSpec((None, tf, D_MODEL), lambda i, f: (layer_idx, f, 0)),
                  pl.BlockSpec((tm, D_MODEL), row, pipeline_mode=once),
                  pl.BlockSpec((1, D_MODEL), lambda i, f: (0, 0))],
        out_specs=[pl.BlockSpec((tm, D_MODEL), row, pipeline_mode=once),
                   pl.BlockSpec((tm, D_MODEL), row, pipeline_mode=once)],
        out_shape=[jax.ShapeDtypeStruct((N_ROWS, D_MODEL), f32),
                   jax.ShapeDtypeStruct((N_ROWS, D_MODEL), bf16)],
        compiler_params=_cparams("parallel", "arbitrary"),
    )(xn, w1, w3, w2, h, g_next)


def _rope_patterns(pos):
    half = ROT_DIM // 2
    inv_freq = jnp.power(ROPE_THETA, -jnp.arange(half, dtype=f32) * 2.0 / ROT_DIM)
    ang = pos.astype(f32)[:, None] * inv_freq[None, :]
    cos, sin = jnp.cos(ang), jnp.sin(ang)
    n = pos.shape[0]
    cos_h = jnp.concatenate([cos, cos, jnp.ones((n, HEAD_DIM - ROT_DIM), f32)], axis=1)
    sin_h = jnp.concatenate([-sin, sin, jnp.zeros((n, HEAD_DIM - ROT_DIM), f32)], axis=1)
    return jnp.concatenate([cos_h, cos_h, sin_h, sin_h], axis=1)


def _block_diag_ones():
    i = np.arange(SEG) // HEAD_DIM
    return jnp.asarray((i[:, None] == i[None, :]).astype(np.float32), dtype=bf16)


def kernel(x_prompt, x_sample, cache_k, cache_v, state_conv, meta_tokens, g_mix, w_in, conv_w, attn_sinks,
           g_attn_out, g_conv_out, w_out, g_ffn, dense_w1, dense_w3, dense_w2, moe_router, moe_w1, moe_w3,
           moe_w2, g_final):
    w_cache = cache_k.shape[2]
    tail = jnp.concatenate([x_sample.reshape(DEC_BATCH, D_MODEL), jnp.zeros((META_ROW_IN_BLOCK, D_MODEL), f32),
                            meta_tokens.astype(f32)], axis=0)

    prompt_pos = N_META + jnp.arange(SEQ, dtype=jnp.int32)
    tail_pos = jnp.maximum(jnp.arange(ROW_BLOCK, dtype=jnp.int32) - META_ROW_IN_BLOCK, 0)
    rope_tab = _rope_patterns(jnp.concatenate([prompt_pos, tail_pos]))
    sample_tab = _rope_patterns(jnp.full((8,), PAST_LEN, jnp.int32))
    bd = _block_diag_ones()
    cache_k = cache_k.reshape(DEPTH, DEC_BATCH, w_cache, KV_DIM)
    cache_v = cache_v.reshape(DEPTH, DEC_BATCH, w_cache, KV_DIM)

    h, xn = _norm_rows(x_prompt.reshape(N_PROMPT_ROWS, D_MODEL), tail, g_mix[0:1])
    nk_p, nv_p, nc_p, nk_s, nv_s, nc_s = [], [], [], [], [], []
    for l in range(DEPTH):
        z = _in_proj(xn, w_in, l)
        g_attn = g_attn_out[l:l + 1]
        state_l = state_conv[l].reshape(DEC_BATCH, (CONV_WIDTH - 1) * CONV_DIM)
        an, nk_prompt, nv_prompt, c, u_tail, conv_state = _mixer(z, attn_sinks[l], rope_tab, g_attn, state_l,
                                                                 conv_w[l], g_conv_out[l:l + 1], bd)
        sinks_exp = jnp.repeat(attn_sinks[l].reshape(N_KV_HEADS, GROUP).T, HEAD_DIM, axis=1)
        an_decode, nk, nv = _attn_sample(z, cache_k, cache_v, l, sample_tab, sinks_exp, g_attn, bd)
        j = l // 2
        if l % 2 == 0:
            h, xn = _out_proj(an, an_decode, c, w_out, l, h, g_ffn[l:l + 1])
            h, xn = _ffn(xn, dense_w1, dense_w3, dense_w2, j, h, g_mix[l + 1:l + 2])
        else:
            h, xn, cmb, dest_tok = _out_proj(an, an_decode, c, w_out, l, h, g_ffn[l:l + 1], router=moe_router[j])
            moe_tables, win_tables = _routing_tables(dest_tok)
            dest_exp = dest_tok.T.reshape(N_EXPERTS, N_GATHER, 1, MOE_GATHER)
            y_sorted = _moe_experts(xn, dest_exp, moe_tables, moe_w1, moe_w3, moe_w2, j)
            y_prompt, y_sample = _moe_combine(y_sorted, win_tables, h, dest_tok, cmb, g_final[None])

        nk_p.append(nk_prompt[:BATCH * WINDOW].reshape(BATCH, WINDOW, N_KV_HEADS, HEAD_DIM))
        nv_p.append(nv_prompt[:BATCH * WINDOW].reshape(BATCH, WINDOW, N_KV_HEADS, HEAD_DIM))
        nc_p.append(u_tail[:BATCH * 8].reshape(BATCH, 8, CONV_DIM)[:, 8 - (CONV_WIDTH - 1):])
        nk_s.append(nk.reshape(DEC_BATCH, w_cache, N_KV_HEADS, HEAD_DIM))
        nv_s.append(nv.reshape(DEC_BATCH, w_cache, N_KV_HEADS, HEAD_DIM))
        nc_s.append(conv_state.reshape(DEC_BATCH, CONV_WIDTH - 1, CONV_DIM))

    return (y_prompt.reshape(BATCH, SEQ, D_MODEL), y_sample.reshape(DEC_BATCH, 1, D_MODEL),
            jnp.stack(nk_p), jnp.stack(nv_p), jnp.stack(nc_p), jnp.stack(nk_s), jnp.stack(nv_s), jnp.stack(nc_s))
```
